```python
import math
import jax, jax.numpy as jnp
from jax import lax
import numpy as np

D_MODEL = 1024
BATCH = 4
SEQ = 4096
DEPTH = 4
DEC_BATCH = 128
DEC_SEQ = 1
PAST_LEN = 8192
PAGE_SIZE = 128

N_A_LAYERS = DEPTH // 2
N_B_LAYERS = DEPTH - N_A_LAYERS
HG_HEADS = 8
HG_DK = D_MODEL // HG_HEADS
HG_DV = D_MODEL // HG_HEADS
HG_CHUNK = 64
N_Q_HEADS = 16
N_KV_HEADS = 4
GROUP = N_Q_HEADS // N_KV_HEADS
HEAD_DIM = D_MODEL // N_Q_HEADS
WINDOW = 128
D_FF = 4 * D_MODEL
ROPE_THETA = 10000.0
EPS = 1e-6
ATTN_SCALE = 1.0 / math.sqrt(HEAD_DIM)

kernel_name = "yoco_hgrn2_swa_sink_decoder_step"

F32 = jnp.float32


def _rms_norm(x, g):
    xf = x.astype(F32)
    y = xf * lax.rsqrt(jnp.mean(xf * xf, axis=-1, keepdims=True) + EPS)
    return (y * g.astype(F32)).astype(x.dtype)


def _ada(c, w, b):
    return (jax.nn.silu(c) @ w + b)[:, None, :]


def _rope(x, pos):
    half = HEAD_DIM // 2
    inv = ROPE_THETA ** (-jnp.arange(half, dtype=F32) / half)
    ang = pos.astype(F32)[:, None] * inv[None, :]
    cos = jnp.cos(ang)[None, :, None, :]
    sin = jnp.sin(ang)[None, :, None, :]
    xf = x.astype(F32)
    x1, x2 = xf[..., :half], xf[..., half:]
    return jnp.concatenate([x1 * cos - x2 * sin, x2 * cos + x1 * sin], axis=-1).astype(x.dtype)


def _hgrn2_scan(q, k, v, logf, s0):
    b, L = q.shape[:2]
    C = HG_CHUNK if L % HG_CHUNK == 0 else L
    n = L // C

    def to_chunks(t):
        return t.astype(F32).reshape(b, n, C, HG_HEADS, t.shape[-1]).transpose(1, 0, 3, 2, 4)

    qc, kc, vc, gc = to_chunks(q), to_chunks(k), to_chunks(v), to_chunks(logf)
    causal = jnp.tril(jnp.ones((C, C), dtype=bool))[:, :, None]

    def step(S, inp):
        qi, ki, vi, gi = inp
        cum = jnp.cumsum(gi, axis=-2)
        diff = cum[..., :, None, :] - cum[..., None, :, :]
        decay = jnp.exp(jnp.where(causal, diff, -jnp.inf))
        att = jnp.einsum('bhtd,bhsd,bhtsd->bhts', qi, ki, decay)
        o = (jnp.einsum('bhts,bhsv->bhtv', att, vi)
             + jnp.einsum('bhtd,bhdv->bhtv', qi * jnp.exp(cum), S))
        last = cum[..., -1:, :]
        S_new = (jnp.exp(last[..., 0, :])[..., None] * S
                 + jnp.einsum('bhsd,bhsv->bhdv', ki * jnp.exp(last - cum), vi))
        return S_new, o

    S_fin, o = lax.scan(step, s0.astype(F32), (qc, kc, vc, gc))
    o = o.transpose(1, 0, 3, 2, 4).reshape(b, L, HG_HEADS, HG_DV)
    return o, S_fin


def _hgrn2(h, s0, w_in, w_out, lb, gn_g):
    b, L, _ = h.shape
    q, f, i, g = jnp.split(h @ w_in, 4, axis=-1)
    q = jax.nn.silu(q)
    fg = lb + (1.0 - lb) * jax.nn.sigmoid(f.astype(F32))
    heads = lambda t: t.reshape(b, L, HG_HEADS, -1)
    o, S = _hgrn2_scan(heads(q), heads(1.0 - fg), heads(i), heads(jnp.log(fg)), s0)
    o = _rms_norm(o, gn_g) * jax.nn.silu(heads(g).astype(F32))
    return o.reshape(b, L, D_MODEL).astype(h.dtype) @ w_out, S


def _sink_softmax(s, sinks):
    sk = jnp.broadcast_to(sinks.astype(F32).reshape(N_KV_HEADS, GROUP, 1, 1), s.shape[:-1] + (1,))
    p = jax.nn.softmax(jnp.concatenate([s, sk], axis=-1), axis=-1)
    return p[..., :-1]


def _swa_prompt(q, k, v, sinks):
    b, L = q.shape[:2]
    nb = L // WINDOW
    qb = q.reshape(b, nb, WINDOW, N_KV_HEADS, GROUP, HEAD_DIM).astype(F32)

    def band(t):
        tp = jnp.concatenate([jnp.zeros_like(t[:, :WINDOW]), t], axis=1)
        tb = tp.reshape(b, nb + 1, WINDOW, N_KV_HEADS, HEAD_DIM)
        return jnp.concatenate([tb[:, :-1], tb[:, 1:]], axis=2).astype(F32)

    kb, vb = band(k), band(v)
    s = jnp.einsum('bnqkgd,bnjkd->bnkgqj', qb, kb) * ATTN_SCALE
    qi = jnp.arange(WINDOW)[:, None] + WINDOW
    kj = jnp.arange(2 * WINDOW)[None, :]
    rel = qi - kj
    valid = (rel >= 0) & (rel < WINDOW)
    not_first = (jnp.arange(nb) > 0)[:, None, None]
    valid = valid[None] & (not_first | (kj >= WINDOW)[None])
    s = jnp.where(valid[None, :, None, None], s, -jnp.inf)
    p = _sink_softmax(s, sinks)
    o = jnp.einsum('bnkgqj,bnjkd->bnqkgd', p, vb)
    return o.reshape(b, L, N_Q_HEADS * HEAD_DIM)


def _swa_sample(q, k_all, v_all, sinks):
    b, T = q.shape[:2]
    qpos = PAST_LEN + jnp.arange(T)
    kpos = PAST_LEN - WINDOW + jnp.arange(WINDOW + T)
    rel = qpos[:, None] - kpos[None, :]
    valid = (rel >= 0) & (rel < WINDOW)
    qg = q.reshape(b, T, N_KV_HEADS, GROUP, HEAD_DIM).astype(F32)
    s = jnp.einsum('bqkgd,bjkd->bkgqj', qg, k_all.astype(F32)) * ATTN_SCALE
    s = jnp.where(valid, s, -jnp.inf)
    p = _sink_softmax(s, sinks)
    o = jnp.einsum('bkgqj,bjkd->bqkgd', p, v_all.astype(F32))
    return o.reshape(b, T, N_Q_HEADS * HEAD_DIM)


def _trunk(x, c, pos, hg_s0, cache_k, cache_v, P):
    (w_ada, b_ada, norm1_g, norm2_g, hg_w_in, hg_w_out, hg_lb, hg_gn_g,
     kv_w_ada, kv_b_ada, kv_norm_g, w_kv, k_norm_g,
     w_q, q_norm_g, sinks, w_o, w_up, w_down) = P
    b, L, _ = x.shape
    hg_states = []
    k_all = v_all = k_state = v_state = None
    for l in range(DEPTH):
        sh1, sc1, g1, sh2, sc2, g2 = jnp.split(_ada(c, w_ada[l], b_ada[l]), 6, axis=-1)
        if l == N_A_LAYERS:
            sh, sc = jnp.split(_ada(c, kv_w_ada, kv_b_ada), 2, axis=-1)
            hk = _rms_norm(x, kv_norm_g) * (1 + sc) + sh
            k_new, v_new = jnp.split(hk @ w_kv, 2, axis=-1)
            k_new = _rope(_rms_norm(k_new.reshape(b, L, N_KV_HEADS, HEAD_DIM), k_norm_g), pos)
            v_new = v_new.reshape(b, L, N_KV_HEADS, HEAD_DIM)
            if cache_k is None:
                k_all, v_all = k_new, v_new
            else:
                k_all = jnp.concatenate([cache_k.astype(k_new.dtype), k_new], axis=1)
                v_all = jnp.concatenate([cache_v.astype(v_new.dtype), v_new], axis=1)
            k_state, v_state = k_all[:, -WINDOW:], v_all[:, -WINDOW:]
        h = _rms_norm(x, norm1_g[l]) * (1 + sc1) + sh1
        if l < N_A_LAYERS:
            mix, S = _hgrn2(h, hg_s0[l], hg_w_in[l], hg_w_out[l], hg_lb[l], hg_gn_g[l])
            hg_states.append(S.astype(x.dtype))
        else:
            j = l - N_A_LAYERS
            q = _rope(_rms_norm((h @ w_q[j]).reshape(b, L, N_Q_HEADS, HEAD_DIM), q_norm_g[j]), pos)
            if cache_k is None:
                att = _swa_prompt(q, k_all, v_all, sinks[j])
            else:
                att = _swa_sample(q, k_all, v_all, sinks[j])
            mix = att.astype(x.dtype) @ w_o[j]
        x = x + g1 * mix
        h2 = _rms_norm(x, norm2_g[l]) * (1 + sc2) + sh2
        x = x + g2 * (jnp.square(jax.nn.relu(h2 @ w_up[l])) @ w_down[l])
    return x, jnp.stack(hg_states), k_state, v_state


def setup_inputs(seed: int = 0) -> dict:
    key = jax.random.key(seed)
    ks = jax.random.split(key, 32)
    n = lambda k, shape, s: jax.random.normal(k, shape, dtype=F32) * s
    D = D_MODEL
    return {
        "x_prompt": n(ks[0], (BATCH, SEQ, D), 1.0),
        "x_sample": n(ks[1], (DEC_BATCH, DEC_SEQ, D), 1.0),
        "c_prompt": n(ks[2], (BATCH, D), 1.0),
        "c_sample": n(ks[3], (DEC_BATCH, D), 1.0),
        "state_hgrn": n(ks[4], (N_A_LAYERS, DEC_BATCH, HG_HEADS, HG_DK, HG_DV), 0.3),
        "cache_k": n(ks[5], (DEC_BATCH, WINDOW, N_KV_HEADS, HEAD_DIM), 1.0),
        "cache_v": n(ks[6], (DEC_BATCH, WINDOW, N_KV_HEADS, HEAD_DIM), 1.0),
        "w_ada": n(ks[7], (DEPTH, D, 6 * D), 0.5 * D ** -0.5),
        "b_ada": n(ks[8], (DEPTH, 6 * D), 0.02),
        "norm1_g": 1.0 + n(ks[9], (DEPTH, D), 0.02),
        "norm2_g": 1.0 + n(ks[10], (DEPTH, D), 0.02),
        "hg_w_in": n(ks[11], (N_A_LAYERS, D, 4 * D), D ** -0.5),
        "hg_w_out": n(ks[12], (N_A_LAYERS, D, D), D ** -0.5),
        "hg_lower_bounds": n(ks[13], (N_A_LAYERS, D), 0.1),
        "hg_gn_g": 1.0 + n(ks[14], (N_A_LAYERS, HG_DV), 0.02),
        "kv_w_ada": n(ks[15], (D, 2 * D), 0.5 * D ** -0.5),
        "kv_b_ada": n(ks[16], (2 * D,), 0.02),
        "kv_norm_g": 1.0 + n(ks[17], (D,), 0.02),
        "w_kv": n(ks[18], (D, 2 * N_KV_HEADS * HEAD_DIM), D ** -0.5),
        "k_norm_g": 1.0 + n(ks[19], (HEAD_DIM,), 0.02),
        "w_q": n(ks[20], (N_B_LAYERS, D, N_Q_HEADS * HEAD_DIM), D ** -0.5),
        "q_norm_g": 1.0 + n(ks[21], (N_B_LAYERS, HEAD_DIM), 0.02),
        "sinks": n(ks[22], (N_B_LAYERS, N_Q_HEADS), 0.5),
        "w_o": n(ks[23], (N_B_LAYERS, N_Q_HEADS * HEAD_DIM, D), (N_Q_HEADS * HEAD_DIM) ** -0.5),
        "w_up": n(ks[24], (DEPTH, D, D_FF), D ** -0.5),
        "w_down": n(ks[25], (DEPTH, D_FF, D), D_FF ** -0.5),
    }


def reference(x_prompt, x_sample, c_prompt, c_sample, state_hgrn, cache_k, cache_v,
              w_ada, b_ada, norm1_g, norm2_g, hg_w_in, hg_w_out, hg_lower_bounds, hg_gn_g,
              kv_w_ada, kv_b_ada, kv_norm_g, w_kv, k_norm_g,
              w_q, q_norm_g, sinks, w_o, w_up, w_down):
    sm = jax.nn.softmax(hg_lower_bounds.astype(F32), axis=0)
    hg_lb = jnp.cumsum(sm, axis=0) - sm[0]
    P = (w_ada, b_ada, norm1_g, norm2_g, hg_w_in, hg_w_out, hg_lb, hg_gn_g,
         kv_w_ada, kv_b_ada, kv_norm_g, w_kv, k_norm_g,
         w_q, q_norm_g, sinks, w_o, w_up, w_down)
    bp, Lp, _ = x_prompt.shape
    s0_prompt = jnp.zeros((N_A_LAYERS, bp, HG_HEADS, HG_DK, HG_DV), dtype=x_prompt.dtype)
    y_prompt, hg_p, k_p, v_p = _trunk(x_prompt, c_prompt, jnp.arange(Lp), s0_prompt,
                                      None, None, P)
    Ls = x_sample.shape[1]
    y_sample, hg_s, k_s, v_s = _trunk(x_sample, c_sample, PAST_LEN + jnp.arange(Ls), state_hgrn,
                                      cache_k, cache_v, P)
    return (y_prompt, y_sample, hg_p, k_p, v_p, hg_s, k_s, v_s)
```

```python
import functools
import math

import numpy as np
import jax
import jax.numpy as jnp
from jax import lax
from jax.experimental import pallas as pl
from jax.experimental.pallas import tpu as pltpu

F32 = jnp.float32
BF16 = jnp.bfloat16

PAST_LEN = 8192
ROPE_THETA = 10000.0
EPS = 1e-6
LANES = 128
VMEM_LIMIT = 56 * 1024 * 1024

NT_DIMS = (((1,), (1,)), ((), ()))
TN_DIMS = (((0,), (0,)), ((), ()))


def _dot(a, b):
    return jnp.dot(a, b, preferred_element_type=F32)


def _sigmoid(x):
    return 1.0 / (1.0 + jnp.exp(-x))


def _rms(x, g):
    ms = jnp.mean(x * x, axis=-1, keepdims=True)
    return x * lax.rsqrt(ms + EPS) * g


def _modulated_norm(x, gain, shift, scale):
    return _rms(x, gain) * (1.0 + scale) + shift


def _params(*sem):
    return pltpu.CompilerParams(dimension_semantics=sem, vmem_limit_bytes=VMEM_LIMIT)


def _full(shape):
    n = len(shape)
    return pl.BlockSpec(shape, lambda *_: (0,) * n)


def _ada_kernel(c_ref, w_ref, b_ref, o_ref):
    c = c_ref[...]
    a = (c * _sigmoid(c)).astype(BF16)
    o_ref[0] = _dot(a, w_ref[0].astype(BF16)) + b_ref[0]


def _ada(c_all, w, b, tn):
    nl, d, n = w.shape
    r = c_all.shape[0]
    return pl.pallas_call(
        _ada_kernel,
        grid=(nl, n // tn),
        in_specs=[pl.BlockSpec((r, d), lambda l, j: (0, 0)),
                  pl.BlockSpec((1, d, tn), lambda l, j: (l, 0, j)),
                  pl.BlockSpec((1, 1, tn), lambda l, j: (l, 0, j))],
        out_specs=pl.BlockSpec((1, r, tn), lambda l, j: (l, 0, j)),
        out_shape=jax.ShapeDtypeStruct((nl, r, n), F32),
        compiler_params=_params("arbitrary", "arbitrary"),
        name="ada",
    )(c_all, w, b.reshape(nl, 1, n))


def _group_mean_matrix(group):
    r = lax.broadcasted_iota(jnp.int32, (LANES, LANES), 0) // group
    c = lax.broadcasted_iota(jnp.int32, (LANES, LANES), 1) // group
    return jnp.where(r == c, 1.0 / group, 0.0).astype(BF16)


def _head_norm_rope(y, gain, cos, sin, head_dim, out_scale):
    ms = _dot((y * y).astype(BF16), _group_mean_matrix(head_dim))
    yn = y * lax.rsqrt(ms + EPS) * gain
    half = head_dim // 2
    lane = lax.broadcasted_iota(jnp.int32, yn.shape, 1)
    first = (lane % head_dim) < half
    rot = jnp.where(first, pltpu.roll(yn, LANES - half, 1), pltpu.roll(yn, half, 1))
    out = yn * cos + rot * sin
    if out_scale != 1.0:
        out = out * out_scale
    return out


def _lower_bound(raw, layer):
    m = jnp.max(raw, axis=0, keepdims=True)
    e = jnp.exp(raw - m)
    sm = e / jnp.sum(e, axis=0, keepdims=True)
    acc = sm[0:1]
    for j in range(1, layer + 1):
        acc = acc + sm[j:j + 1]
    return acc - sm[0:1]


def _hgrn_gates(proj, lb, d):
    qa = proj[:, 0:d]
    q = qa * _sigmoid(qa)
    fg = lb + (1.0 - lb) * _sigmoid(proj[:, d:2 * d])
    return q, fg


def _gated_group_norm(o, gain, gate_pre):
    ms = jnp.mean(o * o, axis=-1, keepdims=True)
    return o * lax.rsqrt(ms + EPS) * gain * (gate_pre * _sigmoid(gate_pre))


def _block_ref(c, level):
    rows, w = c.shape
    m = 1 << level
    if 2 * m >= 8:
        c3 = c.reshape(rows // (2 * m), 2 * m, w)
        return jnp.broadcast_to(c3[:, m - 1:m, :], c3.shape).reshape(rows, w)
    c3 = c.reshape(rows // 8, 8, w)
    sub = lax.broadcasted_iota(jnp.int32, c3.shape, 1)
    row = lambda j: jnp.broadcast_to(c3[:, j:j + 1, :], c3.shape)
    if m == 2:
        r = jnp.where(sub < 4, row(1), row(5))
    else:
        r = jnp.where(sub < 2, row(0), jnp.where(sub < 4, row(2), jnp.where(sub < 6, row(4), row(6))))
    return r.reshape(rows, w)


def _hgrn_prompt_kernel(layer, heads, x_ref, mod_ref, n1g_ref, win_ref, lbraw_ref, gng_ref,
                        lev_ref, tri_ref, o_ref, s_ref, st_scr):
    c = pl.program_id(1)
    rows, d = x_ref.shape[1], x_ref.shape[2]
    dk = d // heads
    n_levels = int(math.log2(rows))

    @pl.when(c == 0)
    def _():
        st_scr[...] = jnp.zeros_like(st_scr)

    mod = mod_ref[0]
    h = _modulated_norm(x_ref[0], n1g_ref[...], mod[:, 0:d], mod[:, d:2 * d])
    proj = _dot(h.astype(BF16), win_ref[...])
    lb = _lower_bound(lbraw_ref[...], layer)
    q, fg = _hgrn_gates(proj, lb, d)
    k = 1.0 - fg
    logf = jnp.log(fg)
    hi = logf.astype(BF16)
    lo = (logf - hi.astype(F32)).astype(BF16)
    tri = tri_ref[...]
    cum = _dot(tri, hi) + _dot(tri, lo)
    lev = lev_ref[...]
    row_id = lax.broadcasted_iota(jnp.int32, (rows, dk), 0)
    upper = [((row_id >> p) & 1) == 1 for p in range(n_levels)]
    gng = gng_ref[...]

    for hh in range(heads):
        sl = slice(hh * dk, (hh + 1) * dk)
        q_h, k_h, cum_h = q[:, sl], k[:, sl], cum[:, sl]
        v_h = proj[:, 2 * d + hh * dk:2 * d + (hh + 1) * dk].astype(BF16)
        att = jnp.where(lev == -1, lax.dot_general(q_h.astype(BF16), k_h.astype(BF16), NT_DIMS,
                                                   preferred_element_type=F32), 0.0)
        for p in range(n_levels):
            e = jnp.exp(-jnp.abs(cum_h - _block_ref(cum_h, p)))
            xs = (jnp.where(upper[p], q_h, k_h) * e).astype(BF16)
            pp = lax.dot_general(xs, xs, NT_DIMS, preferred_element_type=F32)
            att = jnp.where(lev == p, pp, att)
        cum_last = cum_h[rows - 1:rows, :]
        q_dec = (q_h * jnp.exp(cum_h)).astype(BF16)
        k_dec = (k_h * jnp.exp(cum_last - cum_h)).astype(BF16)
        st = st_scr[hh]
        o_h = _dot(att.astype(BF16), v_h) + lax.dot_general(
            q_dec, st.astype(BF16), NT_DIMS, preferred_element_type=F32)
        st_scr[hh] = st * jnp.exp(cum_last) + lax.dot_general(
            v_h, k_dec, TN_DIMS, preferred_element_type=F32)
        g_pre = proj[:, 3 * d + hh * dk:3 * d + (hh + 1) * dk]
        o_ref[0, :, sl] = _gated_group_norm(o_h, gng, g_pre)

    @pl.when(c == pl.num_programs(1) - 1)
    def _():
        for hh in range(heads):
            s_ref[0, hh] = st_scr[hh].T


def _level_table(rows):
    t = np.arange(rows)[:, None]
    s = np.arange(rows)[None, :]
    x = t ^ s
    lev = np.where(t > s, np.floor(np.log2(np.maximum(x, 1))).astype(np.int32), np.where(t == s, -1, -2))
    return jnp.asarray(lev, dtype=jnp.int32)


def _hgrn_prompt(layer, x, mod, n1g, w_in, lb_raw, gn_g, heads, chunk):
    b, L, d = x.shape
    dk = d // heads
    tri = jnp.asarray(np.tril(np.ones((chunk, chunk), np.float32)), dtype=BF16)
    return pl.pallas_call(
        functools.partial(_hgrn_prompt_kernel, layer, heads),
        grid=(b, L // chunk),
        in_specs=[pl.BlockSpec((1, chunk, d), lambda i, c: (i, c, 0)),
                  pl.BlockSpec((1, 1, mod.shape[2]), lambda i, c: (i, 0, 0)),
                  _full((1, d)), _full(w_in.shape), _full(lb_raw.shape), _full((1, dk)),
                  _full((chunk, chunk)), _full((chunk, chunk))],
        out_specs=[pl.BlockSpec((1, chunk, d), lambda i, c: (i, c, 0)),
                   pl.BlockSpec((1, heads, dk, dk), lambda i, c: (i, 0, 0, 0))],
        out_shape=[jax.ShapeDtypeStruct((b, L, d), F32),
                   jax.ShapeDtypeStruct((b, heads, dk, dk), F32)],
        scratch_shapes=[pltpu.VMEM((heads, dk, dk), F32)],
        compiler_params=_params("arbitrary", "arbitrary"),
        name=f"hgrn_prompt_{layer}",
    )(x, mod, n1g, w_in, lb_raw, gn_g, _level_table(chunk), tri)


def _post_kernel(x_ref, a_ref, mod_ref, n2g_ref, wo_ref, wup_ref, wdn_ref, o_ref):
    d = x_ref.shape[2]
    mod = mod_ref[0]
    g1, sh2, sc2, g2 = (mod[:, j * d:(j + 1) * d] for j in range(2, 6))
    x1 = x_ref[0] + g1 * _dot(a_ref[0].astype(BF16), wo_ref[...])
    h2 = _modulated_norm(x1, n2g_ref[...], sh2, sc2)
    u = jnp.square(jnp.maximum(_dot(h2.astype(BF16), wup_ref[...]), 0.0))
    o_ref[0] = x1 + g2 * _dot(u.astype(BF16), wdn_ref[...])


def _post(x, a, mod, n2g, w_o, w_up, w_dn, tm):
    nb, L, d = x.shape
    r = mod.shape[1]
    return pl.pallas_call(
        _post_kernel,
        grid=(nb, L // tm),
        in_specs=[pl.BlockSpec((1, tm, d), lambda i, j: (i, j, 0)),
                  pl.BlockSpec((1, tm, d), lambda i, j: (i, j, 0)),
                  pl.BlockSpec((1, r, mod.shape[2]), lambda i, j: (i, 0, 0)),
                  _full((1, d)), _full(w_o.shape), _full(w_up.shape), _full(w_dn.shape)],
        out_specs=pl.BlockSpec((1, tm, d), lambda i, j: (i, j, 0)),
        out_shape=jax.ShapeDtypeStruct((nb, L, d), F32),
        compiler_params=_params("arbitrary", "arbitrary"),
        name="post_mlp",
    )(x, a, mod, n2g, w_o, w_up, w_dn)


def _kv_kernel(head_dim, x_ref, mod_ref, g_ref, w_ref, kg_ref, cos_ref, sin_ref, k_ref, v_ref):
    d = x_ref.shape[2]
    mod = mod_ref[0]
    hk = _modulated_norm(x_ref[0], g_ref[...], mod[:, 0:d], mod[:, d:2 * d])
    y = _dot(hk.astype(BF16), w_ref[...])
    n = y.shape[1] // 2
    cos, sin, kg = cos_ref[...], sin_ref[...], kg_ref[...]
    for j in range(n // LANES):
        k_ref[0, :, j * LANES:(j + 1) * LANES] = _head_norm_rope(
            y[:, j * LANES:(j + 1) * LANES], kg, cos, sin, head_dim, 1.0)
    v_ref[0] = y[:, n:]


def _kv(x, mod, g, w_kv, kg, cos, sin, head_dim, tm):
    nb, L, d = x.shape
    r = mod.shape[1]
    n = w_kv.shape[1] // 2
    rope_rows = tm if cos.shape[0] > 1 else 1
    rope_map = (lambda i, j: (j, 0)) if cos.shape[0] > 1 else (lambda i, j: (0, 0))
    return pl.pallas_call(
        functools.partial(_kv_kernel, head_dim),
        grid=(nb, L // tm),
        in_specs=[pl.BlockSpec((1, tm, d), lambda i, j: (i, j, 0)),
                  pl.BlockSpec((1, r, mod.shape[2]), lambda i, j: (i, 0, 0)),
                  _full((1, d)), _full(w_kv.shape), _full((1, LANES)),
                  pl.BlockSpec((rope_rows, LANES), rope_map),
                  pl.BlockSpec((rope_rows, LANES), rope_map)],
        out_specs=[pl.BlockSpec((1, tm, n), lambda i, j: (i, j, 0)),
                   pl.BlockSpec((1, tm, n), lambda i, j: (i, j, 0))],
        out_shape=[jax.ShapeDtypeStruct((nb, L, n), F32), jax.ShapeDtypeStruct((nb, L, n), F32)],
        compiler_params=_params("arbitrary", "arbitrary"),
        name="kv_proj",
    )(x, mod, g, w_kv, kg, cos, sin)


def _q_proj(x, mod, n1g, wq, qg, cos, sin, head_dim):
    d = x.shape[1]
    h = _modulated_norm(x, n1g, mod[:, 0:d], mod[:, d:2 * d])
    y = _dot(h.astype(BF16), wq)
    scale = 1.0 / math.sqrt(head_dim)
    return [_head_norm_rope(y[:, j * LANES:(j + 1) * LANES], qg, cos, sin, head_dim, scale)
            for j in range(y.shape[1] // LANES)]


def _attn_prompt_kernel(head_dim, kv_heads, x_ref, mod_ref, n1g_ref, wq_ref, qg_ref, cos_ref, sin_ref,
                        kp_ref, kc_ref, vp_ref, vc_ref, bias_ref, sink_ref, o_ref):
    w = x_ref.shape[1]
    q_blocks = _q_proj(x_ref[0], mod_ref[0], n1g_ref[...], wq_ref[...], qg_ref[...],
                       cos_ref[...], sin_ref[...], head_dim)
    per_block = LANES // head_dim
    q_heads = [blk[:, i * head_dim:(i + 1) * head_dim] for blk in q_blocks for i in range(per_block)]
    group = len(q_heads) // kv_heads
    kcat = jnp.concatenate([kp_ref[0], kc_ref[0]], axis=0).astype(BF16)
    vcat = jnp.concatenate([vp_ref[0], vc_ref[0]], axis=0).astype(BF16)
    bias = bias_ref[0]
    outs = []
    for kh in range(kv_heads):
        qg = jnp.concatenate(q_heads[kh * group:(kh + 1) * group], axis=0).astype(BF16)
        kk = kcat[:, kh * head_dim:(kh + 1) * head_dim]
        vv = vcat[:, kh * head_dim:(kh + 1) * head_dim]
        s = lax.dot_general(qg, kk, NT_DIMS, preferred_element_type=F32) + bias
        sink = jnp.concatenate([jnp.full((w, 1), sink_ref[kh * group + g], F32) for g in range(group)], axis=0)
        m = jnp.maximum(jnp.max(s, axis=-1, keepdims=True), sink)
        p = jnp.exp(s - m)
        denom = jnp.sum(p, axis=-1, keepdims=True) + jnp.exp(sink - m)
        o = _dot(p.astype(BF16), vv) / denom
        outs.extend(o[g * w:(g + 1) * w] for g in range(group))
    o_ref[0] = jnp.concatenate(outs, axis=-1)


def _window_bias(w, group):
    qi = np.arange(w)[:, None] + w
    kj = np.arange(2 * w)[None, :]
    rel = qi - kj
    valid = (rel >= 0) & (rel < w)
    first = valid & (kj >= w)
    bias = np.where(np.stack([first, valid]), 0.0, -np.inf).astype(np.float32)
    return jnp.asarray(np.tile(bias, (1, group, 1)))


def _attn_prompt(x, mod, n1g, wq, qg, cos, sin, k, v, sinks, head_dim, w):
    b, L, d = x.shape
    nkv = k.shape[2]
    kv_heads = nkv // head_dim
    group = (d // head_dim) // kv_heads
    bias = _window_bias(w, group)
    prev = lambda i, j: (i, jnp.maximum(j - 1, 0), 0)
    cur = lambda i, j: (i, j, 0)
    return pl.pallas_call(
        functools.partial(_attn_prompt_kernel, head_dim, kv_heads),
        grid=(b, L // w),
        in_specs=[pl.BlockSpec((1, w, d), cur),
                  pl.BlockSpec((1, 1, mod.shape[2]), lambda i, j: (i, 0, 0)),
                  _full((1, d)), _full(wq.shape), _full((1, LANES)),
                  pl.BlockSpec((w, LANES), lambda i, j: (j, 0)),
                  pl.BlockSpec((w, LANES), lambda i, j: (j, 0)),
                  pl.BlockSpec((1, w, nkv), prev), pl.BlockSpec((1, w, nkv), cur),
                  pl.BlockSpec((1, w, nkv), prev), pl.BlockSpec((1, w, nkv), cur),
                  pl.BlockSpec((1, group * w, 2 * w), lambda i, j: (jnp.minimum(j, 1), 0, 0)),
                  pl.BlockSpec(memory_space=pltpu.SMEM)],
        out_specs=pl.BlockSpec((1, w, d), cur),
        out_shape=jax.ShapeDtypeStruct((b, L, d), F32),
        compiler_params=_params("arbitrary", "arbitrary"),
        name="attn_prompt",
    )(x, mod, n1g, wq, qg, cos, sin, k, k, v, v, bias, sinks)


def _proj_kernel(x_ref, mod_ref, g_ref, w_ref, o_ref):
    d = x_ref.shape[1]
    mod = mod_ref[...]
    h = _modulated_norm(x_ref[...], g_ref[...], mod[:, 0:d], mod[:, d:2 * d])
    o_ref[...] = _dot(h.astype(BF16), w_ref[...])


def _proj(x, mod, g, w, tn):
    r, d = x.shape
    n = w.shape[1]
    return pl.pallas_call(
        _proj_kernel,
        grid=(n // tn,),
        in_specs=[_full((r, d)), _full(mod.shape), _full((1, d)),
                  pl.BlockSpec((d, tn), lambda j: (0, j))],
        out_specs=pl.BlockSpec((r, tn), lambda j: (0, j)),
        out_shape=jax.ShapeDtypeStruct((r, n), F32),
        compiler_params=_params("arbitrary"),
        name="proj_sample",
    )(x, mod, g, w)


def _q_sample_kernel(head_dim, x_ref, mod_ref, n1g_ref, wq_ref, qg_ref, cos_ref, sin_ref, o_ref):
    blocks = _q_proj(x_ref[...], mod_ref[...], n1g_ref[...], wq_ref[...], qg_ref[...],
                     cos_ref[...], sin_ref[...], head_dim)
    for j, blk in enumerate(blocks):
        o_ref[:, j * LANES:(j + 1) * LANES] = blk


def _q_sample(x, mod, n1g, wq, qg, cos, sin, head_dim):
    r, d = x.shape
    return pl.pallas_call(
        functools.partial(_q_sample_kernel, head_dim),
        grid=(1,),
        in_specs=[_full((r, d)), _full(mod.shape), _full((1, d)), _full(wq.shape),
                  _full((1, LANES)), _full((1, LANES)), _full((1, LANES))],
        out_specs=_full((r, wq.shape[1])),
        out_shape=jax.ShapeDtypeStruct((r, wq.shape[1]), F32),
        compiler_params=_params("arbitrary"),
        name="q_sample",
    )(x, mod, n1g, wq, qg, cos, sin)


def _hgrn_sample_kernel(layer, heads, tb, proj_ref, lbraw_ref, gng_ref, s_ref, o_ref, so_ref,
                        ft_scr, kt_scr, acc_scr):
    i = pl.program_id(0)
    rows = proj_ref.shape[0]
    d = proj_ref.shape[1] // 4
    dk = d // heads

    @pl.when(i == 0)
    def _():
        lb = _lower_bound(lbraw_ref[...], layer)
        fg = lb + (1.0 - lb) * _sigmoid(proj_ref[:, d:2 * d])
        for hh in range(heads):
            f_h = fg[:, hh * dk:(hh + 1) * dk]
            ft_scr[hh] = f_h.T
            kt_scr[hh] = (1.0 - f_h).T
        acc_scr[...] = jnp.zeros_like(acc_scr)

    ones = jnp.ones((rows, dk), BF16)
    lane = lax.broadcasted_iota(jnp.int32, (dk, rows), 1)
    row = lax.broadcasted_iota(jnp.int32, (rows, dk), 0)

    def token(t, carry):
        b = i * tb + t
        pick = lane == b
        for hh in range(heads):
            fm = jnp.where(pick, ft_scr[hh], 0.0)
            hi = fm.astype(BF16)
            lo = (fm - hi.astype(F32)).astype(BF16)
            f_b = _dot(hi, ones) + _dot(lo, ones)
            km = jnp.where(pick, kt_scr[hh], 0.0).astype(BF16)
            v_h = proj_ref[:, 2 * d + hh * dk:2 * d + (hh + 1) * dk].astype(BF16)
            s_new = f_b * s_ref[t, hh] + _dot(km, v_h)
            so_ref[t, hh] = s_new
            qa = proj_ref[:, hh * dk:(hh + 1) * dk]
            q_h = (qa * _sigmoid(qa)).astype(BF16)
            contrib = _dot(q_h, s_new.astype(BF16))
            acc_scr[:, hh * dk:(hh + 1) * dk] += jnp.where(row == b, contrib, 0.0)
        return carry

    lax.fori_loop(0, tb, token, 0)

    @pl.when(i == pl.num_programs(0) - 1)
    def _():
        gng = gng_ref[...]
        for hh in range(heads):
            sl = slice(hh * dk, (hh + 1) * dk)
            o_ref[:, sl] = _gated_group_norm(acc_scr[:, sl], gng,
                                             proj_ref[:, 3 * d + hh * dk:3 * d + (hh + 1) * dk])


def _hgrn_sample(layer, proj, lb_raw, gn_g, state, tb):
    nb, heads, dk, dv = state.shape
    d = proj.shape[1] // 4
    return pl.pallas_call(
        functools.partial(_hgrn_sample_kernel, layer, heads, tb),
        grid=(nb // tb,),
        in_specs=[_full(proj.shape), _full(lb_raw.shape), _full((1, dk)),
                  pl.BlockSpec((tb, heads, dk, dv), lambda i: (i, 0, 0, 0))],
        out_specs=[_full((nb, d)),
                   pl.BlockSpec((tb, heads, dk, dv), lambda i: (i, 0, 0, 0))],
        out_shape=[jax.ShapeDtypeStruct((nb, d), F32),
                   jax.ShapeDtypeStruct(state.shape, F32)],
        scratch_shapes=[pltpu.VMEM((heads, dk, nb), F32), pltpu.VMEM((heads, dk, nb), F32),
                        pltpu.VMEM((nb, d), F32)],
        compiler_params=_params("arbitrary"),
        name=f"hgrn_sample_{layer}",
    )(proj, lb_raw, gn_g, state)


def _cache_roll_kernel(ck_ref, cv_ref, kn_ref, vn_ref, ko_ref, vo_ref):
    w = ck_ref.shape[1]
    ko_ref[:, 0:w - 1, :] = ck_ref[:, 1:w, :]
    ko_ref[:, w - 1:w, :] = kn_ref[...]
    vo_ref[:, 0:w - 1, :] = cv_ref[:, 1:w, :]
    vo_ref[:, w - 1:w, :] = vn_ref[...]


def _cache_roll(ck, cv, kn, vn, tb):
    nb, w, n = ck.shape
    blk = pl.BlockSpec((tb, w, n), lambda i: (i, 0, 0))
    new = pl.BlockSpec((tb, 1, n), lambda i: (i, 0, 0))
    return pl.pallas_call(
        _cache_roll_kernel,
        grid=(nb // tb,),
        in_specs=[blk, blk, new, new],
        out_specs=[blk, blk],
        out_shape=[jax.ShapeDtypeStruct(ck.shape, F32), jax.ShapeDtypeStruct(cv.shape, F32)],
        compiler_params=_params("arbitrary"),
        name="cache_roll",
    )(ck, cv, kn.reshape(nb, 1, n), vn.reshape(nb, 1, n))


def _attn_sample_kernel(tb, q_ref, k_ref, v_ref, sink_ref, o_ref):
    heads, head_dim = q_ref.shape[1], q_ref.shape[2]
    nkv = k_ref.shape[2]
    group = heads // (nkv // head_dim)
    r = lax.broadcasted_iota(jnp.int32, (heads, nkv), 0) // group
    c = lax.broadcasted_iota(jnp.int32, (heads, nkv), 1) // head_dim
    own = r == c
    sink = sink_ref[...]

    def token(t, carry):
        qb = q_ref[t]
        qe = jnp.where(own, jnp.concatenate([qb] * (nkv // head_dim), axis=-1), 0.0).astype(BF16)
        s = lax.dot_general(qe, k_ref[t].astype(BF16), NT_DIMS, preferred_element_type=F32)
        m = jnp.maximum(jnp.max(s, axis=-1, keepdims=True), sink)
        p = jnp.exp(s - m)
        denom = jnp.sum(p, axis=-1, keepdims=True) + jnp.exp(sink - m)
        pv = jnp.where(own, _dot(p.astype(BF16), v_ref[t].astype(BF16)), 0.0)
        o = pv[:, 0:head_dim]
        for j in range(1, nkv // head_dim):
            o = o + pv[:, j * head_dim:(j + 1) * head_dim]
        o_ref[t] = o / denom
        return carry

    lax.fori_loop(0, tb, token, 0)


def _attn_sample(q3, k, v, sinks, tb):
    nb, heads, head_dim = q3.shape
    w, nkv = k.shape[1], k.shape[2]
    return pl.pallas_call(
        functools.partial(_attn_sample_kernel, tb),
        grid=(nb // tb,),
        in_specs=[pl.BlockSpec((tb, heads, head_dim), lambda i: (i, 0, 0)),
                  pl.BlockSpec((tb, w, nkv), lambda i: (i, 0, 0)),
                  pl.BlockSpec((tb, w, nkv), lambda i: (i, 0, 0)),
                  _full((heads, 1))],
        out_specs=pl.BlockSpec((tb, heads, head_dim), lambda i: (i, 0, 0)),
        out_shape=jax.ShapeDtypeStruct(q3.shape, F32),
        compiler_params=_params("arbitrary"),
        name="attn_sample",
    )(q3, k, v, sinks.reshape(heads, 1))


def _rope_tables(pos, head_dim):
    half = head_dim // 2
    inv = ROPE_THETA ** (-jnp.arange(half, dtype=F32) / half)
    ang = pos.astype(F32)[:, None] * inv[None, :]
    cos, sin = jnp.cos(ang), jnp.sin(ang)
    reps = LANES // head_dim
    return (jnp.tile(jnp.concatenate([cos, cos], axis=1), (1, reps)),
            jnp.tile(jnp.concatenate([-sin, sin], axis=1), (1, reps)))


def kernel(x_prompt, x_sample, c_prompt, c_sample, state_hgrn, cache_k, cache_v, w_ada, b_ada, norm1_g, norm2_g, hg_w_in, hg_w_out, hg_lower_bounds, hg_gn_g, kv_w_ada, kv_b_ada, kv_norm_g, w_kv, k_norm_g, w_q, q_norm_g, sinks, w_o, w_up, w_down):
    bp, Lp, d = x_prompt.shape
    bs = x_sample.shape[0]
    depth = w_ada.shape[0]
    n_a = hg_w_in.shape[0]
    heads = state_hgrn.shape[2]
    window, kv_heads, head_dim = cache_k.shape[1], cache_k.shape[2], cache_k.shape[3]
    nkv = kv_heads * head_dim
    reps = LANES // head_dim
    bf = lambda t: t.astype(BF16)

    pad = (-(bs + bp)) % 16
    c_all = jnp.concatenate([c_sample, c_prompt, jnp.zeros((pad, d), F32)], axis=0)
    mods = _ada(c_all, w_ada, b_ada, 1536)
    kv_mods = _ada(c_all, kv_w_ada[None], kv_b_ada[None], 1024)[0]
    mod_p = lambda m: m[bs:bs + bp][:, None, :]
    mod_s = lambda m: m[0:bs]

    cos_p, sin_p = _rope_tables(jnp.arange(Lp), head_dim)
    cos_s, sin_s = _rope_tables(jnp.full((1,), PAST_LEN), head_dim)
    kg = jnp.tile(k_norm_g, reps)[None]
    w_kv_b = bf(w_kv)

    xp = x_prompt
    xs = x_sample.reshape(bs, d)
    hg_p, hg_s = [], []
    k_p = v_p = k_s = v_s = None
    for l in range(depth):
        n1g, n2g = norm1_g[l][None], norm2_g[l][None]
        w_up_b, w_dn_b = bf(w_up[l]), bf(w_down[l])
        if l == n_a:
            k_p, v_p = _kv(xp, mod_p(kv_mods), kv_norm_g[None], w_kv_b, kg, cos_p, sin_p, head_dim, 512)
            k_n, v_n = _kv(xs[None], mod_s(kv_mods)[None], kv_norm_g[None], w_kv_b, kg, cos_s, sin_s,
                           head_dim, bs)
            k_s, v_s = _cache_roll(cache_k.reshape(bs, window, nkv), cache_v.reshape(bs, window, nkv),
                                   k_n[0], v_n[0], 8)
        if l < n_a:
            w_in_b, w_mix_b = bf(hg_w_in[l]), bf(hg_w_out[l])
            gng = hg_gn_g[l][None]
            a_p, s_p = _hgrn_prompt(l, xp, mod_p(mods[l]), n1g, w_in_b, hg_lower_bounds, gng, heads, 128)
            hg_p.append(s_p)
            proj_s = _proj(xs, mod_s(mods[l]), n1g, w_in_b, 1024)
            a_s, s_s = _hgrn_sample(l, proj_s, hg_lower_bounds, gng, state_hgrn[l], 8)
            hg_s.append(s_s)
        else:
            j = l - n_a
            w_q_b, w_mix_b = bf(w_q[j]), bf(w_o[j])
            qg = jnp.tile(q_norm_g[j], reps)[None]
            a_p = _attn_prompt(xp, mod_p(mods[l]), n1g, w_q_b, qg, cos_p, sin_p, k_p, v_p, sinks[j],
                               head_dim, window)
            q_s = _q_sample(xs, mod_s(mods[l]), n1g, w_q_b, qg, cos_s, sin_s, head_dim)
            a_s = _attn_sample(q_s.reshape(bs, d // head_dim, head_dim), k_s, v_s, sinks[j], 8)
            a_s = a_s.reshape(bs, d)
        xp = _post(xp, a_p, mod_p(mods[l]), n2g, w_mix_b, w_up_b, w_dn_b, 256)
        xs = _post(xs[None], a_s[None], mod_s(mods[l])[None], n2g, w_mix_b, w_up_b, w_dn_b, bs)[0]

    shape4 = lambda t: t.reshape(t.shape[0], window, kv_heads, head_dim)
    return (xp, xs.reshape(bs, 1, d), jnp.stack(hg_p), shape4(k_p[:, Lp - window:]),
            shape4(v_p[:, Lp - window:]), jnp.stack(hg_s), shape4(k_s), shape4(v_s))
```

```python
import functools
import math

import numpy as np
import jax
import jax.numpy as jnp
from jax import lax
from jax.experimental import pallas as pl
from jax.experimental.pallas import tpu as pltpu

F32 = jnp.float32
BF16 = jnp.bfloat16

PAST_LEN = 8192
ROPE_THETA = 10000.0
EPS = 1e-6
LOG2E = 1.4426950408889634
LANES = 128
VMEM_LIMIT = 56 * 1024 * 1024

NT_DIMS = (((1,), (1,)), ((), ()))
TN_DIMS = (((0,), (0,)), ((), ()))


def _dot(a, b):
    return jnp.dot(a, b, preferred_element_type=F32)


def _dot_nt(a, b):
    return lax.dot_general(a, b, NT_DIMS, preferred_element_type=F32)


def _sigmoid(x):
    return 1.0 / (1.0 + jnp.exp(-x))


def _rms(x, g):
    ms = jnp.mean(x * x, axis=-1, keepdims=True)
    return x * lax.rsqrt(ms + EPS) * g


def _modulated_norm(x, gain, shift, scale):
    return _rms(x, gain) * (1.0 + scale) + shift


def _params(*sem):
    return pltpu.CompilerParams(dimension_semantics=sem, vmem_limit_bytes=VMEM_LIMIT)


def _full(shape):
    n = len(shape)
    return pl.BlockSpec(shape, lambda *_: (0,) * n)


def _ada_kernel(c_ref, w_ref, b_ref, o_ref):
    c = c_ref[...]
    a = (c * _sigmoid(c)).astype(BF16)
    o_ref[0] = _dot(a, w_ref[0].astype(BF16)) + b_ref[0]


def _ada(c_all, w, b, tn):
    nl, d, n = w.shape
    r = c_all.shape[0]
    return pl.pallas_call(
        _ada_kernel,
        grid=(nl, n // tn),
        in_specs=[pl.BlockSpec((r, d), lambda l, j: (0, 0)),
                  pl.BlockSpec((1, d, tn), lambda l, j: (l, 0, j)),
                  pl.BlockSpec((1, 1, tn), lambda l, j: (l, 0, j))],
        out_specs=pl.BlockSpec((1, r, tn), lambda l, j: (l, 0, j)),
        out_shape=jax.ShapeDtypeStruct((nl, r, n), F32),
        compiler_params=_params("arbitrary", "arbitrary"),
        name="ada",
    )(c_all, w, b.reshape(nl, 1, n))


def _group_mean_matrix(group):
    r = lax.broadcasted_iota(jnp.int32, (LANES, LANES), 0) // group
    c = lax.broadcasted_iota(jnp.int32, (LANES, LANES), 1) // group
    return jnp.where(r == c, 1.0 / group, 0.0).astype(BF16)


def _head_norm_rope(y, gain, cos, sin, head_dim, out_scale):
    ms = _dot((y * y).astype(BF16), _group_mean_matrix(head_dim))
    yn = y * lax.rsqrt(ms + EPS) * gain
    half = head_dim // 2
    lane = lax.broadcasted_iota(jnp.int32, yn.shape, 1)
    first = (lane % head_dim) < half
    rot = jnp.where(first, pltpu.roll(yn, LANES - half, 1), pltpu.roll(yn, half, 1))
    out = yn * cos + rot * sin
    if out_scale != 1.0:
        out = out * out_scale
    return out


def _lower_bound(raw, layer):
    m = jnp.max(raw, axis=0, keepdims=True)
    e = jnp.exp(raw - m)
    sm = e / jnp.sum(e, axis=0, keepdims=True)
    acc = sm[0:1]
    for j in range(1, layer + 1):
        acc = acc + sm[j:j + 1]
    return acc - sm[0:1]


def _hgrn_gates(proj, lb, d):
    qa = proj[:, 0:d]
    q = qa * _sigmoid(qa)
    fg = lb + (1.0 - lb) * _sigmoid(proj[:, d:2 * d])
    return q, fg


def _gated_group_norm(o, gain, gate_pre):
    ms = jnp.mean(o * o, axis=-1, keepdims=True)
    return o * lax.rsqrt(ms + EPS) * gain * (gate_pre * _sigmoid(gate_pre))


def _hgrn_prompt_kernel(layer, heads, x_ref, mod_ref, n1g_ref, win_ref, lbraw_ref, gng_ref,
                        lev_ref, sums_ref, o_ref, s_ref, st_scr, d_scr):
    c = pl.program_id(1)
    rows, d = x_ref.shape[1], x_ref.shape[2]
    dk = d // heads
    n_levels = int(math.log2(rows))

    @pl.when(c == 0)
    def _():
        st_scr[...] = jnp.zeros_like(st_scr)

    mod = mod_ref[0]
    h = _modulated_norm(x_ref[0], n1g_ref[...], mod[:, 0:d], mod[:, d:2 * d])
    proj = _dot(h.astype(BF16), win_ref[...])
    lb = _lower_bound(lbraw_ref[...], layer)
    q, fg = _hgrn_gates(proj, lb, d)
    k = 1.0 - fg
    lf2 = jnp.log(fg) * LOG2E
    hi = lf2.astype(BF16)
    lo = (lf2 - hi.astype(F32)).astype(BF16)
    d_scr[...] = _dot(sums_ref[...], jnp.concatenate([hi, lo], axis=0))
    lev = lev_ref[...]
    row_id = lax.broadcasted_iota(jnp.int32, (rows, dk), 0)
    upper = [((row_id >> p) & 1) == 1 for p in range(n_levels)]
    gng = gng_ref[...]

    for hh in range(heads):
        sl = slice(hh * dk, (hh + 1) * dk)
        q_h, k_h = q[:, sl], k[:, sl]
        v_f = proj[:, 2 * d + hh * dk:2 * d + (hh + 1) * dk]
        v_h = v_f.astype(BF16)
        att = jnp.zeros((rows, rows), F32)
        for p in range(n_levels):
            e = jnp.exp2(d_scr[p * rows:(p + 1) * rows, sl])
            xs = (jnp.where(upper[p], q_h, k_h) * e).astype(BF16)
            att = jnp.where(lev == p, _dot_nt(xs, xs), att)
        cum = d_scr[n_levels * rows:(n_levels + 1) * rows, sl]
        cum_last = cum[rows - 1:rows, :]
        q_dec = (q_h * jnp.exp2(cum)).astype(BF16)
        k_dec = (k_h * jnp.exp2(cum_last - cum)).astype(BF16)
        st = st_scr[hh]
        o_h = (_dot(att.astype(BF16), v_h) + _dot_nt(q_dec, st.astype(BF16))
               + jnp.sum(q_h * k_h, axis=-1, keepdims=True) * v_f)
        st_scr[hh] = st * jnp.exp2(cum_last) + lax.dot_general(
            v_h, k_dec, TN_DIMS, preferred_element_type=F32)
        g_pre = proj[:, 3 * d + hh * dk:3 * d + (hh + 1) * dk]
        o_ref[0, :, sl] = _gated_group_norm(o_h, gng, g_pre)

    @pl.when(c == pl.num_programs(1) - 1)
    def _():
        for hh in range(heads):
            s_ref[0, hh] = st_scr[hh].T


def _level_table(rows):
    t = np.arange(rows)[:, None]
    s = np.arange(rows)[None, :]
    x = t ^ s
    lev = np.where(t > s, np.floor(np.log2(np.maximum(x, 1))).astype(np.int32), np.where(t == s, -1, -2))
    return jnp.asarray(lev, dtype=jnp.int32)


def _sum_table(rows):
    t = np.arange(rows)[:, None]
    j = np.arange(rows)[None, :]
    blocks = []
    for p in range(int(math.log2(rows))):
        m = 1 << p
        ref = (t // (2 * m)) * (2 * m) + m - 1
        up = ((t >> p) & 1) == 1
        blocks.append(np.where(up, (j > ref) & (j <= t), (j > t) & (j <= ref)))
    blocks.append(j <= t)
    table = np.concatenate(blocks, axis=0).astype(np.float32)
    return jnp.asarray(np.concatenate([table, table], axis=1), dtype=BF16)


def _hgrn_prompt(layer, x, mod, n1g, w_in, lb_raw, gn_g, heads, chunk):
    b, L, d = x.shape
    dk = d // heads
    sums = _sum_table(chunk)
    return pl.pallas_call(
        functools.partial(_hgrn_prompt_kernel, layer, heads),
        grid=(b, L // chunk),
        in_specs=[pl.BlockSpec((1, chunk, d), lambda i, c: (i, c, 0)),
                  pl.BlockSpec((1, 1, mod.shape[2]), lambda i, c: (i, 0, 0)),
                  _full((1, d)), _full(w_in.shape), _full(lb_raw.shape), _full((1, dk)),
                  _full((chunk, chunk)), _full(sums.shape)],
        out_specs=[pl.BlockSpec((1, chunk, d), lambda i, c: (i, c, 0)),
                   pl.BlockSpec((1, heads, dk, dk), lambda i, c: (i, 0, 0, 0))],
        out_shape=[jax.ShapeDtypeStruct((b, L, d), F32),
                   jax.ShapeDtypeStruct((b, heads, dk, dk), F32)],
        scratch_shapes=[pltpu.VMEM((heads, dk, dk), F32), pltpu.VMEM((sums.shape[0], d), F32)],
        compiler_params=_params("arbitrary", "arbitrary"),
        name=f"hgrn_prompt_{layer}",
    )(x, mod, n1g, w_in, lb_raw, gn_g, _level_table(chunk), sums)


def _post_kernel(x_ref, a_ref, mod_ref, n2g_ref, wo_ref, wup_ref, wdn_ref, o_ref):
    d = x_ref.shape[2]
    mod = mod_ref[0]
    g1, sh2, sc2, g2 = (mod[:, j * d:(j + 1) * d] for j in range(2, 6))
    x1 = x_ref[0] + g1 * _dot(a_ref[0].astype(BF16), wo_ref[...])
    h2 = _modulated_norm(x1, n2g_ref[...], sh2, sc2)
    u = jnp.square(jnp.maximum(_dot(h2.astype(BF16), wup_ref[...]), 0.0))
    o_ref[0] = x1 + g2 * _dot(u.astype(BF16), wdn_ref[...])


def _post(x, a, mod, n2g, w_o, w_up, w_dn, tm):
    nb, L, d = x.shape
    r = mod.shape[1]
    return pl.pallas_call(
        _post_kernel,
        grid=(nb, L // tm),
        in_specs=[pl.BlockSpec((1, tm, d), lambda i, j: (i, j, 0)),
                  pl.BlockSpec((1, tm, d), lambda i, j: (i, j, 0)),
                  pl.BlockSpec((1, r, mod.shape[2]), lambda i, j: (i, 0, 0)),
                  _full((1, d)), _full(w_o.shape), _full(w_up.shape), _full(w_dn.shape)],
        out_specs=pl.BlockSpec((1, tm, d), lambda i, j: (i, j, 0)),
        out_shape=jax.ShapeDtypeStruct((nb, L, d), F32),
        compiler_params=_params("arbitrary", "arbitrary"),
        name="post_mlp",
    )(x, a, mod, n2g, w_o, w_up, w_dn)


def _kv_kernel(head_dim, window, x_ref, mod_ref, g_ref, w_ref, kg_ref, cos_ref, sin_ref,
               k_ref, v_ref, kx_ref, vx_ref):
    tm, d = x_ref.shape[1], x_ref.shape[2]
    mod = mod_ref[0]
    hk = _modulated_norm(x_ref[0], g_ref[...], mod[:, 0:d], mod[:, d:2 * d])
    y = _dot(hk.astype(BF16), w_ref[...])
    n = y.shape[1] // 2
    cos, sin, kg = cos_ref[...], sin_ref[...], kg_ref[...]
    low = lax.broadcasted_iota(jnp.int32, (tm, LANES), 1) < head_dim
    k_blocks, v_blocks = [], []
    for j in range(n // LANES):
        kb = _head_norm_rope(y[:, j * LANES:(j + 1) * LANES], kg, cos, sin, head_dim, 1.0)
        vb = y[:, n + j * LANES:n + (j + 1) * LANES]
        k_blocks.append(kb)
        v_blocks.append(vb)
        kx_ref[0, :, j * LANES:(j + 1) * LANES] = kb.astype(BF16)
        kx_ref[0, :, n + j * LANES:n + (j + 1) * LANES] = pltpu.roll(kb, head_dim, 1).astype(BF16)
        vs = pltpu.roll(vb, head_dim, 1)
        variants = (jnp.where(low, vb, 0.0), jnp.where(low, 0.0, vs),
                    jnp.where(low, vs, 0.0), jnp.where(low, 0.0, vb))
        for i, var in enumerate(variants):
            off = (4 * j + i) * LANES
            vx_ref[0, off:off + LANES, :] = var.T.astype(BF16)

    @pl.when(pl.program_id(1) == pl.num_programs(1) - 1)
    def _():
        for j in range(n // LANES):
            k_ref[0, :, j * LANES:(j + 1) * LANES] = k_blocks[j][tm - window:, :]
            v_ref[0, :, j * LANES:(j + 1) * LANES] = v_blocks[j][tm - window:, :]


def _kv(x, mod, g, w_kv, kg, cos, sin, head_dim, window, tm):
    nb, L, d = x.shape
    r = mod.shape[1]
    n = w_kv.shape[1] // 2
    rope_rows = tm if cos.shape[0] > 1 else 1
    rope_map = (lambda i, j: (j, 0)) if cos.shape[0] > 1 else (lambda i, j: (0, 0))
    last = lambda i, j: (i, 0, 0)
    tile = lambda i, j: (i, j, 0)
    return pl.pallas_call(
        functools.partial(_kv_kernel, head_dim, window),
        grid=(nb, L // tm),
        in_specs=[pl.BlockSpec((1, tm, d), tile),
                  pl.BlockSpec((1, r, mod.shape[2]), lambda i, j: (i, 0, 0)),
                  _full((1, d)), _full(w_kv.shape), _full((1, LANES)),
                  pl.BlockSpec((rope_rows, LANES), rope_map),
                  pl.BlockSpec((rope_rows, LANES), rope_map)],
        out_specs=[pl.BlockSpec((1, window, n), last), pl.BlockSpec((1, window, n), last),
                   pl.BlockSpec((1, tm, 2 * n), tile), pl.BlockSpec((1, 4 * n, tm), lambda i, j: (i, 0, j))],
        out_shape=[jax.ShapeDtypeStruct((nb, window, n), F32), jax.ShapeDtypeStruct((nb, window, n), F32),
                   jax.ShapeDtypeStruct((nb, L, 2 * n), BF16), jax.ShapeDtypeStruct((nb, 4 * n, L), BF16)],
        compiler_params=_params("arbitrary", "arbitrary"),
        name="kv_proj",
    )(x, mod, g, w_kv, kg, cos, sin)


def _q_proj(x, mod, n1g, wq, qg, cos, sin, head_dim):
    d = x.shape[1]
    h = _modulated_norm(x, n1g, mod[:, 0:d], mod[:, d:2 * d])
    y = _dot(h.astype(BF16), wq)
    scale = 1.0 / math.sqrt(head_dim)
    return [_head_norm_rope(y[:, j * LANES:(j + 1) * LANES], qg, cos, sin, head_dim, scale)
            for j in range(y.shape[1] // LANES)]


def _attn_prompt_kernel(head_dim, kv_heads, x_ref, mod_ref, n1g_ref, wq_ref, qg_ref, cos_ref, sin_ref,
                        kp_ref, kc_ref, vp_ref, vc_ref, sink_ref, o_ref):
    w = x_ref.shape[1]
    nkv = kv_heads * head_dim
    q_blocks = _q_proj(x_ref[0], mod_ref[0], n1g_ref[...], wq_ref[...], qg_ref[...],
                       cos_ref[...], sin_ref[...], head_dim)
    group = (2 * len(q_blocks)) // kv_heads
    key = lax.broadcasted_iota(jnp.int32, (w, 2 * w), 0)
    qry = lax.broadcasted_iota(jnp.int32, (w, 2 * w), 1)
    own = key <= (qry & (w - 1))
    first = lax.broadcasted_iota(jnp.int32, (1, 2 * w), 1) < w
    low = lax.broadcasted_iota(jnp.int32, (w, LANES), 1) < head_dim
    no_prev = jnp.where(pl.program_id(1) == 0, -jnp.inf, 0.0)
    for kh in range(kv_heads):
        out_t = None
        for half in range(2):
            ha, hb = kh * group + half, kh * group + 2 + half
            mask = (lambda t: jnp.where(low, t, 0.0)) if half == 0 else (lambda t: jnp.where(low, 0.0, t))
            qcat = jnp.concatenate([mask(q_blocks[ha // 2]), mask(q_blocks[hb // 2])], axis=0).astype(BF16)
            koff = (kh // 2) * LANES + (0 if kh % 2 == half else nkv)
            voff = (2 * kh + half) * LANES
            kcat = jnp.concatenate([kp_ref[0, :, koff:koff + LANES], kc_ref[0, :, koff:koff + LANES]], axis=0)
            v_t = jnp.concatenate([vc_ref[0, voff:voff + LANES, :], vp_ref[0, voff:voff + LANES, :]], axis=1)
            s_t = _dot_nt(kcat, qcat)
            sc = jnp.where(own, s_t[w:], s_t[:w] + no_prev)
            sink = jnp.where(first, sink_ref[ha], sink_ref[hb])
            m = jnp.maximum(jnp.max(sc, axis=0, keepdims=True), sink)
            p = jnp.exp(sc - m)
            pn = p * (1.0 / (jnp.sum(p, axis=0, keepdims=True) + jnp.exp(sink - m)))
            p_t = jnp.concatenate([jnp.where(own, pn, 0.0), jnp.where(own, 0.0, pn)], axis=0).astype(BF16)
            part = _dot(v_t, p_t)
            out_t = part if out_t is None else out_t + part
        for i in range(2):
            o_ref[0, :, (2 * kh + i) * LANES:(2 * kh + i + 1) * LANES] = out_t[:, i * w:(i + 1) * w].T


def _attn_prompt(x, mod, n1g, wq, qg, cos, sin, kx, vx, sinks, head_dim, kv_heads, w):
    b, L, d = x.shape
    prev = lambda i, j: (i, jnp.maximum(j - 1, 0), 0)
    cur = lambda i, j: (i, j, 0)
    return pl.pallas_call(
        functools.partial(_attn_prompt_kernel, head_dim, kv_heads),
        grid=(b, L // w),
        in_specs=[pl.BlockSpec((1, w, d), cur),
                  pl.BlockSpec((1, 1, mod.shape[2]), lambda i, j: (i, 0, 0)),
                  _full((1, d)), _full(wq.shape), _full((1, LANES)),
                  pl.BlockSpec((w, LANES), lambda i, j: (j, 0)),
                  pl.BlockSpec((w, LANES), lambda i, j: (j, 0)),
                  pl.BlockSpec((1, w, kx.shape[2]), prev), pl.BlockSpec((1, w, kx.shape[2]), cur),
                  pl.BlockSpec((1, vx.shape[1], w), lambda i, j: (i, 0, jnp.maximum(j - 1, 0))),
                  pl.BlockSpec((1, vx.shape[1], w), lambda i, j: (i, 0, j)),
                  pl.BlockSpec(memory_space=pltpu.SMEM)],
        out_specs=pl.BlockSpec((1, w, d), cur),
        out_shape=jax.ShapeDtypeStruct((b, L, d), F32),
        compiler_params=_params("arbitrary", "arbitrary"),
        name="attn_prompt",
    )(x, mod, n1g, wq, qg, cos, sin, kx, kx, vx, vx, sinks)


def _proj_kernel(x_ref, mod_ref, g_ref, w_ref, o_ref):
    d = x_ref.shape[1]
    mod = mod_ref[...]
    h = _modulated_norm(x_ref[...], g_ref[...], mod[:, 0:d], mod[:, d:2 * d])
    o_ref[...] = _dot(h.astype(BF16), w_ref[...])


def _proj(x, mod, g, w, tn):
    r, d = x.shape
    n = w.shape[1]
    return pl.pallas_call(
        _proj_kernel,
        grid=(n // tn,),
        in_specs=[_full((r, d)), _full(mod.shape), _full((1, d)),
                  pl.BlockSpec((d, tn), lambda j: (0, j))],
        out_specs=pl.BlockSpec((r, tn), lambda j: (0, j)),
        out_shape=jax.ShapeDtypeStruct((r, n), F32),
        compiler_params=_params("arbitrary"),
        name="proj_sample",
    )(x, mod, g, w)


def _q_sample_kernel(head_dim, x_ref, mod_ref, n1g_ref, wq_ref, qg_ref, cos_ref, sin_ref, o_ref):
    blocks = _q_proj(x_ref[...], mod_ref[...], n1g_ref[...], wq_ref[...], qg_ref[...],
                     cos_ref[...], sin_ref[...], head_dim)
    for j, blk in enumerate(blocks):
        o_ref[:, j * LANES:(j + 1) * LANES] = blk


def _q_sample(x, mod, n1g, wq, qg, cos, sin, head_dim):
    r, d = x.shape
    return pl.pallas_call(
        functools.partial(_q_sample_kernel, head_dim),
        grid=(1,),
        in_specs=[_full((r, d)), _full(mod.shape), _full((1, d)), _full(wq.shape),
                  _full((1, LANES)), _full((1, LANES)), _full((1, LANES))],
        out_specs=_full((r, wq.shape[1])),
        out_shape=jax.ShapeDtypeStruct((r, wq.shape[1]), F32),
        compiler_params=_params("arbitrary"),
        name="q_sample",
    )(x, mod, n1g, wq, qg, cos, sin)


def _hgrn_sample_kernel(layer, heads, tb, aliased, proj_ref, lbraw_ref, gng_ref, s_ref, *rest):
    o_ref, so_ref, stack_scr, acc_scr = rest[1:] if aliased else rest
    i = pl.program_id(0)
    rows = proj_ref.shape[0]
    d = proj_ref.shape[1] // 4
    dk = d // heads

    @pl.when(i == 0)
    def _():
        lb = _lower_bound(lbraw_ref[...], layer)
        q, fg = _hgrn_gates(proj_ref[:, 0:2 * d], lb, d)
        for hh in range(heads):
            sl = slice(hh * dk, (hh + 1) * dk)
            ft = fg[:, sl].T
            hi = ft.astype(BF16)
            stack_scr[hh, 0:dk] = hi
            stack_scr[hh, dk:2 * dk] = (ft - hi.astype(F32)).astype(BF16)
            stack_scr[hh, 2 * dk:3 * dk] = (1.0 - ft).astype(BF16)
            stack_scr[hh, 3 * dk:4 * dk] = q[:, sl].T.astype(BF16)

    token_row = lax.broadcasted_iota(jnp.int32, (rows, dk), 0)
    sub = lax.broadcasted_iota(jnp.int32, (tb, dk), 0)
    base = pl.multiple_of(i * tb, tb)
    v_rows = proj_ref[pl.ds(base, tb), 2 * d:3 * d]
    o_rows = [jnp.zeros((tb, dk), F32) for _ in range(heads)]
    for t in range(tb):
        onehot = jnp.where(token_row == base + t, 1.0, 0.0).astype(BF16)
        for hh in range(heads):
            bc = _dot(stack_scr[hh], onehot)
            f_b = bc[0:dk] + bc[dk:2 * dk]
            k_b = bc[2 * dk:3 * dk]
            q_b = bc[3 * dk:4 * dk]
            s_new = f_b * s_ref[0, t, hh] + k_b * v_rows[t:t + 1, hh * dk:(hh + 1) * dk]
            so_ref[0, t, hh] = s_new
            o_rows[hh] = jnp.where(sub == t, jnp.sum(q_b * s_new, axis=0, keepdims=True), o_rows[hh])
    for hh in range(heads):
        acc_scr[pl.ds(base, tb), hh * dk:(hh + 1) * dk] = o_rows[hh]

    @pl.when(i == pl.num_programs(0) - 1)
    def _():
        gng = gng_ref[...]
        for hh in range(heads):
            sl = slice(hh * dk, (hh + 1) * dk)
            o_ref[:, sl] = _gated_group_norm(acc_scr[:, sl], gng,
                                             proj_ref[:, 3 * d + hh * dk:3 * d + (hh + 1) * dk])


def _hgrn_sample(layer, proj, lb_raw, gn_g, state, new_state, tb):
    n_layers, nb, heads, dk, dv = state.shape
    d = proj.shape[1] // 4
    aliased = new_state is not None
    blk = pl.BlockSpec((1, tb, heads, dk, dv), lambda i: (layer, i, 0, 0, 0))
    in_specs = [_full(proj.shape), _full(lb_raw.shape), _full((1, dk)), blk]
    args = [proj, lb_raw, gn_g, state]
    if aliased:
        in_specs.append(pl.BlockSpec(memory_space=pl.ANY))
        args.append(new_state)
    return pl.pallas_call(
        functools.partial(_hgrn_sample_kernel, layer, heads, tb, aliased),
        grid=(nb // tb,),
        in_specs=in_specs,
        out_specs=[_full((nb, d)), blk],
        out_shape=[jax.ShapeDtypeStruct((nb, d), F32), jax.ShapeDtypeStruct(state.shape, F32)],
        scratch_shapes=[pltpu.VMEM((heads, 4 * dk, nb), BF16), pltpu.VMEM((nb, d), F32)],
        input_output_aliases={4: 1} if aliased else {},
        compiler_params=_params("arbitrary"),
        name=f"hgrn_sample_{layer}",
    )(*args)


def _cache_roll_kernel(ck_ref, cv_ref, kn_ref, vn_ref, ko_ref, vo_ref):
    w = ck_ref.shape[1]
    ko_ref[:, 0:w - 1, :] = ck_ref[:, 1:w, :]
    ko_ref[:, w - 1:w, :] = kn_ref[...]
    vo_ref[:, 0:w - 1, :] = cv_ref[:, 1:w, :]
    vo_ref[:, w - 1:w, :] = vn_ref[...]


def _cache_roll(ck, cv, kn, vn, tb):
    nb, w, n = ck.shape
    blk = pl.BlockSpec((tb, w, n), lambda i: (i, 0, 0))
    new = pl.BlockSpec((tb, 1, n), lambda i: (i, 0, 0))
    return pl.pallas_call(
        _cache_roll_kernel,
        grid=(nb // tb,),
        in_specs=[blk, blk, new, new],
        out_specs=[blk, blk],
        out_shape=[jax.ShapeDtypeStruct(ck.shape, F32), jax.ShapeDtypeStruct(cv.shape, F32)],
        compiler_params=_params("arbitrary"),
        name="cache_roll",
    )(ck, cv, kn.reshape(nb, 1, n), vn.reshape(nb, 1, n))


def _attn_sample_kernel(tb, q_ref, k_ref, v_ref, sink_ref, o_ref):
    heads, head_dim = q_ref.shape[1], q_ref.shape[2]
    nkv = k_ref.shape[2]
    group = heads // (nkv // head_dim)
    r = lax.broadcasted_iota(jnp.int32, (heads, nkv), 0) // group
    c = lax.broadcasted_iota(jnp.int32, (heads, nkv), 1) // head_dim
    own = r == c
    sink = sink_ref[...]

    def token(t, carry):
        qb = q_ref[t]
        qe = jnp.where(own, jnp.concatenate([qb] * (nkv // head_dim), axis=-1), 0.0).astype(BF16)
        s = _dot_nt(qe, k_ref[t].astype(BF16))
        m = jnp.maximum(jnp.max(s, axis=-1, keepdims=True), sink)
        p = jnp.exp(s - m)
        denom = jnp.sum(p, axis=-1, keepdims=True) + jnp.exp(sink - m)
        pv = jnp.where(own, _dot(p.astype(BF16), v_ref[t].astype(BF16)), 0.0)
        o = pv[:, 0:head_dim]
        for j in range(1, nkv // head_dim):
            o = o + pv[:, j * head_dim:(j + 1) * head_dim]
        o_ref[t] = o / denom
        return carry

    lax.fori_loop(0, tb, token, 0)


def _attn_sample(q3, k, v, sinks, tb):
    nb, heads, head_dim = q3.shape
    w, nkv = k.shape[1], k.shape[2]
    return pl.pallas_call(
        functools.partial(_attn_sample_kernel, tb),
        grid=(nb // tb,),
        in_specs=[pl.BlockSpec((tb, heads, head_dim), lambda i: (i, 0, 0)),
                  pl.BlockSpec((tb, w, nkv), lambda i: (i, 0, 0)),
                  pl.BlockSpec((tb, w, nkv), lambda i: (i, 0, 0)),
                  _full((heads, 1))],
        out_specs=pl.BlockSpec((tb, heads, head_dim), lambda i: (i, 0, 0)),
        out_shape=jax.ShapeDtypeStruct(q3.shape, F32),
        compiler_params=_params("arbitrary"),
        name="attn_sample",
    )(q3, k, v, sinks.reshape(heads, 1))


def _rope_tables(pos, head_dim):
    half = head_dim // 2
    inv = ROPE_THETA ** (-jnp.arange(half, dtype=F32) / half)
    ang = pos.astype(F32)[:, None] * inv[None, :]
    cos, sin = jnp.cos(ang), jnp.sin(ang)
    reps = LANES // head_dim
    return (jnp.tile(jnp.concatenate([cos, cos], axis=1), (1, reps)),
            jnp.tile(jnp.concatenate([-sin, sin], axis=1), (1, reps)))


def kernel(x_prompt, x_sample, c_prompt, c_sample, state_hgrn, cache_k, cache_v, w_ada, b_ada, norm1_g, norm2_g, hg_w_in, hg_w_out, hg_lower_bounds, hg_gn_g, kv_w_ada, kv_b_ada, kv_norm_g, w_kv, k_norm_g, w_q, q_norm_g, sinks, w_o, w_up, w_down):
    bp, Lp, d = x_prompt.shape
    bs = x_sample.shape[0]
    depth = w_ada.shape[0]
    n_a = hg_w_in.shape[0]
    heads = state_hgrn.shape[2]
    window, kv_heads, head_dim = cache_k.shape[1], cache_k.shape[2], cache_k.shape[3]
    assert LANES == 2 * head_dim and kv_heads % 2 == 0 and Lp % window == 0
    assert window & (window - 1) == 0 and (d // head_dim) // kv_heads == 4
    nkv = kv_heads * head_dim
    reps = LANES // head_dim
    bf = lambda t: t.astype(BF16)

    pad = (-(bs + bp)) % 16
    c_all = jnp.concatenate([c_sample, c_prompt, jnp.zeros((pad, d), F32)], axis=0)
    mods = _ada(c_all, w_ada, b_ada, 1536)
    kv_mods = _ada(c_all, kv_w_ada[None], kv_b_ada[None], 1024)[0]
    mod_p = lambda m: m[bs:bs + bp][:, None, :]
    mod_s = lambda m: m[0:bs]

    cos_p, sin_p = _rope_tables(jnp.arange(Lp), head_dim)
    cos_s, sin_s = _rope_tables(jnp.full((1,), PAST_LEN), head_dim)
    kg = jnp.tile(k_norm_g, reps)[None]
    w_kv_b = bf(w_kv)

    xp = x_prompt
    xs = x_sample.reshape(bs, d)
    hg_p, hg_s = [], None
    k_p = v_p = kx_p = vx_p = k_s = v_s = None
    for l in range(depth):
        n1g, n2g = norm1_g[l][None], norm2_g[l][None]
        w_up_b, w_dn_b = bf(w_up[l]), bf(w_down[l])
        if l == n_a:
            k_p, v_p, kx_p, vx_p = _kv(xp, mod_p(kv_mods), kv_norm_g[None], w_kv_b, kg, cos_p, sin_p,
                                       head_dim, window, 512)
            k_n, v_n, _, _ = _kv(xs[None], mod_s(kv_mods)[None], kv_norm_g[None], w_kv_b, kg, cos_s, sin_s,
                                 head_dim, bs, bs)
            k_s, v_s = _cache_roll(cache_k.reshape(bs, window, nkv), cache_v.reshape(bs, window, nkv),
                                   k_n[0], v_n[0], 8)
        if l < n_a:
            w_in_b, w_mix_b = bf(hg_w_in[l]), bf(hg_w_out[l])
            gng = hg_gn_g[l][None]
            a_p, s_p = _hgrn_prompt(l, xp, mod_p(mods[l]), n1g, w_in_b, hg_lower_bounds, gng, heads, 128)
            hg_p.append(s_p)
            proj_s = _proj(xs, mod_s(mods[l]), n1g, w_in_b, 1024)
            a_s, hg_s = _hgrn_sample(l, proj_s, hg_lower_bounds, gng, state_hgrn, hg_s, 8)
        else:
            j = l - n_a
            w_q_b, w_mix_b = bf(w_q[j]), bf(w_o[j])
            qg = jnp.tile(q_norm_g[j], reps)[None]
            a_p = _attn_prompt(xp, mod_p(mods[l]), n1g, w_q_b, qg, cos_p, sin_p, kx_p, vx_p, sinks[j],
                               head_dim, kv_heads, window)
            q_s = _q_sample(xs, mod_s(mods[l]), n1g, w_q_b, qg, cos_s, sin_s, head_dim)
            a_s = _attn_sample(q_s.reshape(bs, d // head_dim, head_dim), k_s, v_s, sinks[j], 8)
            a_s = a_s.reshape(bs, d)
        xp = _post(xp, a_p, mod_p(mods[l]), n2g, w_mix_b, w_up_b, w_dn_b, 256)
        xs = _post(xs[None], a_s[None], mod_s(mods[l])[None], n2g, w_mix_b, w_up_b, w_dn_b, bs)[0]

    shape4 = lambda t: t.reshape(t.shape[0], window, kv_heads, head_dim)
    return (xp, xs.reshape(bs, 1, d), jnp.stack(hg_p), shape4(k_p), shape4(v_p), hg_s,
            shape4(k_s), shape4(v_s))
```

```python
import functools
import math

import numpy as np
import jax
import jax.numpy as jnp
from jax import lax
from jax.experimental import pallas as pl
from jax.experimental.pallas import tpu as pltpu

F32 = jnp.float32
BF16 = jnp.bfloat16

PAST_LEN = 8192
ROPE_THETA = 10000.0
EPS = 1e-6
LOG2E = 1.4426950408889634
LANES = 128
SUBLANES = 8
VMEM_LIMIT = 56 * 1024 * 1024

NT_DIMS = (((1,), (1,)), ((), ()))
TN_DIMS = (((0,), (0,)), ((), ()))


def _dot(a, b):
    return jnp.dot(a, b, preferred_element_type=F32)


def _dot_nt(a, b):
    return lax.dot_general(a, b, NT_DIMS, preferred_element_type=F32)


def _sigmoid(x):
    return 1.0 / (1.0 + jnp.exp(-x))


def _rms(x, g):
    ms = jnp.mean(x * x, axis=-1, keepdims=True)
    return x * lax.rsqrt(ms + EPS) * g


def _modulated_norm(x, gain, shift, scale):
    return _rms(x, gain) * (1.0 + scale) + shift


def _params(*sem):
    return pltpu.CompilerParams(dimension_semantics=sem, vmem_limit_bytes=VMEM_LIMIT)


def _full(shape):
    n = len(shape)
    return pl.BlockSpec(shape, lambda *_: (0,) * n)


def _ada_kernel(c_ref, w_ref, b_ref, o_ref):
    c = c_ref[...]
    a = (c * _sigmoid(c)).astype(BF16)
    o_ref[0] = _dot(a, w_ref[0].astype(BF16)) + b_ref[0]


def _ada(c_all, w, b, tn):
    nl, d, n = w.shape
    r = c_all.shape[0]
    return pl.pallas_call(
        _ada_kernel,
        grid=(nl, n // tn),
        in_specs=[pl.BlockSpec((r, d), lambda l, j: (0, 0)),
                  pl.BlockSpec((1, d, tn), lambda l, j: (l, 0, j)),
                  pl.BlockSpec((1, 1, tn), lambda l, j: (l, 0, j))],
        out_specs=pl.BlockSpec((1, r, tn), lambda l, j: (l, 0, j)),
        out_shape=jax.ShapeDtypeStruct((nl, r, n), F32),
        compiler_params=_params("arbitrary", "arbitrary"),
        name="ada",
    )(c_all, w, b.reshape(nl, 1, n))


def _group_mean_matrix(group):
    r = lax.broadcasted_iota(jnp.int32, (LANES, LANES), 0) // group
    c = lax.broadcasted_iota(jnp.int32, (LANES, LANES), 1) // group
    return jnp.where(r == c, 1.0 / group, 0.0).astype(BF16)


def _head_norm_rope(y, gain, cos, sin, head_dim, out_scale):
    ms = _dot((y * y).astype(BF16), _group_mean_matrix(head_dim))
    yn = y * lax.rsqrt(ms + EPS) * gain
    half = head_dim // 2
    lane = lax.broadcasted_iota(jnp.int32, yn.shape, 1)
    first = (lane % head_dim) < half
    rot = jnp.where(first, pltpu.roll(yn, LANES - half, 1), pltpu.roll(yn, half, 1))
    out = yn * cos + rot * sin
    if out_scale != 1.0:
        out = out * out_scale
    return out


def _lower_bound(raw, layer):
    m = jnp.max(raw, axis=0, keepdims=True)
    e = jnp.exp(raw - m)
    sm = e / jnp.sum(e, axis=0, keepdims=True)
    acc = sm[0:1]
    for j in range(1, layer + 1):
        acc = acc + sm[j:j + 1]
    return acc - sm[0:1]


def _hgrn_gates(proj, lb, d):
    qa = proj[:, 0:d]
    q = qa * _sigmoid(qa)
    fg = lb + (1.0 - lb) * _sigmoid(proj[:, d:2 * d])
    return q, fg


def _gated_group_norm(o, gain, gate_pre):
    ms = jnp.mean(o * o, axis=-1, keepdims=True)
    return o * lax.rsqrt(ms + EPS) * gain * (gate_pre * _sigmoid(gate_pre))


def _hgrn_prompt_kernel(layer, heads, x_ref, mod_ref, n1g_ref, win_ref, lbraw_ref, gng_ref,
                        lev_ref, sums_ref, o_ref, s_ref, st_scr, d_scr, xs_scr, att_scr):
    c = pl.program_id(1)
    rows, d = x_ref.shape[1], x_ref.shape[2]
    dk = d // heads
    n_levels = int(math.log2(rows))

    @pl.when(c == 0)
    def _():
        st_scr[...] = jnp.zeros_like(st_scr)

    mod = mod_ref[0]
    h = _modulated_norm(x_ref[0], n1g_ref[...], mod[:, 0:d], mod[:, d:2 * d])
    proj = _dot(h.astype(BF16), win_ref[...])
    lb = _lower_bound(lbraw_ref[...], layer)
    q, fg = _hgrn_gates(proj, lb, d)
    k = 1.0 - fg
    lf2 = jnp.log(fg) * LOG2E
    hi = lf2.astype(BF16)
    lo = (lf2 - hi.astype(F32)).astype(BF16)
    d_scr[...] = _dot(sums_ref[...], jnp.concatenate([hi, lo], axis=0))
    n_table = d_scr.shape[0] // rows - 1
    lev = lev_ref[...]
    row_id = lax.broadcasted_iota(jnp.int32, (rows, dk), 0)
    upper = [((row_id >> p) & 1) == 1 for p in range(n_table + 1)]
    col = lax.broadcasted_iota(jnp.int32, (1, rows), 1)
    gng = gng_ref[...]

    for hh in range(heads):
        sl = slice(hh * dk, (hh + 1) * dk)
        q_h, k_h = q[:, sl], k[:, sl]
        cum = d_scr[n_table * rows:(n_table + 1) * rows, sl]
        xs_scr[hh, 0] = jnp.where(upper[0], q_h * fg[:, sl], k_h).astype(BF16)
        for p in range(1, n_levels):
            m = 1 << p
            if p <= n_table:
                e = jnp.exp2(d_scr[(p - 1) * rows:p * rows, sl])
                xs_scr[hh, p] = (jnp.where(upper[p], q_h, k_h) * e).astype(BF16)
                continue
            pieces = []
            for r in range(0, rows, m):
                ref = (r // (2 * m)) * 2 * m + m - 1
                if (r // m) % 2:
                    pieces.append(q_h[r:r + m] * jnp.exp2(cum[r:r + m] - cum[ref:ref + 1]))
                else:
                    pieces.append(k_h[r:r + m] * jnp.exp2(cum[ref:ref + 1] - cum[r:r + m]))
            xs_scr[hh, p] = jnp.concatenate(pieces, axis=0).astype(BF16)
        cum_last = cum[rows - 1:rows, :]
        xs_scr[hh, n_levels] = (q_h * jnp.exp2(cum)).astype(BF16)
        xs_scr[hh, n_levels + 1] = (k_h * jnp.exp2(cum_last - cum)).astype(BF16)

    for hh in range(heads):
        xs = xs_scr[hh, 0]
        att = jnp.where(lev == 0, _dot_nt(xs, xs), 0.0)
        for p in range(1, n_levels):
            m = 1 << p
            xs = xs_scr[hh, p]
            pp = _dot_nt(xs, xs)
            if p <= n_table:
                att = jnp.where(lev == p, pp, att)
            else:
                att = jnp.concatenate(
                    [jnp.where((col >= r - m) & (col < r), pp[r:r + m], att[r:r + m]) if (r // m) % 2
                     else att[r:r + m] for r in range(0, rows, m)], axis=0)
        att_scr[hh] = att.astype(BF16)

    for hh in range(heads):
        sl = slice(hh * dk, (hh + 1) * dk)
        v_f = proj[:, 2 * d + hh * dk:2 * d + (hh + 1) * dk]
        v_h = v_f.astype(BF16)
        cum_last = d_scr[(n_table + 1) * rows - 1:(n_table + 1) * rows, sl]
        st = st_scr[hh]
        o_h = (_dot(jnp.concatenate([att_scr[hh], xs_scr[hh, n_levels]], axis=1),
                    jnp.concatenate([v_h, st.T.astype(BF16)], axis=0))
               + jnp.sum(q[:, sl] * k[:, sl], axis=-1, keepdims=True) * v_f)
        st_scr[hh] = st * jnp.exp2(cum_last) + lax.dot_general(
            v_h, xs_scr[hh, n_levels + 1], TN_DIMS, preferred_element_type=F32)
        g_pre = proj[:, 3 * d + hh * dk:3 * d + (hh + 1) * dk]
        o_ref[0, :, sl] = _gated_group_norm(o_h, gng, g_pre)

    @pl.when(c == pl.num_programs(1) - 1)
    def _():
        for hh in range(heads):
            s_ref[0, hh] = st_scr[hh].T


def _level_table(rows):
    t = np.arange(rows)[:, None]
    s = np.arange(rows)[None, :]
    x = t ^ s
    lev = np.where(t > s, np.floor(np.log2(np.maximum(x, 1))).astype(np.int32), np.where(t == s, -1, -2))
    return jnp.asarray(lev, dtype=jnp.int32)


def _sum_table(rows):
    t = np.arange(rows)[:, None]
    j = np.arange(rows)[None, :]
    blocks = []
    for p in range(1, int(math.log2(SUBLANES))):
        m = 1 << p
        ref = (t // (2 * m)) * (2 * m) + m - 1
        up = ((t >> p) & 1) == 1
        blocks.append(np.where(up, (j > ref) & (j <= t), (j > t) & (j <= ref)))
    blocks.append(j <= t)
    table = np.concatenate(blocks, axis=0).astype(np.float32)
    return jnp.asarray(np.concatenate([table, table], axis=1), dtype=BF16)


def _hgrn_prompt(layer, x, mod, n1g, w_in, lb_raw, gn_g, heads, chunk):
    b, L, d = x.shape
    dk = d // heads
    sums = _sum_table(chunk)
    return pl.pallas_call(
        functools.partial(_hgrn_prompt_kernel, layer, heads),
        grid=(b, L // chunk),
        in_specs=[pl.BlockSpec((1, chunk, d), lambda i, c: (i, c, 0)),
                  pl.BlockSpec((1, 1, mod.shape[2]), lambda i, c: (i, 0, 0)),
                  _full((1, d)), _full(w_in.shape), _full(lb_raw.shape), _full((1, dk)),
                  _full((chunk, chunk)), _full(sums.shape)],
        out_specs=[pl.BlockSpec((1, chunk, d), lambda i, c: (i, c, 0)),
                   pl.BlockSpec((1, heads, dk, dk), lambda i, c: (i, 0, 0, 0))],
        out_shape=[jax.ShapeDtypeStruct((b, L, d), F32),
                   jax.ShapeDtypeStruct((b, heads, dk, dk), F32)],
        scratch_shapes=[pltpu.VMEM((heads, dk, dk), F32), pltpu.VMEM((sums.shape[0], d), F32),
                        pltpu.VMEM((heads, int(math.log2(chunk)) + 2, chunk, dk), BF16),
                        pltpu.VMEM((heads, chunk, chunk), BF16)],
        compiler_params=_params("arbitrary", "arbitrary"),
        name=f"hgrn_prompt_{layer}",
    )(x, mod, n1g, w_in, lb_raw, gn_g, _level_table(chunk), sums)


def _post_kernel(x_ref, a_ref, mod_ref, n2g_ref, wo_ref, wup_ref, wdn_ref, o_ref):
    d = x_ref.shape[2]
    mod = mod_ref[0]
    g1, sh2, sc2, g2 = (mod[:, j * d:(j + 1) * d] for j in range(2, 6))
    x1 = x_ref[0] + g1 * _dot(a_ref[0].astype(BF16), wo_ref[...])
    h2 = _modulated_norm(x1, n2g_ref[...], sh2, sc2).astype(BF16)
    y = None
    for c in range(0, wup_ref.shape[1], d):
        u = jnp.square(jnp.maximum(_dot(h2, wup_ref[:, c:c + d]), 0.0)).astype(BF16)
        part = _dot(u, wdn_ref[c:c + d, :])
        y = part if y is None else y + part
    o_ref[0] = x1 + g2 * y


def _resident(shape):
    n = len(shape)
    return pl.BlockSpec(shape, lambda *_: (0,) * n, pipeline_mode=pl.Buffered(1))


def _post(x, a, mod, n2g, w_o, w_up, w_dn, tm):
    nb, L, d = x.shape
    r = mod.shape[1]
    return pl.pallas_call(
        _post_kernel,
        grid=(nb, L // tm),
        in_specs=[pl.BlockSpec((1, tm, d), lambda i, j: (i, j, 0)),
                  pl.BlockSpec((1, tm, d), lambda i, j: (i, j, 0)),
                  pl.BlockSpec((1, r, mod.shape[2]), lambda i, j: (i, 0, 0)),
                  _full((1, d)), _resident(w_o.shape), _resident(w_up.shape), _resident(w_dn.shape)],
        out_specs=pl.BlockSpec((1, tm, d), lambda i, j: (i, j, 0)),
        out_shape=jax.ShapeDtypeStruct((nb, L, d), F32),
        compiler_params=_params("arbitrary", "arbitrary"),
        name="post_mlp",
    )(x, a, mod, n2g, w_o, w_up, w_dn)


def _kv_kernel(head_dim, window, x_ref, mod_ref, g_ref, w_ref, kg_ref, cos_ref, sin_ref,
               k_ref, v_ref, kx_ref, vx_ref):
    tm, d = x_ref.shape[1], x_ref.shape[2]
    mod = mod_ref[0]
    hk = _modulated_norm(x_ref[0], g_ref[...], mod[:, 0:d], mod[:, d:2 * d])
    y = _dot(hk.astype(BF16), w_ref[...])
    n = y.shape[1] // 2
    cos, sin, kg = cos_ref[...], sin_ref[...], kg_ref[...]
    low = lax.broadcasted_iota(jnp.int32, (tm, LANES), 1) < head_dim
    k_blocks, v_blocks = [], []
    for j in range(n // LANES):
        kb = _head_norm_rope(y[:, j * LANES:(j + 1) * LANES], kg, cos, sin, head_dim, 1.0)
        vb = y[:, n + j * LANES:n + (j + 1) * LANES]
        k_blocks.append(kb)
        v_blocks.append(vb)
        kx_ref[0, :, j * LANES:(j + 1) * LANES] = kb.astype(BF16)
        kx_ref[0, :, n + j * LANES:n + (j + 1) * LANES] = pltpu.roll(kb, head_dim, 1).astype(BF16)
        vs = pltpu.roll(vb, head_dim, 1)
        variants = (jnp.where(low, vb, 0.0), jnp.where(low, 0.0, vs),
                    jnp.where(low, vs, 0.0), jnp.where(low, 0.0, vb))
        for i, var in enumerate(variants):
            off = (4 * j + i) * LANES
            vx_ref[0, off:off + LANES, :] = var.T.astype(BF16)

    @pl.when(pl.program_id(1) == pl.num_programs(1) - 1)
    def _():
        for j in range(n // LANES):
            k_ref[0, :, j * LANES:(j + 1) * LANES] = k_blocks[j][tm - window:, :]
            v_ref[0, :, j * LANES:(j + 1) * LANES] = v_blocks[j][tm - window:, :]


def _kv(x, mod, g, w_kv, kg, cos, sin, head_dim, window, tm):
    nb, L, d = x.shape
    r = mod.shape[1]
    n = w_kv.shape[1] // 2
    rope_rows = tm if cos.shape[0] > 1 else 1
    rope_map = (lambda i, j: (j, 0)) if cos.shape[0] > 1 else (lambda i, j: (0, 0))
    last = lambda i, j: (i, 0, 0)
    tile = lambda i, j: (i, j, 0)
    return pl.pallas_call(
        functools.partial(_kv_kernel, head_dim, window),
        grid=(nb, L // tm),
        in_specs=[pl.BlockSpec((1, tm, d), tile),
                  pl.BlockSpec((1, r, mod.shape[2]), lambda i, j: (i, 0, 0)),
                  _full((1, d)), _full(w_kv.shape), _full((1, LANES)),
                  pl.BlockSpec((rope_rows, LANES), rope_map),
                  pl.BlockSpec((rope_rows, LANES), rope_map)],
        out_specs=[pl.BlockSpec((1, window, n), last), pl.BlockSpec((1, window, n), last),
                   pl.BlockSpec((1, tm, 2 * n), tile), pl.BlockSpec((1, 4 * n, tm), lambda i, j: (i, 0, j))],
        out_shape=[jax.ShapeDtypeStruct((nb, window, n), F32), jax.ShapeDtypeStruct((nb, window, n), F32),
                   jax.ShapeDtypeStruct((nb, L, 2 * n), BF16), jax.ShapeDtypeStruct((nb, 4 * n, L), BF16)],
        compiler_params=_params("arbitrary", "arbitrary"),
        name="kv_proj",
    )(x, mod, g, w_kv, kg, cos, sin)


def _q_proj(x, mod, n1g, wq, qg, cos, sin, head_dim):
    d = x.shape[1]
    h = _modulated_norm(x, n1g, mod[:, 0:d], mod[:, d:2 * d])
    y = _dot(h.astype(BF16), wq)
    scale = 1.0 / math.sqrt(head_dim)
    return [_head_norm_rope(y[:, j * LANES:(j + 1) * LANES], qg, cos, sin, head_dim, scale)
            for j in range(y.shape[1] // LANES)]


def _attn_prompt_kernel(head_dim, kv_heads, x_ref, mod_ref, n1g_ref, wq_ref, qg_ref, cos_ref, sin_ref,
                        kp_ref, kc_ref, vp_ref, vc_ref, sink_ref, o_ref, s_scr, p_scr):
    w = x_ref.shape[1]
    nkv = kv_heads * head_dim
    q_blocks = _q_proj(x_ref[0], mod_ref[0], n1g_ref[...], wq_ref[...], qg_ref[...],
                       cos_ref[...], sin_ref[...], head_dim)
    group = (2 * len(q_blocks)) // kv_heads
    key = lax.broadcasted_iota(jnp.int32, (w, 2 * w), 0)
    qry = lax.broadcasted_iota(jnp.int32, (w, 2 * w), 1)
    own = key <= (qry & (w - 1))
    first = lax.broadcasted_iota(jnp.int32, (1, 2 * w), 1) < w
    low = lax.broadcasted_iota(jnp.int32, (w, LANES), 1) < head_dim
    no_prev = jnp.where(pl.program_id(1) == 0, -jnp.inf, 0.0)
    pairs = [(kh, half) for kh in range(kv_heads) for half in range(2)]
    for n, (kh, half) in enumerate(pairs):
        ha, hb = kh * group + half, kh * group + 2 + half
        mask = (lambda t: jnp.where(low, t, 0.0)) if half == 0 else (lambda t: jnp.where(low, 0.0, t))
        qcat = jnp.concatenate([mask(q_blocks[ha // 2]), mask(q_blocks[hb // 2])], axis=0).astype(BF16)
        koff = (kh // 2) * LANES + (0 if kh % 2 == half else nkv)
        kcat = jnp.concatenate([kp_ref[0, :, koff:koff + LANES], kc_ref[0, :, koff:koff + LANES]], axis=0)
        s_t = _dot_nt(kcat, qcat)
        s_scr[n] = jnp.where(own, s_t[w:], s_t[:w] + no_prev)
    for n, (kh, half) in enumerate(pairs):
        sc = s_scr[n]
        sink = jnp.where(first, sink_ref[kh * group + half], sink_ref[kh * group + 2 + half])
        m = jnp.maximum(jnp.max(sc, axis=0, keepdims=True), sink)
        p = jnp.exp(sc - m)
        pn = p * (1.0 / (jnp.sum(p, axis=0, keepdims=True) + jnp.exp(sink - m)))
        p_scr[n, 0:w] = jnp.where(own, pn, 0.0).astype(BF16)
        p_scr[n, w:2 * w] = jnp.where(own, 0.0, pn).astype(BF16)
    for kh in range(kv_heads):
        out_t = None
        for half in range(2):
            voff = (2 * kh + half) * LANES
            v_t = jnp.concatenate([vc_ref[0, voff:voff + LANES, :], vp_ref[0, voff:voff + LANES, :]], axis=1)
            part = _dot(v_t, p_scr[2 * kh + half])
            out_t = part if out_t is None else out_t + part
        for i in range(2):
            o_ref[0, :, (2 * kh + i) * LANES:(2 * kh + i + 1) * LANES] = out_t[:, i * w:(i + 1) * w].T


def _attn_prompt(x, mod, n1g, wq, qg, cos, sin, kx, vx, sinks, head_dim, kv_heads, w):
    b, L, d = x.shape
    prev = lambda i, j: (i, jnp.maximum(j - 1, 0), 0)
    cur = lambda i, j: (i, j, 0)
    return pl.pallas_call(
        functools.partial(_attn_prompt_kernel, head_dim, kv_heads),
        grid=(b, L // w),
        in_specs=[pl.BlockSpec((1, w, d), cur),
                  pl.BlockSpec((1, 1, mod.shape[2]), lambda i, j: (i, 0, 0)),
                  _full((1, d)), _full(wq.shape), _full((1, LANES)),
                  pl.BlockSpec((w, LANES), lambda i, j: (j, 0)),
                  pl.BlockSpec((w, LANES), lambda i, j: (j, 0)),
                  pl.BlockSpec((1, w, kx.shape[2]), prev), pl.BlockSpec((1, w, kx.shape[2]), cur),
                  pl.BlockSpec((1, vx.shape[1], w), lambda i, j: (i, 0, jnp.maximum(j - 1, 0))),
                  pl.BlockSpec((1, vx.shape[1], w), lambda i, j: (i, 0, j)),
                  pl.BlockSpec(memory_space=pltpu.SMEM)],
        out_specs=pl.BlockSpec((1, w, d), cur),
        out_shape=jax.ShapeDtypeStruct((b, L, d), F32),
        scratch_shapes=[pltpu.VMEM((2 * kv_heads, w, 2 * w), F32),
                        pltpu.VMEM((2 * kv_heads, 2 * w, 2 * w), BF16)],
        compiler_params=_params("arbitrary", "arbitrary"),
        name="attn_prompt",
    )(x, mod, n1g, wq, qg, cos, sin, kx, kx, vx, vx, sinks)


def _proj_kernel(x_ref, mod_ref, g_ref, w_ref, o_ref):
    d = x_ref.shape[1]
    mod = mod_ref[...]
    h = _modulated_norm(x_ref[...], g_ref[...], mod[:, 0:d], mod[:, d:2 * d])
    o_ref[...] = _dot(h.astype(BF16), w_ref[...])


def _proj(x, mod, g, w, tn):
    r, d = x.shape
    n = w.shape[1]
    return pl.pallas_call(
        _proj_kernel,
        grid=(n // tn,),
        in_specs=[_full((r, d)), _full(mod.shape), _full((1, d)),
                  pl.BlockSpec((d, tn), lambda j: (0, j))],
        out_specs=pl.BlockSpec((r, tn), lambda j: (0, j)),
        out_shape=jax.ShapeDtypeStruct((r, n), F32),
        compiler_params=_params("arbitrary"),
        name="proj_sample",
    )(x, mod, g, w)


def _q_sample_kernel(head_dim, x_ref, mod_ref, n1g_ref, wq_ref, qg_ref, cos_ref, sin_ref, o_ref):
    blocks = _q_proj(x_ref[...], mod_ref[...], n1g_ref[...], wq_ref[...], qg_ref[...],
                     cos_ref[...], sin_ref[...], head_dim)
    for j, blk in enumerate(blocks):
        o_ref[:, j * LANES:(j + 1) * LANES] = blk


def _q_sample(x, mod, n1g, wq, qg, cos, sin, head_dim):
    r, d = x.shape
    return pl.pallas_call(
        functools.partial(_q_sample_kernel, head_dim),
        grid=(1,),
        in_specs=[_full((r, d)), _full(mod.shape), _full((1, d)), _full(wq.shape),
                  _full((1, LANES)), _full((1, LANES)), _full((1, LANES))],
        out_specs=_full((r, wq.shape[1])),
        out_shape=jax.ShapeDtypeStruct((r, wq.shape[1]), F32),
        compiler_params=_params("arbitrary"),
        name="q_sample",
    )(x, mod, n1g, wq, qg, cos, sin)


def _hgrn_sample_kernel(layer, heads, tb, aliased, proj_ref, lbraw_ref, gng_ref, s_ref, *rest):
    o_ref, so_ref, stack_scr, acc_scr = rest[1:] if aliased else rest
    i = pl.program_id(0)
    rows = proj_ref.shape[0]
    d = proj_ref.shape[1] // 4
    dk = d // heads

    @pl.when(i == 0)
    def _():
        lb = _lower_bound(lbraw_ref[...], layer)
        q, fg = _hgrn_gates(proj_ref[:, 0:2 * d], lb, d)
        for hh in range(heads):
            sl = slice(hh * dk, (hh + 1) * dk)
            ft = fg[:, sl].T
            hi = ft.astype(BF16)
            stack_scr[hh, 0:dk] = hi
            stack_scr[hh, dk:2 * dk] = (ft - hi.astype(F32)).astype(BF16)
            stack_scr[hh, 2 * dk:3 * dk] = (1.0 - ft).astype(BF16)
            stack_scr[hh, 3 * dk:4 * dk] = q[:, sl].T.astype(BF16)

    token_row = lax.broadcasted_iota(jnp.int32, (rows, dk), 0)
    sub = lax.broadcasted_iota(jnp.int32, (tb, dk), 0)
    base = pl.multiple_of(i * tb, tb)
    v_rows = proj_ref[pl.ds(base, tb), 2 * d:3 * d]
    o_rows = [jnp.zeros((tb, dk), F32) for _ in range(heads)]
    for t in range(tb):
        onehot = jnp.where(token_row == base + t, 1.0, 0.0).astype(BF16)
        for hh in range(heads):
            bc = _dot(stack_scr[hh], onehot)
            f_b = bc[0:dk] + bc[dk:2 * dk]
            k_b = bc[2 * dk:3 * dk]
            q_b = bc[3 * dk:4 * dk]
            s_new = f_b * s_ref[0, t, hh] + k_b * v_rows[t:t + 1, hh * dk:(hh + 1) * dk]
            so_ref[0, t, hh] = s_new
            o_rows[hh] = jnp.where(sub == t, jnp.sum(q_b * s_new, axis=0, keepdims=True), o_rows[hh])
    for hh in range(heads):
        acc_scr[pl.ds(base, tb), hh * dk:(hh + 1) * dk] = o_rows[hh]

    @pl.when(i == pl.num_programs(0) - 1)
    def _():
        gng = gng_ref[...]
        for hh in range(heads):
            sl = slice(hh * dk, (hh + 1) * dk)
            o_ref[:, sl] = _gated_group_norm(acc_scr[:, sl], gng,
                                             proj_ref[:, 3 * d + hh * dk:3 * d + (hh + 1) * dk])


def _hgrn_sample(layer, proj, lb_raw, gn_g, state, new_state, tb):
    n_layers, nb, heads, dk, dv = state.shape
    d = proj.shape[1] // 4
    aliased = new_state is not None
    blk = pl.BlockSpec((1, tb, heads, dk, dv), lambda i: (layer, i, 0, 0, 0))
    in_specs = [_full(proj.shape), _full(lb_raw.shape), _full((1, dk)), blk]
    args = [proj, lb_raw, gn_g, state]
    if aliased:
        in_specs.append(pl.BlockSpec(memory_space=pl.ANY))
        args.append(new_state)
    return pl.pallas_call(
        functools.partial(_hgrn_sample_kernel, layer, heads, tb, aliased),
        grid=(nb // tb,),
        in_specs=in_specs,
        out_specs=[_full((nb, d)), blk],
        out_shape=[jax.ShapeDtypeStruct((nb, d), F32), jax.ShapeDtypeStruct(state.shape, F32)],
        scratch_shapes=[pltpu.VMEM((heads, 4 * dk, nb), BF16), pltpu.VMEM((nb, d), F32)],
        input_output_aliases={4: 1} if aliased else {},
        compiler_params=_params("arbitrary"),
        name=f"hgrn_sample_{layer}",
    )(*args)


def _cache_roll_kernel(ck_ref, cv_ref, kn_ref, vn_ref, ko_ref, vo_ref):
    w = ck_ref.shape[1]
    ko_ref[:, 0:w - 1, :] = ck_ref[:, 1:w, :]
    ko_ref[:, w - 1:w, :] = kn_ref[...]
    vo_ref[:, 0:w - 1, :] = cv_ref[:, 1:w, :]
    vo_ref[:, w - 1:w, :] = vn_ref[...]


def _cache_roll(ck, cv, kn, vn, tb):
    nb, w, n = ck.shape
    blk = pl.BlockSpec((tb, w, n), lambda i: (i, 0, 0))
    new = pl.BlockSpec((tb, 1, n), lambda i: (i, 0, 0))
    return pl.pallas_call(
        _cache_roll_kernel,
        grid=(nb // tb,),
        in_specs=[blk, blk, new, new],
        out_specs=[blk, blk],
        out_shape=[jax.ShapeDtypeStruct(ck.shape, F32), jax.ShapeDtypeStruct(cv.shape, F32)],
        compiler_params=_params("arbitrary"),
        name="cache_roll",
    )(ck, cv, kn.reshape(nb, 1, n), vn.reshape(nb, 1, n))


def _attn_sample_kernel(tb, q_ref, k_ref, v_ref, sink_ref, o_ref):
    heads, head_dim = q_ref.shape[1], q_ref.shape[2]
    nkv = k_ref.shape[2]
    group = heads // (nkv // head_dim)
    r = lax.broadcasted_iota(jnp.int32, (heads, nkv), 0) // group
    c = lax.broadcasted_iota(jnp.int32, (heads, nkv), 1) // head_dim
    own = r == c
    sink = sink_ref[...]

    for t in range(tb):
        qb = q_ref[t]
        qe = jnp.where(own, jnp.concatenate([qb] * (nkv // head_dim), axis=-1), 0.0).astype(BF16)
        s = _dot_nt(qe, k_ref[t].astype(BF16))
        m = jnp.maximum(jnp.max(s, axis=-1, keepdims=True), sink)
        p = jnp.exp(s - m)
        denom = jnp.sum(p, axis=-1, keepdims=True) + jnp.exp(sink - m)
        pv = jnp.where(own, _dot(p.astype(BF16), v_ref[t].astype(BF16)), 0.0)
        o = pv[:, 0:head_dim]
        for j in range(1, nkv // head_dim):
            o = o + pv[:, j * head_dim:(j + 1) * head_dim]
        o_ref[t] = o / denom


def _attn_sample(q3, k, v, sinks, tb):
    nb, heads, head_dim = q3.shape
    w, nkv = k.shape[1], k.shape[2]
    return pl.pallas_call(
        functools.partial(_attn_sample_kernel, tb),
        grid=(nb // tb,),
        in_specs=[pl.BlockSpec((tb, heads, head_dim), lambda i: (i, 0, 0)),
                  pl.BlockSpec((tb, w, nkv), lambda i: (i, 0, 0)),
                  pl.BlockSpec((tb, w, nkv), lambda i: (i, 0, 0)),
                  _full((heads, 1))],
        out_specs=pl.BlockSpec((tb, heads, head_dim), lambda i: (i, 0, 0)),
        out_shape=jax.ShapeDtypeStruct(q3.shape, F32),
        compiler_params=_params("arbitrary"),
        name="attn_sample",
    )(q3, k, v, sinks.reshape(heads, 1))


def _rope_tables(pos, head_dim):
    half = head_dim // 2
    inv = ROPE_THETA ** (-jnp.arange(half, dtype=F32) / half)
    ang = pos.astype(F32)[:, None] * inv[None, :]
    cos, sin = jnp.cos(ang), jnp.sin(ang)
    reps = LANES // head_dim
    return (jnp.tile(jnp.concatenate([cos, cos], axis=1), (1, reps)),
            jnp.tile(jnp.concatenate([-sin, sin], axis=1), (1, reps)))


def kernel(x_prompt, x_sample, c_prompt, c_sample, state_hgrn, cache_k, cache_v, w_ada, b_ada, norm1_g, norm2_g, hg_w_in, hg_w_out, hg_lower_bounds, hg_gn_g, kv_w_ada, kv_b_ada, kv_norm_g, w_kv, k_norm_g, w_q, q_norm_g, sinks, w_o, w_up, w_down):
    bp, Lp, d = x_prompt.shape
    bs = x_sample.shape[0]
    depth = w_ada.shape[0]
    n_a = hg_w_in.shape[0]
    heads = state_hgrn.shape[2]
    window, kv_heads, head_dim = cache_k.shape[1], cache_k.shape[2], cache_k.shape[3]
    assert LANES == 2 * head_dim and kv_heads % 2 == 0 and Lp % window == 0
    assert window & (window - 1) == 0 and (d // head_dim) // kv_heads == 4
    nkv = kv_heads * head_dim
    reps = LANES // head_dim
    bf = lambda t: t.astype(BF16)

    pad = (-(bs + bp)) % 16
    c_all = jnp.concatenate([c_sample, c_prompt, jnp.zeros((pad, d), F32)], axis=0)
    mods = _ada(c_all, w_ada, b_ada, 1536)
    kv_mods = _ada(c_all, kv_w_ada[None], kv_b_ada[None], 1024)[0]
    mod_p = lambda m: m[bs:bs + bp][:, None, :]
    mod_s = lambda m: m[0:bs]

    cos_p, sin_p = _rope_tables(jnp.arange(Lp), head_dim)
    cos_s, sin_s = _rope_tables(jnp.full((1,), PAST_LEN), head_dim)
    kg = jnp.tile(k_norm_g, reps)[None]
    w_kv_b = bf(w_kv)

    xp = x_prompt
    xs = x_sample.reshape(bs, d)
    hg_p, hg_s = [], None
    k_p = v_p = kx_p = vx_p = k_s = v_s = None
    for l in range(depth):
        n1g, n2g = norm1_g[l][None], norm2_g[l][None]
        w_up_b, w_dn_b = bf(w_up[l]), bf(w_down[l])
        if l == n_a:
            k_p, v_p, kx_p, vx_p = _kv(xp, mod_p(kv_mods), kv_norm_g[None], w_kv_b, kg, cos_p, sin_p,
                                       head_dim, window, 512)
            k_n, v_n, _, _ = _kv(xs[None], mod_s(kv_mods)[None], kv_norm_g[None], w_kv_b, kg, cos_s, sin_s,
                                 head_dim, bs, bs)
            k_s, v_s = _cache_roll(cache_k.reshape(bs, window, nkv), cache_v.reshape(bs, window, nkv),
                                   k_n[0], v_n[0], 8)
        if l < n_a:
            w_in_b, w_mix_b = bf(hg_w_in[l]), bf(hg_w_out[l])
            gng = hg_gn_g[l][None]
            a_p, s_p = _hgrn_prompt(l, xp, mod_p(mods[l]), n1g, w_in_b, hg_lower_bounds, gng, heads, 128)
            hg_p.append(s_p)
            proj_s = _proj(xs, mod_s(mods[l]), n1g, w_in_b, 1024)
            a_s, hg_s = _hgrn_sample(l, proj_s, hg_lower_bounds, gng, state_hgrn, hg_s, 8)
        else:
            j = l - n_a
            w_q_b, w_mix_b = bf(w_q[j]), bf(w_o[j])
            qg = jnp.tile(q_norm_g[j], reps)[None]
            a_p = _attn_prompt(xp, mod_p(mods[l]), n1g, w_q_b, qg, cos_p, sin_p, kx_p, vx_p, sinks[j],
                               head_dim, kv_heads, window)
            q_s = _q_sample(xs, mod_s(mods[l]), n1g, w_q_b, qg, cos_s, sin_s, head_dim)
            a_s = _attn_sample(q_s.reshape(bs, d // head_dim, head_dim), k_s, v_s, sinks[j], 8)
            a_s = a_s.reshape(bs, d)
        xp = _post(xp, a_p, mod_p(mods[l]), n2g, w_mix_b, w_up_b, w_dn_b, 512)
        xs = _post(xs[None], a_s[None], mod_s(mods[l])[None], n2g, w_mix_b, w_up_b, w_dn_b, bs)[0]

    shape4 = lambda t: t.reshape(t.shape[0], window, kv_heads, head_dim)
    return (xp, xs.reshape(bs, 1, d), jnp.stack(hg_p), shape4(k_p), shape4(v_p), hg_s,
            shape4(k_s), shape4(v_s))
```

```python
import functools
import math

import numpy as np
import jax
import jax.numpy as jnp
from jax import lax
from jax.experimental import pallas as pl
from jax.experimental.pallas import tpu as pltpu

F32 = jnp.float32
BF16 = jnp.bfloat16

PAST_LEN = 8192
ROPE_THETA = 10000.0
EPS = 1e-6
LOG2E = 1.4426950408889634
LANES = 128
SUBLANES = 8
VMEM_LIMIT = 56 * 1024 * 1024

NT_DIMS = (((1,), (1,)), ((), ()))
TN_DIMS = (((0,), (0,)), ((), ()))


def _dot(a, b):
    return jnp.dot(a, b, preferred_element_type=F32)


def _dot_nt(a, b):
    return lax.dot_general(a, b, NT_DIMS, preferred_element_type=F32)


def _sigmoid(x):
    return 1.0 / (1.0 + jnp.exp(-x))


def _rms(x, g):
    ms = jnp.mean(x * x, axis=-1, keepdims=True)
    return x * lax.rsqrt(ms + EPS) * g


def _modulated_norm(x, gain, shift, scale):
    return _rms(x, gain) * (1.0 + scale) + shift


def _params(*sem):
    return pltpu.CompilerParams(dimension_semantics=sem, vmem_limit_bytes=VMEM_LIMIT)


def _full(shape):
    n = len(shape)
    return pl.BlockSpec(shape, lambda *_: (0,) * n)


def _ada_kernel(c_ref, w_ref, b_ref, os_ref, op_ref):
    bs, bp = os_ref.shape[1], op_ref.shape[1]
    c = c_ref[...]
    a = (c * _sigmoid(c)).astype(BF16)
    res = _dot(a, w_ref[0].astype(BF16)) + b_ref[0]
    os_ref[0] = res[0:bs]
    for r in range(bp):
        op_ref[0, r] = jnp.broadcast_to(res[bs + r:bs + r + 1], op_ref.shape[2:])


def _ada(c_all, bs, bp, w, b, tn):
    nl, d, n = w.shape
    r = c_all.shape[0]
    return pl.pallas_call(
        _ada_kernel,
        grid=(nl, n // tn),
        in_specs=[pl.BlockSpec((r, d), lambda l, j: (0, 0)),
                  pl.BlockSpec((1, d, tn), lambda l, j: (l, 0, j)),
                  pl.BlockSpec((1, 1, tn), lambda l, j: (l, 0, j))],
        out_specs=[pl.BlockSpec((1, bs, tn), lambda l, j: (l, 0, j)),
                   pl.BlockSpec((1, bp, SUBLANES, tn), lambda l, j: (l, 0, 0, j))],
        out_shape=[jax.ShapeDtypeStruct((nl, bs, n), F32),
                   jax.ShapeDtypeStruct((nl, bp, SUBLANES, n), F32)],
        compiler_params=_params("arbitrary", "arbitrary"),
        name="ada",
    )(c_all, w, b.reshape(nl, 1, n))


def _mod_rows(mod_ref):
    return mod_ref[0, 0, 0:1, :] if len(mod_ref.shape) == 4 else mod_ref[0]


def _mod_spec(mods, layer):
    if mods.ndim == 4:
        return pl.BlockSpec((1, 1) + mods.shape[2:], lambda i, *_: (layer, i, 0, 0))
    return pl.BlockSpec((1,) + mods.shape[1:], lambda *_: (layer, 0, 0))


def _layer_spec(stack, layer):
    zeros = (0,) * (stack.ndim - 1)
    return pl.BlockSpec((1,) + stack.shape[1:], lambda *_: (layer,) + zeros, pipeline_mode=pl.Buffered(1))


def _group_mean_matrix(group):
    r = lax.broadcasted_iota(jnp.int32, (LANES, LANES), 0) // group
    c = lax.broadcasted_iota(jnp.int32, (LANES, LANES), 1) // group
    return jnp.where(r == c, 1.0 / group, 0.0).astype(BF16)


def _head_norm_rope(y, gain, cos, sin, head_dim, out_scale):
    ms = _dot((y * y).astype(BF16), _group_mean_matrix(head_dim))
    yn = y * lax.rsqrt(ms + EPS) * gain
    half = head_dim // 2
    lane = lax.broadcasted_iota(jnp.int32, yn.shape, 1)
    first = (lane % head_dim) < half
    rot = jnp.where(first, pltpu.roll(yn, LANES - half, 1), pltpu.roll(yn, half, 1))
    out = yn * cos + rot * sin
    if out_scale != 1.0:
        out = out * out_scale
    return out


def _lower_bound(raw, layer):
    m = jnp.max(raw, axis=0, keepdims=True)
    e = jnp.exp(raw - m)
    sm = e / jnp.sum(e, axis=0, keepdims=True)
    acc = sm[0:1]
    for j in range(1, layer + 1):
        acc = acc + sm[j:j + 1]
    return acc - sm[0:1]


def _hgrn_gates(proj, lb, d):
    qa = proj[:, 0:d]
    q = qa * _sigmoid(qa)
    fg = lb + (1.0 - lb) * _sigmoid(proj[:, d:2 * d])
    return q, fg


def _gated_group_norm(o, gain, gate_pre):
    ms = jnp.mean(o * o, axis=-1, keepdims=True)
    return o * lax.rsqrt(ms + EPS) * gain * (gate_pre * _sigmoid(gate_pre))


def _hgrn_prompt_kernel(layer, heads, x_ref, mod_ref, n1g_ref, win_ref, lbraw_ref, gng_ref,
                        lev_ref, sums_ref, o_ref, s_ref, st_scr, d_scr, xs_scr, att_scr):
    c = pl.program_id(1)
    total, d = x_ref.shape[1], x_ref.shape[2]
    rows = lev_ref.shape[0]
    dk = d // heads
    n_levels = int(math.log2(rows))

    @pl.when(c == 0)
    def _():
        st_scr[...] = jnp.zeros_like(st_scr)

    mod = _mod_rows(mod_ref)
    h = _modulated_norm(x_ref[0], n1g_ref[layer:layer + 1, :], mod[:, 0:d], mod[:, d:2 * d])
    proj_all = _dot(h.astype(BF16), win_ref[0])
    lb = _lower_bound(lbraw_ref[...], layer)
    q_all, fg_all = _hgrn_gates(proj_all, lb, d)
    lf2_all = jnp.log(fg_all) * LOG2E
    n_table = d_scr.shape[0] // rows - 1
    lev = lev_ref[...]
    row_id = lax.broadcasted_iota(jnp.int32, (rows, dk), 0)
    upper = [((row_id >> p) & 1) == 1 for p in range(n_table + 1)]
    col = lax.broadcasted_iota(jnp.int32, (1, rows), 1)
    gng = gng_ref[layer:layer + 1, :]

    for r0 in range(0, total, rows):
        proj, q, fg = (t[r0:r0 + rows] for t in (proj_all, q_all, fg_all))
        k = 1.0 - fg
        lf2 = lf2_all[r0:r0 + rows]
        hi = lf2.astype(BF16)
        lo = (lf2 - hi.astype(F32)).astype(BF16)
        d_scr[...] = _dot(sums_ref[...], jnp.concatenate([hi, lo], axis=0))

        for hh in range(heads):
            sl = slice(hh * dk, (hh + 1) * dk)
            q_h, k_h = q[:, sl], k[:, sl]
            cum = d_scr[n_table * rows:(n_table + 1) * rows, sl]
            xs_scr[hh, 0] = jnp.where(upper[0], q_h * fg[:, sl], k_h).astype(BF16)
            for p in range(1, n_levels):
                m = 1 << p
                if p <= n_table:
                    e = jnp.exp2(d_scr[(p - 1) * rows:p * rows, sl])
                    xs_scr[hh, p] = (jnp.where(upper[p], q_h, k_h) * e).astype(BF16)
                    continue
                pieces = []
                for r in range(0, rows, m):
                    ref = (r // (2 * m)) * 2 * m + m - 1
                    if (r // m) % 2:
                        pieces.append(q_h[r:r + m] * jnp.exp2(cum[r:r + m] - cum[ref:ref + 1]))
                    else:
                        pieces.append(k_h[r:r + m] * jnp.exp2(cum[ref:ref + 1] - cum[r:r + m]))
                xs_scr[hh, p] = jnp.concatenate(pieces, axis=0).astype(BF16)
            cum_last = cum[rows - 1:rows, :]
            xs_scr[hh, n_levels] = (q_h * jnp.exp2(cum)).astype(BF16)
            xs_scr[hh, n_levels + 1] = (k_h * jnp.exp2(cum_last - cum)).astype(BF16)

        for hh in range(heads):
            xs = xs_scr[hh, 0]
            att = jnp.where(lev == 0, _dot_nt(xs, xs), 0.0)
            for p in range(1, n_levels):
                m = 1 << p
                xs = xs_scr[hh, p]
                pp = _dot_nt(xs, xs)
                if p <= n_table:
                    att = jnp.where(lev == p, pp, att)
                else:
                    att = jnp.concatenate(
                        [jnp.where((col >= r - m) & (col < r), pp[r:r + m], att[r:r + m]) if (r // m) % 2
                         else att[r:r + m] for r in range(0, rows, m)], axis=0)
            att_scr[hh] = att.astype(BF16)

        for hh in range(heads):
            sl = slice(hh * dk, (hh + 1) * dk)
            v_f = proj[:, 2 * d + hh * dk:2 * d + (hh + 1) * dk]
            v_h = v_f.astype(BF16)
            cum_last = d_scr[(n_table + 1) * rows - 1:(n_table + 1) * rows, sl]
            st = st_scr[hh]
            o_h = (_dot(jnp.concatenate([att_scr[hh], xs_scr[hh, n_levels]], axis=1),
                        jnp.concatenate([v_h, st.T.astype(BF16)], axis=0))
                   + jnp.sum(q[:, sl] * k[:, sl], axis=-1, keepdims=True) * v_f)
            st_scr[hh] = st * jnp.exp2(cum_last) + lax.dot_general(
                v_h, xs_scr[hh, n_levels + 1], TN_DIMS, preferred_element_type=F32)
            g_pre = proj[:, 3 * d + hh * dk:3 * d + (hh + 1) * dk]
            o_ref[0, r0:r0 + rows, sl] = _gated_group_norm(o_h, gng, g_pre)

    @pl.when(c == pl.num_programs(1) - 1)
    def _():
        for hh in range(heads):
            s_ref[0, hh] = st_scr[hh].T


def _level_table(rows):
    t = np.arange(rows)[:, None]
    s = np.arange(rows)[None, :]
    x = t ^ s
    lev = np.where(t > s, np.floor(np.log2(np.maximum(x, 1))).astype(np.int32), np.where(t == s, -1, -2))
    return jnp.asarray(lev, dtype=jnp.int32)


def _sum_table(rows):
    t = np.arange(rows)[:, None]
    j = np.arange(rows)[None, :]
    blocks = []
    for p in range(1, int(math.log2(SUBLANES))):
        m = 1 << p
        ref = (t // (2 * m)) * (2 * m) + m - 1
        up = ((t >> p) & 1) == 1
        blocks.append(np.where(up, (j > ref) & (j <= t), (j > t) & (j <= ref)))
    blocks.append(j <= t)
    table = np.concatenate(blocks, axis=0).astype(np.float32)
    return jnp.asarray(np.concatenate([table, table], axis=1), dtype=BF16)


def _hgrn_prompt(layer, x, mods, n1g, w_in, lb_raw, gn_g, heads, chunk, step_rows):
    b, L, d = x.shape
    dk = d // heads
    sums = _sum_table(chunk)
    return pl.pallas_call(
        functools.partial(_hgrn_prompt_kernel, layer, heads),
        grid=(b, L // step_rows),
        in_specs=[pl.BlockSpec((1, step_rows, d), lambda i, c: (i, c, 0)),
                  _mod_spec(mods, layer),
                  _full(n1g.shape), _layer_spec(w_in, layer), _full(lb_raw.shape), _full(gn_g.shape),
                  _full((chunk, chunk)), _full(sums.shape)],
        out_specs=[pl.BlockSpec((1, step_rows, d), lambda i, c: (i, c, 0)),
                   pl.BlockSpec((1, heads, dk, dk), lambda i, c: (i, 0, 0, 0))],
        out_shape=[jax.ShapeDtypeStruct((b, L, d), F32),
                   jax.ShapeDtypeStruct((b, heads, dk, dk), F32)],
        scratch_shapes=[pltpu.VMEM((heads, dk, dk), F32), pltpu.VMEM((sums.shape[0], d), F32),
                        pltpu.VMEM((heads, int(math.log2(chunk)) + 2, chunk, dk), BF16),
                        pltpu.VMEM((heads, chunk, chunk), BF16)],
        compiler_params=_params("arbitrary", "arbitrary"),
        name=f"hgrn_prompt_{layer}",
    )(x, mods, n1g, w_in, lb_raw, gn_g, _level_table(chunk), sums)


def _post_kernel(layer, x_ref, a_ref, mod_ref, n2g_ref, wo_ref, wup_ref, wdn_ref, o_ref):
    d = x_ref.shape[2]
    mod = _mod_rows(mod_ref)
    g1, sh2, sc2, g2 = (mod[:, j * d:(j + 1) * d] for j in range(2, 6))
    x1 = x_ref[0] + g1 * _dot(a_ref[0].astype(BF16), wo_ref[0])
    h2 = _modulated_norm(x1, n2g_ref[layer:layer + 1, :], sh2, sc2).astype(BF16)
    y = None
    for c in range(0, wup_ref.shape[2], d):
        u = jnp.square(jnp.maximum(_dot(h2, wup_ref[0, :, c:c + d]), 0.0)).astype(BF16)
        part = _dot(u, wdn_ref[0, c:c + d, :])
        y = part if y is None else y + part
    o_ref[0] = x1 + g2 * y


def _post(layer, x, a, mods, n2g, w_o, o_layer, w_up, w_dn, tm):
    nb, L, d = x.shape
    return pl.pallas_call(
        functools.partial(_post_kernel, layer),
        grid=(nb, L // tm),
        in_specs=[pl.BlockSpec((1, tm, d), lambda i, j: (i, j, 0)),
                  pl.BlockSpec((1, tm, d), lambda i, j: (i, j, 0)),
                  _mod_spec(mods, layer), _full(n2g.shape),
                  _layer_spec(w_o, o_layer), _layer_spec(w_up, layer), _layer_spec(w_dn, layer)],
        out_specs=pl.BlockSpec((1, tm, d), lambda i, j: (i, j, 0)),
        out_shape=jax.ShapeDtypeStruct((nb, L, d), F32),
        compiler_params=_params("arbitrary", "arbitrary"),
        name="post_mlp",
    )(x, a, mods, n2g, w_o, w_up, w_dn)


def _kv_kernel(head_dim, window, x_ref, mod_ref, g_ref, w_ref, kg_ref, cos_ref, sin_ref,
               k_ref, v_ref, kx_ref, vx_ref):
    tm, d = x_ref.shape[1], x_ref.shape[2]
    mod = _mod_rows(mod_ref)
    hk = _modulated_norm(x_ref[0], g_ref[...], mod[:, 0:d], mod[:, d:2 * d])
    y = _dot(hk.astype(BF16), w_ref[...])
    n = y.shape[1] // 2
    cos, sin, kg = cos_ref[...], sin_ref[...], kg_ref[...]
    low = lax.broadcasted_iota(jnp.int32, (tm, LANES), 1) < head_dim
    k_blocks, v_blocks = [], []
    for j in range(n // LANES):
        kb = _head_norm_rope(y[:, j * LANES:(j + 1) * LANES], kg, cos, sin, head_dim, 1.0)
        vb = y[:, n + j * LANES:n + (j + 1) * LANES]
        k_blocks.append(kb)
        v_blocks.append(vb)
        kx_ref[0, :, j * LANES:(j + 1) * LANES] = kb.astype(BF16)
        kx_ref[0, :, n + j * LANES:n + (j + 1) * LANES] = pltpu.roll(kb, head_dim, 1).astype(BF16)
        vs = pltpu.roll(vb, head_dim, 1)
        variants = (jnp.where(low, vb, 0.0), jnp.where(low, 0.0, vs),
                    jnp.where(low, vs, 0.0), jnp.where(low, 0.0, vb))
        for i, var in enumerate(variants):
            off = (4 * j + i) * LANES
            vx_ref[0, off:off + LANES, :] = var.T.astype(BF16)

    @pl.when(pl.program_id(1) == pl.num_programs(1) - 1)
    def _():
        for j in range(n // LANES):
            k_ref[0, :, j * LANES:(j + 1) * LANES] = k_blocks[j][tm - window:, :]
            v_ref[0, :, j * LANES:(j + 1) * LANES] = v_blocks[j][tm - window:, :]


def _kv(x, mods, g, w_kv, kg, cos, sin, head_dim, window, tm):
    nb, L, d = x.shape
    n = w_kv.shape[1] // 2
    rope_rows = tm if cos.shape[0] > 1 else 1
    rope_map = (lambda i, j: (j, 0)) if cos.shape[0] > 1 else (lambda i, j: (0, 0))
    last = lambda i, j: (i, 0, 0)
    tile = lambda i, j: (i, j, 0)
    return pl.pallas_call(
        functools.partial(_kv_kernel, head_dim, window),
        grid=(nb, L // tm),
        in_specs=[pl.BlockSpec((1, tm, d), tile), _mod_spec(mods, 0),
                  _full((1, d)), _full(w_kv.shape), _full((1, LANES)),
                  pl.BlockSpec((rope_rows, LANES), rope_map),
                  pl.BlockSpec((rope_rows, LANES), rope_map)],
        out_specs=[pl.BlockSpec((1, window, n), last), pl.BlockSpec((1, window, n), last),
                   pl.BlockSpec((1, tm, 2 * n), tile), pl.BlockSpec((1, 4 * n, tm), lambda i, j: (i, 0, j))],
        out_shape=[jax.ShapeDtypeStruct((nb, window, n), F32), jax.ShapeDtypeStruct((nb, window, n), F32),
                   jax.ShapeDtypeStruct((nb, L, 2 * n), BF16), jax.ShapeDtypeStruct((nb, 4 * n, L), BF16)],
        compiler_params=_params("arbitrary", "arbitrary"),
        name="kv_proj",
    )(x, mods, g, w_kv, kg, cos, sin)


def _q_proj(x, mod, n1g, wq, qg, cos, sin, head_dim):
    d = x.shape[1]
    h = _modulated_norm(x, n1g, mod[:, 0:d], mod[:, d:2 * d])
    y = _dot(h.astype(BF16), wq)
    scale = 1.0 / math.sqrt(head_dim)
    return [_head_norm_rope(y[:, j * LANES:(j + 1) * LANES], qg, cos, sin, head_dim, scale)
            for j in range(y.shape[1] // LANES)]


def _attn_prompt_kernel(layer, att_layer, head_dim, kv_heads, x_ref, mod_ref, n1g_ref, wq_ref, qg_ref,
                        cos_ref, sin_ref, kp_ref, kc_ref, vp_ref, vc_ref, sink_ref, o_ref, s_scr, p_scr):
    w = x_ref.shape[1]
    nkv = kv_heads * head_dim
    q_blocks = _q_proj(x_ref[0], _mod_rows(mod_ref), n1g_ref[layer:layer + 1, :], wq_ref[0],
                       qg_ref[att_layer:att_layer + 1, :], cos_ref[...], sin_ref[...], head_dim)
    group = (2 * len(q_blocks)) // kv_heads
    key = lax.broadcasted_iota(jnp.int32, (w, 2 * w), 0)
    qry = lax.broadcasted_iota(jnp.int32, (w, 2 * w), 1)
    own = key <= (qry & (w - 1))
    first = lax.broadcasted_iota(jnp.int32, (1, 2 * w), 1) < w
    low = lax.broadcasted_iota(jnp.int32, (w, LANES), 1) < head_dim
    no_prev = jnp.where(pl.program_id(1) == 0, -jnp.inf, 0.0)
    pairs = [(kh, half) for kh in range(kv_heads) for half in range(2)]
    for n, (kh, half) in enumerate(pairs):
        ha, hb = kh * group + half, kh * group + 2 + half
        mask = (lambda t: jnp.where(low, t, 0.0)) if half == 0 else (lambda t: jnp.where(low, 0.0, t))
        qcat = jnp.concatenate([mask(q_blocks[ha // 2]), mask(q_blocks[hb // 2])], axis=0).astype(BF16)
        koff = (kh // 2) * LANES + (0 if kh % 2 == half else nkv)
        kcat = jnp.concatenate([kp_ref[0, :, koff:koff + LANES], kc_ref[0, :, koff:koff + LANES]], axis=0)
        s_t = _dot_nt(kcat, qcat)
        s_scr[n] = jnp.where(own, s_t[w:], s_t[:w] + no_prev)
    for n, (kh, half) in enumerate(pairs):
        sc = s_scr[n]
        sink = jnp.where(first, sink_ref[att_layer, kh * group + half],
                         sink_ref[att_layer, kh * group + 2 + half])
        m = jnp.maximum(jnp.max(sc, axis=0, keepdims=True), sink)
        p = jnp.exp(sc - m)
        pn = p * (1.0 / (jnp.sum(p, axis=0, keepdims=True) + jnp.exp(sink - m)))
        p_scr[n, 0:w] = jnp.where(own, pn, 0.0).astype(BF16)
        p_scr[n, w:2 * w] = jnp.where(own, 0.0, pn).astype(BF16)
    for kh in range(kv_heads):
        out_t = None
        for half in range(2):
            voff = (2 * kh + half) * LANES
            v_t = jnp.concatenate([vc_ref[0, voff:voff + LANES, :], vp_ref[0, voff:voff + LANES, :]], axis=1)
            part = _dot(v_t, p_scr[2 * kh + half])
            out_t = part if out_t is None else out_t + part
        for i in range(2):
            o_ref[0, :, (2 * kh + i) * LANES:(2 * kh + i + 1) * LANES] = out_t[:, i * w:(i + 1) * w].T


def _attn_prompt(layer, att_layer, x, mods, n1g, wq, qg, cos, sin, kx, vx, sinks, head_dim, kv_heads, w):
    b, L, d = x.shape
    prev = lambda i, j: (i, jnp.maximum(j - 1, 0), 0)
    cur = lambda i, j: (i, j, 0)
    return pl.pallas_call(
        functools.partial(_attn_prompt_kernel, layer, att_layer, head_dim, kv_heads),
        grid=(b, L // w),
        in_specs=[pl.BlockSpec((1, w, d), cur), _mod_spec(mods, layer),
                  _full(n1g.shape), _layer_spec(wq, att_layer), _full(qg.shape),
                  pl.BlockSpec((w, LANES), lambda i, j: (j, 0)),
                  pl.BlockSpec((w, LANES), lambda i, j: (j, 0)),
                  pl.BlockSpec((1, w, kx.shape[2]), prev), pl.BlockSpec((1, w, kx.shape[2]), cur),
                  pl.BlockSpec((1, vx.shape[1], w), lambda i, j: (i, 0, jnp.maximum(j - 1, 0))),
                  pl.BlockSpec((1, vx.shape[1], w), lambda i, j: (i, 0, j)),
                  pl.BlockSpec(memory_space=pltpu.SMEM)],
        out_specs=pl.BlockSpec((1, w, d), cur),
        out_shape=jax.ShapeDtypeStruct((b, L, d), F32),
        scratch_shapes=[pltpu.VMEM((2 * kv_heads, w, 2 * w), F32),
                        pltpu.VMEM((2 * kv_heads, 2 * w, 2 * w), BF16)],
        compiler_params=_params("arbitrary", "arbitrary"),
        name="attn_prompt",
    )(x, mods, n1g, wq, qg, cos, sin, kx, kx, vx, vx, sinks)


def _proj_kernel(layer, x_ref, mod_ref, g_ref, w_ref, o_ref):
    d = x_ref.shape[1]
    mod = _mod_rows(mod_ref)
    h = _modulated_norm(x_ref[...], g_ref[layer:layer + 1, :], mod[:, 0:d], mod[:, d:2 * d])
    o_ref[...] = _dot(h.astype(BF16), w_ref[0])


def _proj(layer, x, mods, g, w, tn):
    r, d = x.shape
    n = w.shape[2]
    return pl.pallas_call(
        functools.partial(_proj_kernel, layer),
        grid=(n // tn,),
        in_specs=[_full((r, d)), _mod_spec(mods, layer), _full(g.shape),
                  pl.BlockSpec((1, d, tn), lambda j: (layer, 0, j))],
        out_specs=pl.BlockSpec((r, tn), lambda j: (0, j)),
        out_shape=jax.ShapeDtypeStruct((r, n), F32),
        compiler_params=_params("arbitrary"),
        name="proj_sample",
    )(x, mods, g, w)


def _q_sample_kernel(layer, att_layer, head_dim, x_ref, mod_ref, n1g_ref, wq_ref, qg_ref, cos_ref, sin_ref,
                     o_ref):
    blocks = _q_proj(x_ref[...], _mod_rows(mod_ref), n1g_ref[layer:layer + 1, :], wq_ref[0],
                     qg_ref[att_layer:att_layer + 1, :], cos_ref[...], sin_ref[...], head_dim)
    for j, blk in enumerate(blocks):
        o_ref[:, j * LANES:(j + 1) * LANES] = blk


def _q_sample(layer, att_layer, x, mods, n1g, wq, qg, cos, sin, head_dim):
    r, d = x.shape
    return pl.pallas_call(
        functools.partial(_q_sample_kernel, layer, att_layer, head_dim),
        grid=(1,),
        in_specs=[_full((r, d)), _mod_spec(mods, layer), _full(n1g.shape), _layer_spec(wq, att_layer),
                  _full(qg.shape), _full((1, LANES)), _full((1, LANES))],
        out_specs=_full((r, wq.shape[2])),
        out_shape=jax.ShapeDtypeStruct((r, wq.shape[2]), F32),
        compiler_params=_params("arbitrary"),
        name="q_sample",
    )(x, mods, n1g, wq, qg, cos, sin)


def _hgrn_sample_kernel(layer, heads, tb, steps, aliased, proj_ref, lbraw_ref, gng_ref, s_ref, *rest):
    o_ref, so_ref, stack_scr, acc_scr = rest[1:] if aliased else rest
    i = pl.program_id(0)
    rows = proj_ref.shape[0]
    d = proj_ref.shape[1] // 4
    dk = d // heads

    def update():
        @pl.when(i == 0)
        def _():
            lb = _lower_bound(lbraw_ref[...], layer)
            q, fg = _hgrn_gates(proj_ref[:, 0:2 * d], lb, d)
            for hh in range(heads):
                sl = slice(hh * dk, (hh + 1) * dk)
                ft = fg[:, sl].T
                hi = ft.astype(BF16)
                stack_scr[hh, 0:dk] = hi
                stack_scr[hh, dk:2 * dk] = (ft - hi.astype(F32)).astype(BF16)
                stack_scr[hh, 2 * dk:3 * dk] = (1.0 - ft).astype(BF16)
                stack_scr[hh, 3 * dk:4 * dk] = q[:, sl].T.astype(BF16)

        token_row = lax.broadcasted_iota(jnp.int32, (rows, dk), 0)
        sub = lax.broadcasted_iota(jnp.int32, (tb, dk), 0)
        base = pl.multiple_of(i * tb, tb)
        v_rows = proj_ref[pl.ds(base, tb), 2 * d:3 * d]
        o_rows = [jnp.zeros((tb, dk), F32) for _ in range(heads)]
        for t in range(tb):
            onehot = jnp.where(token_row == base + t, 1.0, 0.0).astype(BF16)
            for hh in range(heads):
                bc = _dot(stack_scr[hh], onehot)
                f_b = bc[0:dk] + bc[dk:2 * dk]
                k_b = bc[2 * dk:3 * dk]
                q_b = bc[3 * dk:4 * dk]
                s_new = f_b * s_ref[0, t, hh] + k_b * v_rows[t:t + 1, hh * dk:(hh + 1) * dk]
                so_ref[0, t, hh] = s_new
                o_rows[hh] = jnp.where(sub == t, jnp.sum(q_b * s_new, axis=0, keepdims=True), o_rows[hh])
        for hh in range(heads):
            acc_scr[pl.ds(base, tb), hh * dk:(hh + 1) * dk] = o_rows[hh]

        @pl.when(i == steps - 1)
        def _():
            gng = gng_ref[layer:layer + 1, :]
            for hh in range(heads):
                sl = slice(hh * dk, (hh + 1) * dk)
                o_ref[:, sl] = _gated_group_norm(acc_scr[:, sl], gng,
                                                 proj_ref[:, 3 * d + hh * dk:3 * d + (hh + 1) * dk])

    if aliased:
        update()
    else:
        pl.when(i < steps)(update)

        @pl.when(i >= steps)
        def _():
            so_ref[...] = jnp.zeros_like(so_ref)


def _hgrn_sample(layer, proj, lb_raw, gn_g, state, new_state, tb):
    n_layers, nb, heads, dk, dv = state.shape
    d = proj.shape[1] // 4
    steps = nb // tb
    aliased = new_state is not None
    assert aliased == (layer > 0) and tb == SUBLANES
    if aliased:
        grid = (steps,)
        s_in = s_out = pl.BlockSpec((1, tb, heads, dk, dv), lambda i: (layer, i, 0, 0, 0))
    else:
        grid = (n_layers * steps,)
        s_in = pl.BlockSpec((1, tb, heads, dk, dv), lambda i: (0, jnp.minimum(i, steps - 1), 0, 0, 0))
        s_out = pl.BlockSpec((1, tb, heads, dk, dv), lambda i: (i // steps, i % steps, 0, 0, 0))
    in_specs = [_full(proj.shape), _full(lb_raw.shape), _full(gn_g.shape), s_in]
    args = [proj, lb_raw, gn_g, state]
    if aliased:
        in_specs.append(pl.BlockSpec(memory_space=pl.ANY))
        args.append(new_state)
    return pl.pallas_call(
        functools.partial(_hgrn_sample_kernel, layer, heads, tb, steps, aliased),
        grid=grid,
        in_specs=in_specs,
        out_specs=[_full((nb, d)), s_out],
        out_shape=[jax.ShapeDtypeStruct((nb, d), F32), jax.ShapeDtypeStruct(state.shape, F32)],
        scratch_shapes=[pltpu.VMEM((heads, 4 * dk, nb), BF16), pltpu.VMEM((nb, d), F32)],
        input_output_aliases={4: 1} if aliased else {},
        compiler_params=_params("arbitrary"),
        name=f"hgrn_sample_{layer}",
    )(*args)


def _cache_roll_kernel(ck_ref, cv_ref, kn_ref, vn_ref, ko_ref, vo_ref):
    w = ck_ref.shape[1]
    ko_ref[:, 0:w - 1, :] = ck_ref[:, 1:w, :]
    ko_ref[:, w - 1:w, :] = kn_ref[...]
    vo_ref[:, 0:w - 1, :] = cv_ref[:, 1:w, :]
    vo_ref[:, w - 1:w, :] = vn_ref[...]


def _cache_roll(ck, cv, kn, vn, tb):
    nb, w, n = ck.shape
    blk = pl.BlockSpec((tb, w, n), lambda i: (i, 0, 0))
    new = pl.BlockSpec((tb, 1, n), lambda i: (i, 0, 0))
    return pl.pallas_call(
        _cache_roll_kernel,
        grid=(nb // tb,),
        in_specs=[blk, blk, new, new],
        out_specs=[blk, blk],
        out_shape=[jax.ShapeDtypeStruct(ck.shape, F32), jax.ShapeDtypeStruct(cv.shape, F32)],
        compiler_params=_params("arbitrary"),
        name="cache_roll",
    )(ck, cv, kn.reshape(nb, 1, n), vn.reshape(nb, 1, n))


def _attn_sample_kernel(att_layer, tb, q_ref, k_ref, v_ref, sink_ref, o_ref):
    heads, head_dim = q_ref.shape[1], q_ref.shape[2]
    nkv = k_ref.shape[2]
    group = heads // (nkv // head_dim)
    r = lax.broadcasted_iota(jnp.int32, (heads, nkv), 0) // group
    c = lax.broadcasted_iota(jnp.int32, (heads, nkv), 1) // head_dim
    own = r == c
    sink = sink_ref[:, att_layer:att_layer + 1]

    for t in range(tb):
        qb = q_ref[t]
        qe = jnp.where(own, jnp.concatenate([qb] * (nkv // head_dim), axis=-1), 0.0).astype(BF16)
        s = _dot_nt(qe, k_ref[t].astype(BF16))
        m = jnp.maximum(jnp.max(s, axis=-1, keepdims=True), sink)
        p = jnp.exp(s - m)
        denom = jnp.sum(p, axis=-1, keepdims=True) + jnp.exp(sink - m)
        pv = jnp.where(own, _dot(p.astype(BF16), v_ref[t].astype(BF16)), 0.0)
        o = pv[:, 0:head_dim]
        for j in range(1, nkv // head_dim):
            o = o + pv[:, j * head_dim:(j + 1) * head_dim]
        o_ref[t] = o / denom


def _attn_sample(att_layer, q3, k, v, sinks_t, tb):
    nb, heads, head_dim = q3.shape
    w, nkv = k.shape[1], k.shape[2]
    return pl.pallas_call(
        functools.partial(_attn_sample_kernel, att_layer, tb),
        grid=(nb // tb,),
        in_specs=[pl.BlockSpec((tb, heads, head_dim), lambda i: (i, 0, 0)),
                  pl.BlockSpec((tb, w, nkv), lambda i: (i, 0, 0)),
                  pl.BlockSpec((tb, w, nkv), lambda i: (i, 0, 0)),
                  _full(sinks_t.shape)],
        out_specs=pl.BlockSpec((tb, heads, head_dim), lambda i: (i, 0, 0)),
        out_shape=jax.ShapeDtypeStruct(q3.shape, F32),
        compiler_params=_params("arbitrary"),
        name="attn_sample",
    )(q3, k, v, sinks_t)


def _rope_tables(pos, head_dim):
    half = head_dim // 2
    inv = ROPE_THETA ** (-jnp.arange(half, dtype=F32) / half)
    ang = pos.astype(F32)[:, None] * inv[None, :]
    cos, sin = jnp.cos(ang), jnp.sin(ang)
    reps = LANES // head_dim
    return (jnp.tile(jnp.concatenate([cos, cos], axis=1), (1, reps)),
            jnp.tile(jnp.concatenate([-sin, sin], axis=1), (1, reps)))


def kernel(x_prompt, x_sample, c_prompt, c_sample, state_hgrn, cache_k, cache_v, w_ada, b_ada, norm1_g, norm2_g, hg_w_in, hg_w_out, hg_lower_bounds, hg_gn_g, kv_w_ada, kv_b_ada, kv_norm_g, w_kv, k_norm_g, w_q, q_norm_g, sinks, w_o, w_up, w_down):
    bp, Lp, d = x_prompt.shape
    bs = x_sample.shape[0]
    depth = w_ada.shape[0]
    n_a = hg_w_in.shape[0]
    heads = state_hgrn.shape[2]
    window, kv_heads, head_dim = cache_k.shape[1], cache_k.shape[2], cache_k.shape[3]
    assert LANES == 2 * head_dim and kv_heads % 2 == 0 and Lp % window == 0
    assert window & (window - 1) == 0 and (d // head_dim) // kv_heads == 4
    nkv = kv_heads * head_dim
    reps = LANES // head_dim
    bf = lambda t: t.astype(BF16)

    pad = (-(bs + bp)) % 16
    c_all = jnp.concatenate([c_sample, c_prompt, jnp.zeros((pad, d), F32)], axis=0)
    mods_s, mods_p = _ada(c_all, bs, bp, w_ada, b_ada, 1536)
    kv_mods_s, kv_mods_p = _ada(c_all, bs, bp, kv_w_ada[None], kv_b_ada[None], 1024)

    cos_p, sin_p = _rope_tables(jnp.arange(Lp), head_dim)
    cos_s, sin_s = _rope_tables(jnp.full((1,), PAST_LEN), head_dim)
    kg = jnp.tile(k_norm_g, reps)[None]
    qg = jnp.tile(q_norm_g, (1, reps))
    sinks_t = sinks.T
    w_kv_b, w_in_b, w_out_b, w_q_b, w_o_b = bf(w_kv), bf(hg_w_in), bf(hg_w_out), bf(w_q), bf(w_o)
    w_up_b, w_dn_b = bf(w_up), bf(w_down)

    xp = x_prompt
    xs = x_sample.reshape(bs, d)
    hg_p, hg_s = [], None
    k_p = v_p = kx_p = vx_p = k_s = v_s = None
    for l in range(depth):
        if l == n_a:
            k_p, v_p, kx_p, vx_p = _kv(xp, kv_mods_p, kv_norm_g[None], w_kv_b, kg, cos_p, sin_p,
                                       head_dim, window, 512)
            k_n, v_n, _, _ = _kv(xs[None], kv_mods_s, kv_norm_g[None], w_kv_b, kg, cos_s, sin_s,
                                 head_dim, bs, bs)
            k_s, v_s = _cache_roll(cache_k.reshape(bs, window, nkv), cache_v.reshape(bs, window, nkv),
                                   k_n[0], v_n[0], 8)
        if l < n_a:
            w_mix_b, mix_layer = w_out_b, l
            a_p, s_p = _hgrn_prompt(l, xp, mods_p, norm1_g, w_in_b, hg_lower_bounds, hg_gn_g, heads, 128, 256)
            hg_p.append(s_p)
            proj_s = _proj(l, xs, mods_s, norm1_g, w_in_b, 1024)
            a_s, hg_s = _hgrn_sample(l, proj_s, hg_lower_bounds, hg_gn_g, state_hgrn, hg_s, SUBLANES)
        else:
            j = l - n_a
            w_mix_b, mix_layer = w_o_b, j
            a_p = _attn_prompt(l, j, xp, mods_p, norm1_g, w_q_b, qg, cos_p, sin_p, kx_p, vx_p, sinks,
                               head_dim, kv_heads, window)
            q_s = _q_sample(l, j, xs, mods_s, norm1_g, w_q_b, qg, cos_s, sin_s, head_dim)
            a_s = _attn_sample(j, q_s.reshape(bs, d // head_dim, head_dim), k_s, v_s, sinks_t, 8)
            a_s = a_s.reshape(bs, d)
        xp = _post(l, xp, a_p, mods_p, norm2_g, w_mix_b, mix_layer, w_up_b, w_dn_b, 512)
        xs = _post(l, xs[None], a_s[None], mods_s, norm2_g, w_mix_b, mix_layer, w_up_b, w_dn_b, bs)[0]

    shape4 = lambda t: t.reshape(t.shape[0], window, kv_heads, head_dim)
    return (xp, xs.reshape(bs, 1, d), jnp.stack(hg_p), shape4(k_p), shape4(v_p), hg_s,
            shape4(k_s), shape4(v_s))
```

```python
import functools
import math

import numpy as np
import jax
import jax.numpy as jnp
from jax import lax
from jax.experimental import pallas as pl
from jax.experimental.pallas import tpu as pltpu

F32 = jnp.float32
BF16 = jnp.bfloat16

PAST_LEN = 8192
ROPE_THETA = 10000.0
EPS = 1e-6
LOG2E = 1.4426950408889634
LANES = 128
SUBLANES = 8
VMEM_LIMIT = 56 * 1024 * 1024

NT_DIMS = (((1,), (1,)), ((), ()))
TN_DIMS = (((0,), (0,)), ((), ()))


def _dot(a, b):
    return jnp.dot(a, b, preferred_element_type=F32)


def _dot_nt(a, b):
    return lax.dot_general(a, b, NT_DIMS, preferred_element_type=F32)


def _sigmoid(x):
    return 1.0 / (1.0 + jnp.exp(-x))


def _rms(x, g):
    ms = jnp.mean(x * x, axis=-1, keepdims=True)
    return x * lax.rsqrt(ms + EPS) * g


def _modulated_norm(x, gain, shift, scale):
    return _rms(x, gain) * (1.0 + scale) + shift


def _params(*sem):
    return pltpu.CompilerParams(dimension_semantics=sem, vmem_limit_bytes=VMEM_LIMIT)


def _full(shape):
    n = len(shape)
    return pl.BlockSpec(shape, lambda *_: (0,) * n)


def _ada_kernel(c_ref, w_ref, b_ref, os_ref, op_ref):
    bs, bp = os_ref.shape[1], op_ref.shape[1]
    c = c_ref[...]
    a = (c * _sigmoid(c)).astype(BF16)
    res = _dot(a, w_ref[0].astype(BF16)) + b_ref[0]
    os_ref[0] = res[0:bs]
    for r in range(bp):
        op_ref[0, r] = jnp.broadcast_to(res[bs + r:bs + r + 1], op_ref.shape[2:])


def _ada(c_all, bs, bp, w, b, tn):
    nl, d, n = w.shape
    r = c_all.shape[0]
    return pl.pallas_call(
        _ada_kernel,
        grid=(nl, n // tn),
        in_specs=[pl.BlockSpec((r, d), lambda l, j: (0, 0)),
                  pl.BlockSpec((1, d, tn), lambda l, j: (l, 0, j)),
                  pl.BlockSpec((1, 1, tn), lambda l, j: (l, 0, j))],
        out_specs=[pl.BlockSpec((1, bs, tn), lambda l, j: (l, 0, j)),
                   pl.BlockSpec((1, bp, SUBLANES, tn), lambda l, j: (l, 0, 0, j))],
        out_shape=[jax.ShapeDtypeStruct((nl, bs, n), F32),
                   jax.ShapeDtypeStruct((nl, bp, SUBLANES, n), F32)],
        compiler_params=_params("arbitrary", "arbitrary"),
        name="ada",
    )(c_all, w, b.reshape(nl, 1, n))


def _mod_rows(mod_ref):
    return mod_ref[0, 0, 0:1, :] if len(mod_ref.shape) == 4 else mod_ref[0]


def _mod_spec(mods, layer):
    if mods.ndim == 4:
        return pl.BlockSpec((1, 1) + mods.shape[2:], lambda i, *_: (layer, i, 0, 0))
    return pl.BlockSpec((1,) + mods.shape[1:], lambda *_: (layer, 0, 0))


def _layer_spec(stack, layer):
    zeros = (0,) * (stack.ndim - 1)
    return pl.BlockSpec((1,) + stack.shape[1:], lambda *_: (layer,) + zeros, pipeline_mode=pl.Buffered(1))


def _group_mean_matrix(group):
    r = lax.broadcasted_iota(jnp.int32, (LANES, LANES), 0) // group
    c = lax.broadcasted_iota(jnp.int32, (LANES, LANES), 1) // group
    return jnp.where(r == c, 1.0 / group, 0.0).astype(BF16)


def _head_norm_rope(y, gain, cos, sin, head_dim, out_scale):
    ms = _dot((y * y).astype(BF16), _group_mean_matrix(head_dim))
    yn = y * lax.rsqrt(ms + EPS) * gain
    half = head_dim // 2
    lane = lax.broadcasted_iota(jnp.int32, yn.shape, 1)
    first = (lane % head_dim) < half
    rot = jnp.where(first, pltpu.roll(yn, LANES - half, 1), pltpu.roll(yn, half, 1))
    out = yn * cos + rot * sin
    if out_scale != 1.0:
        out = out * out_scale
    return out


def _lower_bound(raw, layer):
    m = jnp.max(raw, axis=0, keepdims=True)
    e = jnp.exp(raw - m)
    sm = e / jnp.sum(e, axis=0, keepdims=True)
    acc = sm[0:1]
    for j in range(1, layer + 1):
        acc = acc + sm[j:j + 1]
    return acc - sm[0:1]


def _hgrn_gates(proj, lb, d):
    qa = proj[:, 0:d]
    q = qa * _sigmoid(qa)
    fg = lb + (1.0 - lb) * _sigmoid(proj[:, d:2 * d])
    return q, fg


def _gated_group_norm(o, gain, gate_pre):
    ms = jnp.mean(o * o, axis=-1, keepdims=True)
    return o * lax.rsqrt(ms + EPS) * gain * (gate_pre * _sigmoid(gate_pre))


def _hgrn_prompt_kernel(layer, heads, x_ref, mod_ref, n1g_ref, win_ref, lbraw_ref, gng_ref,
                        lev_ref, sums_ref, o_ref, s_ref, st_scr, d_scr, xs_scr, att_scr):
    c = pl.program_id(1)
    total, d = x_ref.shape[1], x_ref.shape[2]
    rows = lev_ref.shape[0]
    dk = d // heads
    n_levels = int(math.log2(rows))

    @pl.when(c == 0)
    def _():
        st_scr[...] = jnp.zeros_like(st_scr)

    mod = _mod_rows(mod_ref)
    h = _modulated_norm(x_ref[0], n1g_ref[layer:layer + 1, :], mod[:, 0:d], mod[:, d:2 * d])
    proj_all = _dot(h.astype(BF16), win_ref[0])
    lb = _lower_bound(lbraw_ref[...], layer)
    q_all, fg_all = _hgrn_gates(proj_all, lb, d)
    lf2_all = jnp.log(fg_all) * LOG2E
    n_table = d_scr.shape[0] // rows - 1
    lev = lev_ref[...]
    row_id = lax.broadcasted_iota(jnp.int32, (rows, dk), 0)
    upper = [((row_id >> p) & 1) == 1 for p in range(n_table + 1)]
    col = lax.broadcasted_iota(jnp.int32, (1, rows), 1)
    gng = gng_ref[layer:layer + 1, :]

    for r0 in range(0, total, rows):
        proj, q, fg = (t[r0:r0 + rows] for t in (proj_all, q_all, fg_all))
        k = 1.0 - fg
        lf2 = lf2_all[r0:r0 + rows]
        hi = lf2.astype(BF16)
        lo = (lf2 - hi.astype(F32)).astype(BF16)
        d_scr[...] = _dot(sums_ref[...], jnp.concatenate([hi, lo], axis=0))

        for hh in range(heads):
            sl = slice(hh * dk, (hh + 1) * dk)
            q_h, k_h = q[:, sl], k[:, sl]
            cum = d_scr[n_table * rows:(n_table + 1) * rows, sl]
            xs_scr[hh, 0] = jnp.where(upper[0], q_h * fg[:, sl], k_h).astype(BF16)
            for p in range(1, n_levels):
                m = 1 << p
                if p <= n_table:
                    e = jnp.exp2(d_scr[(p - 1) * rows:p * rows, sl])
                    xs_scr[hh, p] = (jnp.where(upper[p], q_h, k_h) * e).astype(BF16)
                    continue
                pieces = []
                for r in range(0, rows, m):
                    ref = (r // (2 * m)) * 2 * m + m - 1
                    if (r // m) % 2:
                        pieces.append(q_h[r:r + m] * jnp.exp2(cum[r:r + m] - cum[ref:ref + 1]))
                    else:
                        pieces.append(k_h[r:r + m] * jnp.exp2(cum[ref:ref + 1] - cum[r:r + m]))
                xs_scr[hh, p] = jnp.concatenate(pieces, axis=0).astype(BF16)
            cum_last = cum[rows - 1:rows, :]
            xs_scr[hh, n_levels] = (q_h * jnp.exp2(cum)).astype(BF16)
            xs_scr[hh, n_levels + 1] = (k_h * jnp.exp2(cum_last - cum)).astype(BF16)

        for hh in range(heads):
            xs = xs_scr[hh, 0]
            att = jnp.where(lev == 0, _dot_nt(xs, xs), 0.0)
            for p in range(1, n_levels):
                m = 1 << p
                xs = xs_scr[hh, p]
                pp = _dot_nt(xs, xs)
                if p <= n_table:
                    att = jnp.where(lev == p, pp, att)
                else:
                    att = jnp.concatenate(
                        [jnp.where((col >= r - m) & (col < r), pp[r:r + m], att[r:r + m]) if (r // m) % 2
                         else att[r:r + m] for r in range(0, rows, m)], axis=0)
            att_scr[hh] = att.astype(BF16)

        for hh in range(heads):
            sl = slice(hh * dk, (hh + 1) * dk)
            v_f = proj[:, 2 * d + hh * dk:2 * d + (hh + 1) * dk]
            v_h = v_f.astype(BF16)
            cum_last = d_scr[(n_table + 1) * rows - 1:(n_table + 1) * rows, sl]
            st = st_scr[hh]
            o_h = (_dot(jnp.concatenate([att_scr[hh], xs_scr[hh, n_levels]], axis=1),
                        jnp.concatenate([v_h, st.T.astype(BF16)], axis=0))
                   + jnp.sum(q[:, sl] * k[:, sl], axis=-1, keepdims=True) * v_f)
            st_scr[hh] = st * jnp.exp2(cum_last) + lax.dot_general(
                v_h, xs_scr[hh, n_levels + 1], TN_DIMS, preferred_element_type=F32)
            g_pre = proj[:, 3 * d + hh * dk:3 * d + (hh + 1) * dk]
            o_ref[0, r0:r0 + rows, sl] = _gated_group_norm(o_h, gng, g_pre)

    @pl.when(c == pl.num_programs(1) - 1)
    def _():
        for hh in range(heads):
            s_ref[0, hh] = st_scr[hh].T


def _level_table(rows):
    t = np.arange(rows)[:, None]
    s = np.arange(rows)[None, :]
    x = t ^ s
    lev = np.where(t > s, np.floor(np.log2(np.maximum(x, 1))).astype(np.int32), np.where(t == s, -1, -2))
    return jnp.asarray(lev, dtype=jnp.int32)


def _sum_table(rows):
    t = np.arange(rows)[:, None]
    j = np.arange(rows)[None, :]
    blocks = []
    for p in range(1, int(math.log2(SUBLANES))):
        m = 1 << p
        ref = (t // (2 * m)) * (2 * m) + m - 1
        up = ((t >> p) & 1) == 1
        blocks.append(np.where(up, (j > ref) & (j <= t), (j > t) & (j <= ref)))
    blocks.append(j <= t)
    table = np.concatenate(blocks, axis=0).astype(np.float32)
    return jnp.asarray(np.concatenate([table, table], axis=1), dtype=BF16)


def _hgrn_prompt(layer, x, mods, n1g, w_in, lb_raw, gn_g, heads, chunk, step_rows):
    b, L, d = x.shape
    dk = d // heads
    sums = _sum_table(chunk)
    return pl.pallas_call(
        functools.partial(_hgrn_prompt_kernel, layer, heads),
        grid=(b, L // step_rows),
        in_specs=[pl.BlockSpec((1, step_rows, d), lambda i, c: (i, c, 0)),
                  _mod_spec(mods, layer),
                  _full(n1g.shape), _layer_spec(w_in, layer), _full(lb_raw.shape), _full(gn_g.shape),
                  _full((chunk, chunk)), _full(sums.shape)],
        out_specs=[pl.BlockSpec((1, step_rows, d), lambda i, c: (i, c, 0)),
                   pl.BlockSpec((1, heads, dk, dk), lambda i, c: (i, 0, 0, 0))],
        out_shape=[jax.ShapeDtypeStruct((b, L, d), F32),
                   jax.ShapeDtypeStruct((b, heads, dk, dk), F32)],
        scratch_shapes=[pltpu.VMEM((heads, dk, dk), F32), pltpu.VMEM((sums.shape[0], d), F32),
                        pltpu.VMEM((heads, int(math.log2(chunk)) + 2, chunk, dk), BF16),
                        pltpu.VMEM((heads, chunk, chunk), BF16)],
        compiler_params=_params("arbitrary", "arbitrary"),
        name=f"hgrn_prompt_{layer}",
    )(x, mods, n1g, w_in, lb_raw, gn_g, _level_table(chunk), sums)


def _post_kernel(layer, x_ref, a_ref, mod_ref, n2g_ref, wo_ref, wup_ref, wdn_ref, o_ref):
    d = x_ref.shape[2]
    mod = _mod_rows(mod_ref)
    g1, sh2, sc2, g2 = (mod[:, j * d:(j + 1) * d] for j in range(2, 6))
    x1 = x_ref[0] + g1 * _dot(a_ref[0].astype(BF16), wo_ref[0])
    h2 = _modulated_norm(x1, n2g_ref[layer:layer + 1, :], sh2, sc2).astype(BF16)
    y = None
    for c in range(0, wup_ref.shape[2], d):
        u = jnp.square(jnp.maximum(_dot(h2, wup_ref[0, :, c:c + d]), 0.0)).astype(BF16)
        part = _dot(u, wdn_ref[0, c:c + d, :])
        y = part if y is None else y + part
    o_ref[0] = x1 + g2 * y


def _post(layer, x, a, mods, n2g, w_o, o_layer, w_up, w_dn, tm):
    nb, L, d = x.shape
    return pl.pallas_call(
        functools.partial(_post_kernel, layer),
        grid=(nb, L // tm),
        in_specs=[pl.BlockSpec((1, tm, d), lambda i, j: (i, j, 0)),
                  pl.BlockSpec((1, tm, d), lambda i, j: (i, j, 0)),
                  _mod_spec(mods, layer), _full(n2g.shape),
                  _layer_spec(w_o, o_layer), _layer_spec(w_up, layer), _layer_spec(w_dn, layer)],
        out_specs=pl.BlockSpec((1, tm, d), lambda i, j: (i, j, 0)),
        out_shape=jax.ShapeDtypeStruct((nb, L, d), F32),
        compiler_params=_params("arbitrary", "arbitrary"),
        name="post_mlp",
    )(x, a, mods, n2g, w_o, w_up, w_dn)


def _kv_kernel(head_dim, window, x_ref, mod_ref, g_ref, w_ref, kg_ref, cos_ref, sin_ref,
               k_ref, v_ref, kx_ref, vx_ref):
    tm, d = x_ref.shape[1], x_ref.shape[2]
    mod = _mod_rows(mod_ref)
    hk = _modulated_norm(x_ref[0], g_ref[...], mod[:, 0:d], mod[:, d:2 * d])
    y = _dot(hk.astype(BF16), w_ref[...])
    n = y.shape[1] // 2
    cos, sin, kg = cos_ref[...], sin_ref[...], kg_ref[...]
    low = lax.broadcasted_iota(jnp.int32, (tm, LANES), 1) < head_dim
    k_blocks, v_blocks = [], []
    for j in range(n // LANES):
        kb = _head_norm_rope(y[:, j * LANES:(j + 1) * LANES], kg, cos, sin, head_dim, 1.0)
        vb = y[:, n + j * LANES:n + (j + 1) * LANES]
        k_blocks.append(kb)
        v_blocks.append(vb)
        kx_ref[0, :, j * LANES:(j + 1) * LANES] = kb.astype(BF16)
        kx_ref[0, :, n + j * LANES:n + (j + 1) * LANES] = pltpu.roll(kb, head_dim, 1).astype(BF16)
        vs = pltpu.roll(vb, head_dim, 1)
        variants = (jnp.where(low, vb, 0.0), jnp.where(low, 0.0, vs),
                    jnp.where(low, vs, 0.0), jnp.where(low, 0.0, vb))
        for i, var in enumerate(variants):
            off = (4 * j + i) * LANES
            vx_ref[0, off:off + LANES, :] = var.T.astype(BF16)

    @pl.when(pl.program_id(1) == pl.num_programs(1) - 1)
    def _():
        for j in range(n // LANES):
            k_ref[0, :, j * LANES:(j + 1) * LANES] = k_blocks[j][tm - window:, :]
            v_ref[0, :, j * LANES:(j + 1) * LANES] = v_blocks[j][tm - window:, :]


def _kv(x, mods, g, w_kv, kg, cos, sin, head_dim, window, tm):
    nb, L, d = x.shape
    n = w_kv.shape[1] // 2
    rope_rows = tm if cos.shape[0] > 1 else 1
    rope_map = (lambda i, j: (j, 0)) if cos.shape[0] > 1 else (lambda i, j: (0, 0))
    last = lambda i, j: (i, 0, 0)
    tile = lambda i, j: (i, j, 0)
    return pl.pallas_call(
        functools.partial(_kv_kernel, head_dim, window),
        grid=(nb, L // tm),
        in_specs=[pl.BlockSpec((1, tm, d), tile), _mod_spec(mods, 0),
                  _full((1, d)), _full(w_kv.shape), _full((1, LANES)),
                  pl.BlockSpec((rope_rows, LANES), rope_map),
                  pl.BlockSpec((rope_rows, LANES), rope_map)],
        out_specs=[pl.BlockSpec((1, window, n), last), pl.BlockSpec((1, window, n), last),
                   pl.BlockSpec((1, tm, 2 * n), tile), pl.BlockSpec((1, 4 * n, tm), lambda i, j: (i, 0, j))],
        out_shape=[jax.ShapeDtypeStruct((nb, window, n), F32), jax.ShapeDtypeStruct((nb, window, n), F32),
                   jax.ShapeDtypeStruct((nb, L, 2 * n), BF16), jax.ShapeDtypeStruct((nb, 4 * n, L), BF16)],
        compiler_params=_params("arbitrary", "arbitrary"),
        name="kv_proj",
    )(x, mods, g, w_kv, kg, cos, sin)


def _q_proj(x, mod, n1g, wq, qg, cos, sin, head_dim, extra_scale=1.0):
    d = x.shape[1]
    h = _modulated_norm(x, n1g, mod[:, 0:d], mod[:, d:2 * d])
    y = _dot(h.astype(BF16), wq)
    scale = extra_scale / math.sqrt(head_dim)
    return [_head_norm_rope(y[:, j * LANES:(j + 1) * LANES], qg, cos, sin, head_dim, scale)
            for j in range(y.shape[1] // LANES)]


def _attn_prompt_kernel(layer, att_layer, head_dim, kv_heads, nq, x_ref, mod_ref, n1g_ref, wq_ref, qg_ref,
                        cos_ref, sin_ref, *rest):
    k_refs, v_refs = rest[0:nq + 1], rest[nq + 1:2 * nq + 2]
    sink_ref, o_ref, s_scr, p_scr = rest[2 * nq + 2:]
    w = x_ref.shape[1] // nq
    nkv = kv_heads * head_dim
    q_blocks = _q_proj(x_ref[0], _mod_rows(mod_ref), n1g_ref[layer:layer + 1, :], wq_ref[0],
                       qg_ref[att_layer:att_layer + 1, :], cos_ref[...], sin_ref[...], head_dim, LOG2E)
    group = (2 * len(q_blocks)) // kv_heads
    key = lax.broadcasted_iota(jnp.int32, (w, 2 * w), 0)
    qry = lax.broadcasted_iota(jnp.int32, (w, 2 * w), 1)
    own = key <= (qry & (w - 1))
    first = lax.broadcasted_iota(jnp.int32, (1, 2 * w), 1) < w
    low = lax.broadcasted_iota(jnp.int32, (w, LANES), 1) < head_dim
    no_prev = jnp.where(pl.program_id(1) == 0, -jnp.inf, 0.0)
    tiles = [(u, kh, half) for u in range(nq) for kh in range(kv_heads) for half in range(2)]
    for n, (u, kh, half) in enumerate(tiles):
        ha, hb = kh * group + half, kh * group + 2 + half
        mask = (lambda t: jnp.where(low, t, 0.0)) if half == 0 else (lambda t: jnp.where(low, 0.0, t))
        rows = slice(u * w, (u + 1) * w)
        qcat = jnp.concatenate([mask(q_blocks[ha // 2][rows]), mask(q_blocks[hb // 2][rows])],
                               axis=0).astype(BF16)
        koff = (kh // 2) * LANES + (0 if kh % 2 == half else nkv)
        kcat = jnp.concatenate([k_refs[u][0, :, koff:koff + LANES], k_refs[u + 1][0, :, koff:koff + LANES]],
                               axis=0)
        s_t = _dot_nt(kcat, qcat)
        s_prev = s_t[:w] + no_prev if u == 0 else s_t[:w]
        s_scr[n] = jnp.where(own, s_t[w:], s_prev)
    for n, (u, kh, half) in enumerate(tiles):
        sc = s_scr[n]
        sink = jnp.where(first, sink_ref[att_layer, kh * group + half],
                         sink_ref[att_layer, kh * group + 2 + half]) * LOG2E
        m = jnp.maximum(jnp.max(sc, axis=0, keepdims=True), sink)
        p = jnp.exp2(sc - m)
        pn = p * (1.0 / (jnp.sum(p, axis=0, keepdims=True) + jnp.exp2(sink - m)))
        p_scr[n, 0:w] = jnp.where(own, pn, 0.0).astype(BF16)
        p_scr[n, w:2 * w] = jnp.where(own, 0.0, pn).astype(BF16)
    for u in range(nq):
        for kh in range(kv_heads):
            out_t = None
            for half in range(2):
                voff = (2 * kh + half) * LANES
                v_t = jnp.concatenate([v_refs[u + 1][0, voff:voff + LANES, :],
                                       v_refs[u][0, voff:voff + LANES, :]], axis=1)
                part = _dot(v_t, p_scr[(u * kv_heads + kh) * 2 + half])
                out_t = part if out_t is None else out_t + part
            for i in range(2):
                o_ref[0, u * w:(u + 1) * w, (2 * kh + i) * LANES:(2 * kh + i + 1) * LANES] = (
                    out_t[:, i * w:(i + 1) * w].T)


def _attn_prompt(layer, att_layer, x, mods, n1g, wq, qg, cos, sin, kx, vx, sinks, head_dim, kv_heads, w, nq):
    b, L, d = x.shape
    rows = lambda i, j: (i, j, 0)
    k_specs = [pl.BlockSpec((1, w, kx.shape[2]), lambda i, j, o=o: (i, jnp.maximum(nq * j + o, 0), 0))
               for o in range(-1, nq)]
    v_specs = [pl.BlockSpec((1, vx.shape[1], w), lambda i, j, o=o: (i, 0, jnp.maximum(nq * j + o, 0)))
               for o in range(-1, nq)]
    n_tiles = nq * 2 * kv_heads
    return pl.pallas_call(
        functools.partial(_attn_prompt_kernel, layer, att_layer, head_dim, kv_heads, nq),
        grid=(b, L // (nq * w)),
        in_specs=[pl.BlockSpec((1, nq * w, d), rows), _mod_spec(mods, layer),
                  _full(n1g.shape), _layer_spec(wq, att_layer), _full(qg.shape),
                  pl.BlockSpec((nq * w, LANES), lambda i, j: (j, 0)),
                  pl.BlockSpec((nq * w, LANES), lambda i, j: (j, 0))]
                 + k_specs + v_specs + [pl.BlockSpec(memory_space=pltpu.SMEM)],
        out_specs=pl.BlockSpec((1, nq * w, d), rows),
        out_shape=jax.ShapeDtypeStruct((b, L, d), F32),
        scratch_shapes=[pltpu.VMEM((n_tiles, w, 2 * w), F32), pltpu.VMEM((n_tiles, 2 * w, 2 * w), BF16)],
        compiler_params=_params("arbitrary", "arbitrary"),
        name="attn_prompt",
    )(x, mods, n1g, wq, qg, cos, sin, *([kx] * (nq + 1)), *([vx] * (nq + 1)), sinks)


def _proj_kernel(layer, x_ref, mod_ref, g_ref, w_ref, o_ref):
    d = x_ref.shape[1]
    mod = _mod_rows(mod_ref)
    h = _modulated_norm(x_ref[...], g_ref[layer:layer + 1, :], mod[:, 0:d], mod[:, d:2 * d])
    o_ref[...] = _dot(h.astype(BF16), w_ref[0])


def _proj(layer, x, mods, g, w, tn):
    r, d = x.shape
    n = w.shape[2]
    return pl.pallas_call(
        functools.partial(_proj_kernel, layer),
        grid=(n // tn,),
        in_specs=[_full((r, d)), _mod_spec(mods, layer), _full(g.shape),
                  pl.BlockSpec((1, d, tn), lambda j: (layer, 0, j))],
        out_specs=pl.BlockSpec((r, tn), lambda j: (0, j)),
        out_shape=jax.ShapeDtypeStruct((r, n), F32),
        compiler_params=_params("arbitrary"),
        name="proj_sample",
    )(x, mods, g, w)


def _q_sample_kernel(layer, att_layer, head_dim, x_ref, mod_ref, n1g_ref, wq_ref, qg_ref, cos_ref, sin_ref,
                     o_ref):
    blocks = _q_proj(x_ref[...], _mod_rows(mod_ref), n1g_ref[layer:layer + 1, :], wq_ref[0],
                     qg_ref[att_layer:att_layer + 1, :], cos_ref[...], sin_ref[...], head_dim)
    for j, blk in enumerate(blocks):
        o_ref[:, j * LANES:(j + 1) * LANES] = blk


def _q_sample(layer, att_layer, x, mods, n1g, wq, qg, cos, sin, head_dim):
    r, d = x.shape
    return pl.pallas_call(
        functools.partial(_q_sample_kernel, layer, att_layer, head_dim),
        grid=(1,),
        in_specs=[_full((r, d)), _mod_spec(mods, layer), _full(n1g.shape), _layer_spec(wq, att_layer),
                  _full(qg.shape), _full((1, LANES)), _full((1, LANES))],
        out_specs=_full((r, wq.shape[2])),
        out_shape=jax.ShapeDtypeStruct((r, wq.shape[2]), F32),
        compiler_params=_params("arbitrary"),
        name="q_sample",
    )(x, mods, n1g, wq, qg, cos, sin)


def _hgrn_sample_kernel(layer, heads, tb, steps, aliased, proj_ref, lbraw_ref, gng_ref, s_ref, *rest):
    o_ref, so_ref, stack_scr, acc_scr = rest[1:] if aliased else rest
    i = pl.program_id(0)
    rows = proj_ref.shape[0]
    d = proj_ref.shape[1] // 4
    dk = d // heads

    def update():
        @pl.when(i == 0)
        def _():
            lb = _lower_bound(lbraw_ref[...], layer)
            q, fg = _hgrn_gates(proj_ref[:, 0:2 * d], lb, d)
            for hh in range(heads):
                sl = slice(hh * dk, (hh + 1) * dk)
                ft = fg[:, sl].T
                hi = ft.astype(BF16)
                stack_scr[hh, 0:dk] = hi
                stack_scr[hh, dk:2 * dk] = (ft - hi.astype(F32)).astype(BF16)
                stack_scr[hh, 2 * dk:3 * dk] = (1.0 - ft).astype(BF16)
                stack_scr[hh, 3 * dk:4 * dk] = q[:, sl].T.astype(BF16)

        token_row = lax.broadcasted_iota(jnp.int32, (rows, dk), 0)
        sub = lax.broadcasted_iota(jnp.int32, (tb, dk), 0)
        base = pl.multiple_of(i * tb, tb)
        v_rows = proj_ref[pl.ds(base, tb), 2 * d:3 * d]
        o_rows = [jnp.zeros((tb, dk), F32) for _ in range(heads)]
        for t in range(tb):
            onehot = jnp.where(token_row == base + t, 1.0, 0.0).astype(BF16)
            for hh in range(heads):
                bc = _dot(stack_scr[hh], onehot)
                f_b = bc[0:dk] + bc[dk:2 * dk]
                k_b = bc[2 * dk:3 * dk]
                q_b = bc[3 * dk:4 * dk]
                s_new = f_b * s_ref[0, t, hh] + k_b * v_rows[t:t + 1, hh * dk:(hh + 1) * dk]
                so_ref[0, t, hh] = s_new
                o_rows[hh] = jnp.where(sub == t, jnp.sum(q_b * s_new, axis=0, keepdims=True), o_rows[hh])
        for hh in range(heads):
            acc_scr[pl.ds(base, tb), hh * dk:(hh + 1) * dk] = o_rows[hh]

        @pl.when(i == steps - 1)
        def _():
            gng = gng_ref[layer:layer + 1, :]
            for hh in range(heads):
                sl = slice(hh * dk, (hh + 1) * dk)
                o_ref[:, sl] = _gated_group_norm(acc_scr[:, sl], gng,
                                                 proj_ref[:, 3 * d + hh * dk:3 * d + (hh + 1) * dk])

    if aliased:
        update()
    else:
        pl.when(i < steps)(update)

        @pl.when(i >= steps)
        def _():
            so_ref[...] = jnp.zeros_like(so_ref)


def _hgrn_sample(layer, proj, lb_raw, gn_g, state, new_state, tb):
    n_layers, nb, heads, dk, dv = state.shape
    d = proj.shape[1] // 4
    steps = nb // tb
    aliased = new_state is not None
    assert aliased == (layer > 0) and tb == SUBLANES
    if aliased:
        grid = (steps,)
        s_in = s_out = pl.BlockSpec((1, tb, heads, dk, dv), lambda i: (layer, i, 0, 0, 0))
    else:
        grid = (n_layers * steps,)
        s_in = pl.BlockSpec((1, tb, heads, dk, dv), lambda i: (0, jnp.minimum(i, steps - 1), 0, 0, 0))
        s_out = pl.BlockSpec((1, tb, heads, dk, dv), lambda i: (i // steps, i % steps, 0, 0, 0))
    in_specs = [_full(proj.shape), _full(lb_raw.shape), _full(gn_g.shape), s_in]
    args = [proj, lb_raw, gn_g, state]
    if aliased:
        in_specs.append(pl.BlockSpec(memory_space=pl.ANY))
        args.append(new_state)
    return pl.pallas_call(
        functools.partial(_hgrn_sample_kernel, layer, heads, tb, steps, aliased),
        grid=grid,
        in_specs=in_specs,
        out_specs=[_full((nb, d)), s_out],
        out_shape=[jax.ShapeDtypeStruct((nb, d), F32), jax.ShapeDtypeStruct(state.shape, F32)],
        scratch_shapes=[pltpu.VMEM((heads, 4 * dk, nb), BF16), pltpu.VMEM((nb, d), F32)],
        input_output_aliases={4: 1} if aliased else {},
        compiler_params=_params("arbitrary"),
        name=f"hgrn_sample_{layer}",
    )(*args)


def _cache_roll_kernel(ck_ref, cv_ref, kn_ref, vn_ref, ko_ref, vo_ref):
    w = ck_ref.shape[1]
    ko_ref[:, 0:w - 1, :] = ck_ref[:, 1:w, :]
    ko_ref[:, w - 1:w, :] = kn_ref[...]
    vo_ref[:, 0:w - 1, :] = cv_ref[:, 1:w, :]
    vo_ref[:, w - 1:w, :] = vn_ref[...]


def _cache_roll(ck, cv, kn, vn, tb):
    nb, w, n = ck.shape
    blk = pl.BlockSpec((tb, w, n), lambda i: (i, 0, 0))
    new = pl.BlockSpec((tb, 1, n), lambda i: (i, 0, 0))
    return pl.pallas_call(
        _cache_roll_kernel,
        grid=(nb // tb,),
        in_specs=[blk, blk, new, new],
        out_specs=[blk, blk],
        out_shape=[jax.ShapeDtypeStruct(ck.shape, F32), jax.ShapeDtypeStruct(cv.shape, F32)],
        compiler_params=_params("arbitrary"),
        name="cache_roll",
    )(ck, cv, kn.reshape(nb, 1, n), vn.reshape(nb, 1, n))


def _attn_sample_kernel(att_layer, tb, q_ref, k_ref, v_ref, sink_ref, o_ref):
    heads, head_dim = q_ref.shape[1], q_ref.shape[2]
    nkv = k_ref.shape[2]
    group = heads // (nkv // head_dim)
    r = lax.broadcasted_iota(jnp.int32, (heads, nkv), 0) // group
    c = lax.broadcasted_iota(jnp.int32, (heads, nkv), 1) // head_dim
    own = r == c
    sink = sink_ref[:, att_layer:att_layer + 1]

    for t in range(tb):
        qb = q_ref[t]
        qe = jnp.where(own, jnp.concatenate([qb] * (nkv // head_dim), axis=-1), 0.0).astype(BF16)
        s = _dot_nt(qe, k_ref[t].astype(BF16))
        m = jnp.maximum(jnp.max(s, axis=-1, keepdims=True), sink)
        p = jnp.exp(s - m)
        denom = jnp.sum(p, axis=-1, keepdims=True) + jnp.exp(sink - m)
        pv = jnp.where(own, _dot(p.astype(BF16), v_ref[t].astype(BF16)), 0.0)
        o = pv[:, 0:head_dim]
        for j in range(1, nkv // head_dim):
            o = o + pv[:, j * head_dim:(j + 1) * head_dim]
        o_ref[t] = o / denom


def _attn_sample(att_layer, q3, k, v, sinks_t, tb):
    nb, heads, head_dim = q3.shape
    w, nkv = k.shape[1], k.shape[2]
    return pl.pallas_call(
        functools.partial(_attn_sample_kernel, att_layer, tb),
        grid=(nb // tb,),
        in_specs=[pl.BlockSpec((tb, heads, head_dim), lambda i: (i, 0, 0)),
                  pl.BlockSpec((tb, w, nkv), lambda i: (i, 0, 0)),
                  pl.BlockSpec((tb, w, nkv), lambda i: (i, 0, 0)),
                  _full(sinks_t.shape)],
        out_specs=pl.BlockSpec((tb, heads, head_dim), lambda i: (i, 0, 0)),
        out_shape=jax.ShapeDtypeStruct(q3.shape, F32),
        compiler_params=_params("arbitrary"),
        name="attn_sample",
    )(q3, k, v, sinks_t)


def _rope_tables(pos, head_dim):
    half = head_dim // 2
    inv = ROPE_THETA ** (-jnp.arange(half, dtype=F32) / half)
    ang = pos.astype(F32)[:, None] * inv[None, :]
    cos, sin = jnp.cos(ang), jnp.sin(ang)
    reps = LANES // head_dim
    return (jnp.tile(jnp.concatenate([cos, cos], axis=1), (1, reps)),
            jnp.tile(jnp.concatenate([-sin, sin], axis=1), (1, reps)))


def kernel(x_prompt, x_sample, c_prompt, c_sample, state_hgrn, cache_k, cache_v, w_ada, b_ada, norm1_g, norm2_g, hg_w_in, hg_w_out, hg_lower_bounds, hg_gn_g, kv_w_ada, kv_b_ada, kv_norm_g, w_kv, k_norm_g, w_q, q_norm_g, sinks, w_o, w_up, w_down):
    bp, Lp, d = x_prompt.shape
    bs = x_sample.shape[0]
    depth = w_ada.shape[0]
    n_a = hg_w_in.shape[0]
    heads = state_hgrn.shape[2]
    window, kv_heads, head_dim = cache_k.shape[1], cache_k.shape[2], cache_k.shape[3]
    assert LANES == 2 * head_dim and kv_heads % 2 == 0 and Lp % window == 0
    assert window & (window - 1) == 0 and (d // head_dim) // kv_heads == 4
    nkv = kv_heads * head_dim
    reps = LANES // head_dim
    bf = lambda t: t.astype(BF16)

    pad = (-(bs + bp)) % 16
    c_all = jnp.concatenate([c_sample, c_prompt, jnp.zeros((pad, d), F32)], axis=0)
    mods_s, mods_p = _ada(c_all, bs, bp, w_ada, b_ada, 1536)
    kv_mods_s, kv_mods_p = _ada(c_all, bs, bp, kv_w_ada[None], kv_b_ada[None], 1024)

    cos_p, sin_p = _rope_tables(jnp.arange(Lp), head_dim)
    cos_s, sin_s = _rope_tables(jnp.full((1,), PAST_LEN), head_dim)
    kg = jnp.tile(k_norm_g, reps)[None]
    qg = jnp.tile(q_norm_g, (1, reps))
    sinks_t = sinks.T
    w_kv_b, w_in_b, w_out_b, w_q_b, w_o_b = bf(w_kv), bf(hg_w_in), bf(hg_w_out), bf(w_q), bf(w_o)
    w_up_b, w_dn_b = bf(w_up), bf(w_down)

    xp = x_prompt
    xs = x_sample.reshape(bs, d)
    hg_p, hg_s = [], None
    k_p = v_p = kx_p = vx_p = k_s = v_s = None
    for l in range(depth):
        if l == n_a:
            k_p, v_p, kx_p, vx_p = _kv(xp, kv_mods_p, kv_norm_g[None], w_kv_b, kg, cos_p, sin_p,
                                       head_dim, window, 512)
            k_n, v_n, _, _ = _kv(xs[None], kv_mods_s, kv_norm_g[None], w_kv_b, kg, cos_s, sin_s,
                                 head_dim, bs, bs)
            k_s, v_s = _cache_roll(cache_k.reshape(bs, window, nkv), cache_v.reshape(bs, window, nkv),
                                   k_n[0], v_n[0], 8)
        if l < n_a:
            w_mix_b, mix_layer = w_out_b, l
            a_p, s_p = _hgrn_prompt(l, xp, mods_p, norm1_g, w_in_b, hg_lower_bounds, hg_gn_g, heads, 128, 512)
            hg_p.append(s_p)
            proj_s = _proj(l, xs, mods_s, norm1_g, w_in_b, 1024)
            a_s, hg_s = _hgrn_sample(l, proj_s, hg_lower_bounds, hg_gn_g, state_hgrn, hg_s, SUBLANES)
        else:
            j = l - n_a
            w_mix_b, mix_layer = w_o_b, j
            a_p = _attn_prompt(l, j, xp, mods_p, norm1_g, w_q_b, qg, cos_p, sin_p, kx_p, vx_p, sinks,
                               head_dim, kv_heads, window, 4)
            q_s = _q_sample(l, j, xs, mods_s, norm1_g, w_q_b, qg, cos_s, sin_s, head_dim)
            a_s = _attn_sample(j, q_s.reshape(bs, d // head_dim, head_dim), k_s, v_s, sinks_t, 8)
            a_s = a_s.reshape(bs, d)
        xp = _post(l, xp, a_p, mods_p, norm2_g, w_mix_b, mix_layer, w_up_b, w_dn_b, 512)
        xs = _post(l, xs[None], a_s[None], mods_s, norm2_g, w_mix_b, mix_layer, w_up_b, w_dn_b, bs)[0]

    shape4 = lambda t: t.reshape(t.shape[0], window, kv_heads, head_dim)
    return (xp, xs.reshape(bs, 1, d), jnp.stack(hg_p), shape4(k_p), shape4(v_p), hg_s,
            shape4(k_s), shape4(v_s))
```

```python
import functools
import math
from typing import NamedTuple

import numpy as np
import jax
import jax.numpy as jnp
from jax import lax
from jax.experimental import pallas as pl
from jax.experimental.pallas import tpu as pltpu

F32 = jnp.float32
BF16 = jnp.bfloat16

PAST_LEN = 8192
ROPE_THETA = 10000.0
EPS = 1e-6
LOG2E = 1.4426950408889634
LANES = 128
SUBLANES = 8
VMEM_LIMIT = 56 * 1024 * 1024


class _Tiles(NamedTuple):
    ada_cols: int = 1536
    kv_ada_cols: int = 1024
    hgrn_chunk: int = 128
    hgrn_rows: int = 512
    mlp_rows: int = 512
    kv_rows: int = 512
    attn_blocks: int = 4
    proj_cols: int = 1024
    cache_tokens: int = 8
    attn_sample_tokens: int = 16
    mlp_sample_slab: int = 1024


T = _Tiles()

NT_DIMS = (((1,), (1,)), ((), ()))
TN_DIMS = (((0,), (0,)), ((), ()))


def _dot(a, b):
    return jnp.dot(a, b, preferred_element_type=F32)


def _dot_nt(a, b):
    return lax.dot_general(a, b, NT_DIMS, preferred_element_type=F32)


def _sigmoid(x):
    return 1.0 / (1.0 + jnp.exp(-x))


def _rms(x, g):
    ms = jnp.mean(x * x, axis=-1, keepdims=True)
    return x * lax.rsqrt(ms + EPS) * g


def _modulated_norm(x, gain, shift, scale):
    return _rms(x, gain) * (1.0 + scale) + shift


def _params(*sem):
    return pltpu.CompilerParams(dimension_semantics=sem, vmem_limit_bytes=VMEM_LIMIT)


def _full(shape):
    n = len(shape)
    return pl.BlockSpec(shape, lambda *_: (0,) * n)


def _ada_kernel(c_ref, w_ref, b_ref, os_ref, op_ref):
    bs, bp = os_ref.shape[1], op_ref.shape[1]
    c = c_ref[...]
    a = (c * _sigmoid(c)).astype(BF16)
    res = _dot(a, w_ref[0].astype(BF16)) + b_ref[0]
    os_ref[0] = res[0:bs]
    for r in range(bp):
        op_ref[0, r] = jnp.broadcast_to(res[bs + r:bs + r + 1], op_ref.shape[2:])


def _ada(c_all, bs, bp, w, b, tn):
    nl, d, n = w.shape
    r = c_all.shape[0]
    return pl.pallas_call(
        _ada_kernel,
        grid=(nl, n // tn),
        in_specs=[pl.BlockSpec((r, d), lambda l, j: (0, 0)),
                  pl.BlockSpec((1, d, tn), lambda l, j: (l, 0, j)),
                  pl.BlockSpec((1, 1, tn), lambda l, j: (l, 0, j))],
        out_specs=[pl.BlockSpec((1, bs, tn), lambda l, j: (l, 0, j)),
                   pl.BlockSpec((1, bp, SUBLANES, tn), lambda l, j: (l, 0, 0, j))],
        out_shape=[jax.ShapeDtypeStruct((nl, bs, n), F32),
                   jax.ShapeDtypeStruct((nl, bp, SUBLANES, n), F32)],
        compiler_params=_params("arbitrary", "arbitrary"),
        name="ada",
    )(c_all, w, b.reshape(nl, 1, n))


def _mod_rows(mod_ref):
    return mod_ref[0, 0, 0:1, :] if len(mod_ref.shape) == 4 else mod_ref[0]


def _mod_spec(mods, layer):
    if mods.ndim == 4:
        return pl.BlockSpec((1, 1) + mods.shape[2:], lambda i, *_: (layer, i, 0, 0))
    return pl.BlockSpec((1,) + mods.shape[1:], lambda *_: (layer, 0, 0))


def _layer_spec(stack, layer):
    zeros = (0,) * (stack.ndim - 1)
    return pl.BlockSpec((1,) + stack.shape[1:], lambda *_: (layer,) + zeros, pipeline_mode=pl.Buffered(1))


def _group_mean_matrix(group):
    r = lax.broadcasted_iota(jnp.int32, (LANES, LANES), 0) // group
    c = lax.broadcasted_iota(jnp.int32, (LANES, LANES), 1) // group
    return jnp.where(r == c, 1.0 / group, 0.0).astype(BF16)


def _head_norm_rope(y, gain, cos, sin, head_dim, out_scale):
    ms = _dot((y * y).astype(BF16), _group_mean_matrix(head_dim))
    yn = y * lax.rsqrt(ms + EPS) * gain
    half = head_dim // 2
    lane = lax.broadcasted_iota(jnp.int32, yn.shape, 1)
    first = (lane % head_dim) < half
    rot = jnp.where(first, pltpu.roll(yn, LANES - half, 1), pltpu.roll(yn, half, 1))
    out = yn * cos + rot * sin
    if out_scale != 1.0:
        out = out * out_scale
    return out


def _lower_bound(raw, layer):
    m = jnp.max(raw, axis=0, keepdims=True)
    e = jnp.exp(raw - m)
    sm = e / jnp.sum(e, axis=0, keepdims=True)
    acc = sm[0:1]
    for j in range(1, layer + 1):
        acc = acc + sm[j:j + 1]
    return acc - sm[0:1]


def _hgrn_gates(proj, lb, d):
    qa = proj[:, 0:d]
    q = qa * _sigmoid(qa)
    fg = lb + (1.0 - lb) * _sigmoid(proj[:, d:2 * d])
    return q, fg


def _gated_group_norm(o, gain, gate_pre):
    ms = jnp.mean(o * o, axis=-1, keepdims=True)
    return o * lax.rsqrt(ms + EPS) * gain * (gate_pre * _sigmoid(gate_pre))


def _hgrn_prompt_kernel(layer, heads, x_ref, mod_ref, n1g_ref, win_ref, lbraw_ref, gng_ref,
                        lev_ref, sums_ref, o_ref, s_ref, st_scr, d_scr, xs_scr, att_scr):
    c = pl.program_id(1)
    total, d = x_ref.shape[1], x_ref.shape[2]
    rows = lev_ref.shape[0]
    dk = d // heads
    n_levels = int(math.log2(rows))

    @pl.when(c == 0)
    def _():
        st_scr[...] = jnp.zeros_like(st_scr)

    mod = _mod_rows(mod_ref)
    h = _modulated_norm(x_ref[0], n1g_ref[layer:layer + 1, :], mod[:, 0:d], mod[:, d:2 * d])
    proj_all = _dot(h.astype(BF16), win_ref[0])
    lb = _lower_bound(lbraw_ref[...], layer)
    q_all, fg_all = _hgrn_gates(proj_all, lb, d)
    lf2_all = jnp.log(fg_all) * LOG2E
    n_table = d_scr.shape[0] // rows - 1
    lev = lev_ref[...]
    row_id = lax.broadcasted_iota(jnp.int32, (rows, dk), 0)
    upper = [((row_id >> p) & 1) == 1 for p in range(n_table + 1)]
    col = lax.broadcasted_iota(jnp.int32, (1, rows), 1)
    gng = gng_ref[layer:layer + 1, :]

    for r0 in range(0, total, rows):
        proj, q, fg = (t[r0:r0 + rows] for t in (proj_all, q_all, fg_all))
        k = 1.0 - fg
        lf2 = lf2_all[r0:r0 + rows]
        hi = lf2.astype(BF16)
        lo = (lf2 - hi.astype(F32)).astype(BF16)
        d_scr[...] = _dot(sums_ref[...], jnp.concatenate([hi, lo], axis=0))

        for hh in range(heads):
            sl = slice(hh * dk, (hh + 1) * dk)
            q_h, k_h = q[:, sl], k[:, sl]
            cum = d_scr[n_table * rows:(n_table + 1) * rows, sl]
            xs_scr[hh, 0] = jnp.where(upper[0], q_h * fg[:, sl], k_h).astype(BF16)
            for p in range(1, n_levels):
                m = 1 << p
                if p <= n_table:
                    e = jnp.exp2(d_scr[(p - 1) * rows:p * rows, sl])
                    xs_scr[hh, p] = (jnp.where(upper[p], q_h, k_h) * e).astype(BF16)
                    continue
                pieces = []
                for r in range(0, rows, m):
                    ref = (r // (2 * m)) * 2 * m + m - 1
                    if (r // m) % 2:
                        pieces.append(q_h[r:r + m] * jnp.exp2(cum[r:r + m] - cum[ref:ref + 1]))
                    else:
                        pieces.append(k_h[r:r + m] * jnp.exp2(cum[ref:ref + 1] - cum[r:r + m]))
                xs_scr[hh, p] = jnp.concatenate(pieces, axis=0).astype(BF16)
            cum_last = cum[rows - 1:rows, :]
            xs_scr[hh, n_levels] = (q_h * jnp.exp2(cum)).astype(BF16)
            xs_scr[hh, n_levels + 1] = (k_h * jnp.exp2(cum_last - cum)).astype(BF16)

        for hh in range(heads):
            xs = xs_scr[hh, 0]
            att = jnp.where(lev == 0, _dot_nt(xs, xs), 0.0)
            for p in range(1, n_levels):
                m = 1 << p
                xs = xs_scr[hh, p]
                pp = _dot_nt(xs, xs)
                if p <= n_table:
                    att = jnp.where(lev == p, pp, att)
                else:
                    att = jnp.concatenate(
                        [jnp.where((col >= r - m) & (col < r), pp[r:r + m], att[r:r + m]) if (r // m) % 2
                         else att[r:r + m] for r in range(0, rows, m)], axis=0)
            att_scr[hh] = att.astype(BF16)

        for hh in range(heads):
            sl = slice(hh * dk, (hh + 1) * dk)
            v_f = proj[:, 2 * d + hh * dk:2 * d + (hh + 1) * dk]
            v_h = v_f.astype(BF16)
            cum_last = d_scr[(n_table + 1) * rows - 1:(n_table + 1) * rows, sl]
            st = st_scr[hh]
            o_h = (_dot(jnp.concatenate([att_scr[hh], xs_scr[hh, n_levels]], axis=1),
                        jnp.concatenate([v_h, st.T.astype(BF16)], axis=0))
                   + jnp.sum(q[:, sl] * k[:, sl], axis=-1, keepdims=True) * v_f)
            st_scr[hh] = st * jnp.exp2(cum_last) + lax.dot_general(
                v_h, xs_scr[hh, n_levels + 1], TN_DIMS, preferred_element_type=F32)
            g_pre = proj[:, 3 * d + hh * dk:3 * d + (hh + 1) * dk]
            o_ref[0, r0:r0 + rows, sl] = _gated_group_norm(o_h, gng, g_pre)

    @pl.when(c == pl.num_programs(1) - 1)
    def _():
        for hh in range(heads):
            s_ref[0, hh] = st_scr[hh].T


def _level_table(rows):
    t = np.arange(rows)[:, None]
    s = np.arange(rows)[None, :]
    x = t ^ s
    lev = np.where(t > s, np.floor(np.log2(np.maximum(x, 1))).astype(np.int32), np.where(t == s, -1, -2))
    return jnp.asarray(lev, dtype=jnp.int32)


def _sum_table(rows):
    t = np.arange(rows)[:, None]
    j = np.arange(rows)[None, :]
    blocks = []
    for p in range(1, int(math.log2(SUBLANES))):
        m = 1 << p
        ref = (t // (2 * m)) * (2 * m) + m - 1
        up = ((t >> p) & 1) == 1
        blocks.append(np.where(up, (j > ref) & (j <= t), (j > t) & (j <= ref)))
    blocks.append(j <= t)
    table = np.concatenate(blocks, axis=0).astype(np.float32)
    return jnp.asarray(np.concatenate([table, table], axis=1), dtype=BF16)


def _hgrn_prompt(layer, x, mods, n1g, w_in, lb_raw, gn_g, heads, chunk, step_rows):
    b, L, d = x.shape
    dk = d // heads
    sums = _sum_table(chunk)
    return pl.pallas_call(
        functools.partial(_hgrn_prompt_kernel, layer, heads),
        grid=(b, L // step_rows),
        in_specs=[pl.BlockSpec((1, step_rows, d), lambda i, c: (i, c, 0)),
                  _mod_spec(mods, layer),
                  _full(n1g.shape), _layer_spec(w_in, layer), _full(lb_raw.shape), _full(gn_g.shape),
                  _full((chunk, chunk)), _full(sums.shape)],
        out_specs=[pl.BlockSpec((1, step_rows, d), lambda i, c: (i, c, 0)),
                   pl.BlockSpec((1, heads, dk, dk), lambda i, c: (i, 0, 0, 0))],
        out_shape=[jax.ShapeDtypeStruct((b, L, d), F32),
                   jax.ShapeDtypeStruct((b, heads, dk, dk), F32)],
        scratch_shapes=[pltpu.VMEM((heads, dk, dk), F32), pltpu.VMEM((sums.shape[0], d), F32),
                        pltpu.VMEM((heads, int(math.log2(chunk)) + 2, chunk, dk), BF16),
                        pltpu.VMEM((heads, chunk, chunk), BF16)],
        compiler_params=_params("arbitrary", "arbitrary"),
        name=f"hgrn_prompt_{layer}",
    )(x, mods, n1g, w_in, lb_raw, gn_g, _level_table(chunk), sums)


def _post_kernel(layer, x_ref, a_ref, mod_ref, n2g_ref, wo_ref, wup_ref, wdn_ref, o_ref):
    d = x_ref.shape[2]
    mod = _mod_rows(mod_ref)
    g1, sh2, sc2, g2 = (mod[:, j * d:(j + 1) * d] for j in range(2, 6))
    x1 = x_ref[0] + g1 * _dot(a_ref[0].astype(BF16), wo_ref[0])
    h2 = _modulated_norm(x1, n2g_ref[layer:layer + 1, :], sh2, sc2).astype(BF16)
    y = None
    for c in range(0, wup_ref.shape[2], d):
        u = jnp.square(jnp.maximum(_dot(h2, wup_ref[0, :, c:c + d]), 0.0)).astype(BF16)
        part = _dot(u, wdn_ref[0, c:c + d, :])
        y = part if y is None else y + part
    o_ref[0] = x1 + g2 * y


def _post_sample_kernel(layer, x_ref, a_ref, mod_ref, n2g_ref, wo_ref, wup_ref, wdn_ref, o_ref,
                        x1_scr, h2_scr, y_scr):
    c = pl.program_id(0)
    d = x_ref.shape[1]
    mod = _mod_rows(mod_ref)

    @pl.when(c == 0)
    def _():
        g1, sh2, sc2 = (mod[:, j * d:(j + 1) * d] for j in range(2, 5))
        x1 = x_ref[...] + g1 * _dot(a_ref[...].astype(BF16), wo_ref[0])
        x1_scr[...] = x1
        h2_scr[...] = _modulated_norm(x1, n2g_ref[layer:layer + 1, :], sh2, sc2).astype(BF16)
        y_scr[...] = jnp.zeros_like(y_scr)

    u = jnp.square(jnp.maximum(_dot(h2_scr[...], wup_ref[0]), 0.0)).astype(BF16)
    y_scr[...] += _dot(u, wdn_ref[0])

    @pl.when(c == pl.num_programs(0) - 1)
    def _():
        o_ref[...] = x1_scr[...] + mod[:, 5 * d:6 * d] * y_scr[...]


def _post_sample(layer, x, a, mods, n2g, w_o, o_layer, w_up, w_dn, slab):
    r, d = x.shape
    ff = w_up.shape[2]
    return pl.pallas_call(
        functools.partial(_post_sample_kernel, layer),
        grid=(ff // slab,),
        in_specs=[_full((r, d)), _full((r, d)), _mod_spec(mods, layer), _full(n2g.shape),
                  _layer_spec(w_o, o_layer),
                  pl.BlockSpec((1, d, slab), lambda c: (layer, 0, c)),
                  pl.BlockSpec((1, slab, d), lambda c: (layer, c, 0))],
        out_specs=_full((r, d)),
        out_shape=jax.ShapeDtypeStruct((r, d), F32),
        scratch_shapes=[pltpu.VMEM((r, d), F32), pltpu.VMEM((r, d), BF16), pltpu.VMEM((r, d), F32)],
        compiler_params=_params("arbitrary"),
        name="post_mlp_sample",
    )(x, a, mods, n2g, w_o, w_up, w_dn)


def _post(layer, x, a, mods, n2g, w_o, o_layer, w_up, w_dn, tm):
    nb, L, d = x.shape
    return pl.pallas_call(
        functools.partial(_post_kernel, layer),
        grid=(nb, L // tm),
        in_specs=[pl.BlockSpec((1, tm, d), lambda i, j: (i, j, 0)),
                  pl.BlockSpec((1, tm, d), lambda i, j: (i, j, 0)),
                  _mod_spec(mods, layer), _full(n2g.shape),
                  _layer_spec(w_o, o_layer), _layer_spec(w_up, layer), _layer_spec(w_dn, layer)],
        out_specs=pl.BlockSpec((1, tm, d), lambda i, j: (i, j, 0)),
        out_shape=jax.ShapeDtypeStruct((nb, L, d), F32),
        compiler_params=_params("arbitrary", "arbitrary"),
        name="post_mlp",
    )(x, a, mods, n2g, w_o, w_up, w_dn)


def _kv_kernel(head_dim, window, x_ref, mod_ref, g_ref, w_ref, kg_ref, cos_ref, sin_ref,
               k_ref, v_ref, kx_ref, vx_ref):
    tm, d = x_ref.shape[1], x_ref.shape[2]
    mod = _mod_rows(mod_ref)
    hk = _modulated_norm(x_ref[0], g_ref[...], mod[:, 0:d], mod[:, d:2 * d])
    y = _dot(hk.astype(BF16), w_ref[...])
    n = y.shape[1] // 2
    cos, sin, kg = cos_ref[...], sin_ref[...], kg_ref[...]
    low = lax.broadcasted_iota(jnp.int32, (tm, LANES), 1) < head_dim
    k_blocks, v_blocks = [], []
    for j in range(n // LANES):
        kb = _head_norm_rope(y[:, j * LANES:(j + 1) * LANES], kg, cos, sin, head_dim, 1.0)
        vb = y[:, n + j * LANES:n + (j + 1) * LANES]
        k_blocks.append(kb)
        v_blocks.append(vb)
        kx_ref[0, :, j * LANES:(j + 1) * LANES] = kb.astype(BF16)
        kx_ref[0, :, n + j * LANES:n + (j + 1) * LANES] = pltpu.roll(kb, head_dim, 1).astype(BF16)
        vs = pltpu.roll(vb, head_dim, 1)
        variants = (jnp.where(low, vb, 0.0), jnp.where(low, 0.0, vs),
                    jnp.where(low, vs, 0.0), jnp.where(low, 0.0, vb))
        for i, var in enumerate(variants):
            off = (4 * j + i) * LANES
            vx_ref[0, off:off + LANES, :] = var.T.astype(BF16)

    @pl.when(pl.program_id(1) == pl.num_programs(1) - 1)
    def _():
        for j in range(n // LANES):
            k_ref[0, :, j * LANES:(j + 1) * LANES] = k_blocks[j][tm - window:, :]
            v_ref[0, :, j * LANES:(j + 1) * LANES] = v_blocks[j][tm - window:, :]


def _kv(x, mods, g, w_kv, kg, cos, sin, head_dim, window, tm):
    nb, L, d = x.shape
    n = w_kv.shape[1] // 2
    rope_rows = tm if cos.shape[0] > 1 else 1
    rope_map = (lambda i, j: (j, 0)) if cos.shape[0] > 1 else (lambda i, j: (0, 0))
    last = lambda i, j: (i, 0, 0)
    tile = lambda i, j: (i, j, 0)
    return pl.pallas_call(
        functools.partial(_kv_kernel, head_dim, window),
        grid=(nb, L // tm),
        in_specs=[pl.BlockSpec((1, tm, d), tile), _mod_spec(mods, 0),
                  _full((1, d)), _full(w_kv.shape), _full((1, LANES)),
                  pl.BlockSpec((rope_rows, LANES), rope_map),
                  pl.BlockSpec((rope_rows, LANES), rope_map)],
        out_specs=[pl.BlockSpec((1, window, n), last), pl.BlockSpec((1, window, n), last),
                   pl.BlockSpec((1, tm, 2 * n), tile), pl.BlockSpec((1, 4 * n, tm), lambda i, j: (i, 0, j))],
        out_shape=[jax.ShapeDtypeStruct((nb, window, n), F32), jax.ShapeDtypeStruct((nb, window, n), F32),
                   jax.ShapeDtypeStruct((nb, L, 2 * n), BF16), jax.ShapeDtypeStruct((nb, 4 * n, L), BF16)],
        compiler_params=_params("arbitrary", "arbitrary"),
        name="kv_proj",
    )(x, mods, g, w_kv, kg, cos, sin)


def _q_proj(x, mod, n1g, wq, qg, cos, sin, head_dim, extra_scale=1.0):
    d = x.shape[1]
    h = _modulated_norm(x, n1g, mod[:, 0:d], mod[:, d:2 * d])
    y = _dot(h.astype(BF16), wq)
    scale = extra_scale / math.sqrt(head_dim)
    return [_head_norm_rope(y[:, j * LANES:(j + 1) * LANES], qg, cos, sin, head_dim, scale)
            for j in range(y.shape[1] // LANES)]


def _attn_prompt_kernel(layer, att_layer, head_dim, kv_heads, nq, x_ref, mod_ref, n1g_ref, wq_ref, qg_ref,
                        cos_ref, sin_ref, *rest):
    k_refs, v_refs = rest[0:nq + 1], rest[nq + 1:2 * nq + 2]
    sink_ref, o_ref, s_scr, p_scr = rest[2 * nq + 2:]
    w = x_ref.shape[1] // nq
    nkv = kv_heads * head_dim
    q_blocks = _q_proj(x_ref[0], _mod_rows(mod_ref), n1g_ref[layer:layer + 1, :], wq_ref[0],
                       qg_ref[att_layer:att_layer + 1, :], cos_ref[...], sin_ref[...], head_dim, LOG2E)
    group = (2 * len(q_blocks)) // kv_heads
    key = lax.broadcasted_iota(jnp.int32, (w, 2 * w), 0)
    qry = lax.broadcasted_iota(jnp.int32, (w, 2 * w), 1)
    own = key <= (qry & (w - 1))
    first = lax.broadcasted_iota(jnp.int32, (1, 2 * w), 1) < w
    low = lax.broadcasted_iota(jnp.int32, (w, LANES), 1) < head_dim
    no_prev = jnp.where(pl.program_id(1) == 0, -jnp.inf, 0.0)
    tiles = [(u, kh, half) for u in range(nq) for kh in range(kv_heads) for half in range(2)]
    for n, (u, kh, half) in enumerate(tiles):
        ha, hb = kh * group + half, kh * group + 2 + half
        mask = (lambda t: jnp.where(low, t, 0.0)) if half == 0 else (lambda t: jnp.where(low, 0.0, t))
        rows = slice(u * w, (u + 1) * w)
        qcat = jnp.concatenate([mask(q_blocks[ha // 2][rows]), mask(q_blocks[hb // 2][rows])],
                               axis=0).astype(BF16)
        koff = (kh // 2) * LANES + (0 if kh % 2 == half else nkv)
        kcat = jnp.concatenate([k_refs[u][0, :, koff:koff + LANES], k_refs[u + 1][0, :, koff:koff + LANES]],
                               axis=0)
        s_t = _dot_nt(kcat, qcat)
        s_prev = s_t[:w] + no_prev if u == 0 else s_t[:w]
        s_scr[n] = jnp.where(own, s_t[w:], s_prev)
    for n, (u, kh, half) in enumerate(tiles):
        sc = s_scr[n]
        sink = jnp.where(first, sink_ref[att_layer, kh * group + half],
                         sink_ref[att_layer, kh * group + 2 + half]) * LOG2E
        m = jnp.maximum(jnp.max(sc, axis=0, keepdims=True), sink)
        p = jnp.exp2(sc - m)
        pn = p * (1.0 / (jnp.sum(p, axis=0, keepdims=True) + jnp.exp2(sink - m)))
        p_scr[n, 0:w] = jnp.where(own, pn, 0.0).astype(BF16)
        p_scr[n, w:2 * w] = jnp.where(own, 0.0, pn).astype(BF16)
    for u in range(nq):
        for kh in range(kv_heads):
            out_t = None
            for half in range(2):
                voff = (2 * kh + half) * LANES
                v_t = jnp.concatenate([v_refs[u + 1][0, voff:voff + LANES, :],
                                       v_refs[u][0, voff:voff + LANES, :]], axis=1)
                part = _dot(v_t, p_scr[(u * kv_heads + kh) * 2 + half])
                out_t = part if out_t is None else out_t + part
            for i in range(2):
                o_ref[0, u * w:(u + 1) * w, (2 * kh + i) * LANES:(2 * kh + i + 1) * LANES] = (
                    out_t[:, i * w:(i + 1) * w].T)


def _attn_prompt(layer, att_layer, x, mods, n1g, wq, qg, cos, sin, kx, vx, sinks, head_dim, kv_heads, w, nq):
    b, L, d = x.shape
    rows = lambda i, j: (i, j, 0)
    k_specs = [pl.BlockSpec((1, w, kx.shape[2]), lambda i, j, o=o: (i, jnp.maximum(nq * j + o, 0), 0))
               for o in range(-1, nq)]
    v_specs = [pl.BlockSpec((1, vx.shape[1], w), lambda i, j, o=o: (i, 0, jnp.maximum(nq * j + o, 0)))
               for o in range(-1, nq)]
    n_tiles = nq * 2 * kv_heads
    return pl.pallas_call(
        functools.partial(_attn_prompt_kernel, layer, att_layer, head_dim, kv_heads, nq),
        grid=(b, L // (nq * w)),
        in_specs=[pl.BlockSpec((1, nq * w, d), rows), _mod_spec(mods, layer),
                  _full(n1g.shape), _layer_spec(wq, att_layer), _full(qg.shape),
                  pl.BlockSpec((nq * w, LANES), lambda i, j: (j, 0)),
                  pl.BlockSpec((nq * w, LANES), lambda i, j: (j, 0))]
                 + k_specs + v_specs + [pl.BlockSpec(memory_space=pltpu.SMEM)],
        out_specs=pl.BlockSpec((1, nq * w, d), rows),
        out_shape=jax.ShapeDtypeStruct((b, L, d), F32),
        scratch_shapes=[pltpu.VMEM((n_tiles, w, 2 * w), F32), pltpu.VMEM((n_tiles, 2 * w, 2 * w), BF16)],
        compiler_params=_params("arbitrary", "arbitrary"),
        name="attn_prompt",
    )(x, mods, n1g, wq, qg, cos, sin, *([kx] * (nq + 1)), *([vx] * (nq + 1)), sinks)


def _proj_kernel(layer, x_ref, mod_ref, g_ref, w_ref, o_ref):
    d = x_ref.shape[1]
    mod = _mod_rows(mod_ref)
    h = _modulated_norm(x_ref[...], g_ref[layer:layer + 1, :], mod[:, 0:d], mod[:, d:2 * d])
    o_ref[...] = _dot(h.astype(BF16), w_ref[0])


def _proj(layer, x, mods, g, w, tn):
    r, d = x.shape
    n = w.shape[2]
    return pl.pallas_call(
        functools.partial(_proj_kernel, layer),
        grid=(n // tn,),
        in_specs=[_full((r, d)), _mod_spec(mods, layer), _full(g.shape),
                  pl.BlockSpec((1, d, tn), lambda j: (layer, 0, j))],
        out_specs=pl.BlockSpec((r, tn), lambda j: (0, j)),
        out_shape=jax.ShapeDtypeStruct((r, n), F32),
        compiler_params=_params("arbitrary"),
        name="proj_sample",
    )(x, mods, g, w)


def _q_sample_kernel(layer, att_layer, head_dim, x_ref, mod_ref, n1g_ref, wq_ref, qg_ref, cos_ref, sin_ref,
                     o_ref):
    blocks = _q_proj(x_ref[...], _mod_rows(mod_ref), n1g_ref[layer:layer + 1, :], wq_ref[0],
                     qg_ref[att_layer:att_layer + 1, :], cos_ref[...], sin_ref[...], head_dim)
    for j, blk in enumerate(blocks):
        o_ref[:, j * LANES:(j + 1) * LANES] = blk


def _q_sample(layer, att_layer, x, mods, n1g, wq, qg, cos, sin, head_dim):
    r, d = x.shape
    return pl.pallas_call(
        functools.partial(_q_sample_kernel, layer, att_layer, head_dim),
        grid=(1,),
        in_specs=[_full((r, d)), _mod_spec(mods, layer), _full(n1g.shape), _layer_spec(wq, att_layer),
                  _full(qg.shape), _full((1, LANES)), _full((1, LANES))],
        out_specs=_full((r, wq.shape[2])),
        out_shape=jax.ShapeDtypeStruct((r, wq.shape[2]), F32),
        compiler_params=_params("arbitrary"),
        name="q_sample",
    )(x, mods, n1g, wq, qg, cos, sin)


def _hgrn_sample_kernel(layer, heads, tb, steps, aliased, proj_ref, lbraw_ref, gng_ref, s_ref, *rest):
    o_ref, so_ref, stack_scr, acc_scr = rest[1:] if aliased else rest
    i = pl.program_id(0)
    rows = proj_ref.shape[0]
    d = proj_ref.shape[1] // 4
    dk = d // heads

    def update():
        @pl.when(i == 0)
        def _():
            lb = _lower_bound(lbraw_ref[...], layer)
            q, fg = _hgrn_gates(proj_ref[:, 0:2 * d], lb, d)
            for hh in range(heads):
                sl = slice(hh * dk, (hh + 1) * dk)
                ft = fg[:, sl].T
                hi = ft.astype(BF16)
                stack_scr[hh, 0:dk] = hi
                stack_scr[hh, dk:2 * dk] = (ft - hi.astype(F32)).astype(BF16)
                stack_scr[hh, 2 * dk:3 * dk] = (1.0 - ft).astype(BF16)
                stack_scr[hh, 3 * dk:4 * dk] = q[:, sl].T.astype(BF16)

        token_row = lax.broadcasted_iota(jnp.int32, (rows, dk), 0)
        sub = lax.broadcasted_iota(jnp.int32, (tb, dk), 0)
        base = pl.multiple_of(i * tb, tb)
        v_rows = proj_ref[pl.ds(base, tb), 2 * d:3 * d]
        o_rows = [jnp.zeros((tb, dk), F32) for _ in range(heads)]
        for t0 in range(0, tb, 2):
            onehot = jnp.concatenate([jnp.where(token_row == base + t, 1.0, 0.0) for t in (t0, t0 + 1)],
                                     axis=1).astype(BF16)
            for hh in range(heads):
                bc2 = _dot(stack_scr[hh], onehot)
                for t in (t0, t0 + 1):
                    bc = bc2[:, (t - t0) * dk:(t - t0 + 1) * dk]
                    f_b = bc[0:dk] + bc[dk:2 * dk]
                    k_b = bc[2 * dk:3 * dk]
                    q_b = bc[3 * dk:4 * dk]
                    s_new = f_b * s_ref[0, t, hh] + k_b * v_rows[t:t + 1, hh * dk:(hh + 1) * dk]
                    so_ref[0, t, hh] = s_new
                    o_rows[hh] = jnp.where(sub == t, jnp.sum(q_b * s_new, axis=0, keepdims=True), o_rows[hh])
        for hh in range(heads):
            acc_scr[pl.ds(base, tb), hh * dk:(hh + 1) * dk] = o_rows[hh]

        @pl.when(i == steps - 1)
        def _():
            gng = gng_ref[layer:layer + 1, :]
            for hh in range(heads):
                sl = slice(hh * dk, (hh + 1) * dk)
                o_ref[:, sl] = _gated_group_norm(acc_scr[:, sl], gng,
                                                 proj_ref[:, 3 * d + hh * dk:3 * d + (hh + 1) * dk])

    if aliased:
        update()
    else:
        pl.when(i < steps)(update)

        @pl.when(i >= steps)
        def _():
            so_ref[...] = jnp.zeros_like(so_ref)


def _hgrn_sample(layer, proj, lb_raw, gn_g, state, new_state, tb):
    n_layers, nb, heads, dk, dv = state.shape
    d = proj.shape[1] // 4
    steps = nb // tb
    aliased = new_state is not None
    assert aliased == (layer > 0) and tb == SUBLANES
    if aliased:
        grid = (steps,)
        s_in = s_out = pl.BlockSpec((1, tb, heads, dk, dv), lambda i: (layer, i, 0, 0, 0))
    else:
        grid = (n_layers * steps,)
        s_in = pl.BlockSpec((1, tb, heads, dk, dv), lambda i: (0, jnp.minimum(i, steps - 1), 0, 0, 0))
        s_out = pl.BlockSpec((1, tb, heads, dk, dv), lambda i: (i // steps, i % steps, 0, 0, 0))
    in_specs = [_full(proj.shape), _full(lb_raw.shape), _full(gn_g.shape), s_in]
    args = [proj, lb_raw, gn_g, state]
    if aliased:
        in_specs.append(pl.BlockSpec(memory_space=pl.ANY))
        args.append(new_state)
    return pl.pallas_call(
        functools.partial(_hgrn_sample_kernel, layer, heads, tb, steps, aliased),
        grid=grid,
        in_specs=in_specs,
        out_specs=[_full((nb, d)), s_out],
        out_shape=[jax.ShapeDtypeStruct((nb, d), F32), jax.ShapeDtypeStruct(state.shape, F32)],
        scratch_shapes=[pltpu.VMEM((heads, 4 * dk, nb), BF16), pltpu.VMEM((nb, d), F32)],
        input_output_aliases={4: 1} if aliased else {},
        compiler_params=_params("arbitrary"),
        name=f"hgrn_sample_{layer}",
    )(*args)


def _cache_roll_kernel(ck_ref, cv_ref, kn_ref, vn_ref, ko_ref, vo_ref, kb_ref, vb_ref):
    w = ck_ref.shape[1]
    ko_ref[:, 0:w - 1, :] = ck_ref[:, 1:w, :]
    ko_ref[:, w - 1:w, :] = kn_ref[...]
    vo_ref[:, 0:w - 1, :] = cv_ref[:, 1:w, :]
    vo_ref[:, w - 1:w, :] = vn_ref[...]
    kb_ref[...] = ko_ref[...].astype(BF16)
    vb_ref[...] = vo_ref[...].astype(BF16)


def _cache_roll(ck, cv, kn, vn, tb):
    nb, w, n = ck.shape
    blk = pl.BlockSpec((tb, w, n), lambda i: (i, 0, 0))
    new = pl.BlockSpec((tb, 1, n), lambda i: (i, 0, 0))
    return pl.pallas_call(
        _cache_roll_kernel,
        grid=(nb // tb,),
        in_specs=[blk, blk, new, new],
        out_specs=[blk, blk, blk, blk],
        out_shape=[jax.ShapeDtypeStruct(ck.shape, F32), jax.ShapeDtypeStruct(cv.shape, F32),
                   jax.ShapeDtypeStruct(ck.shape, BF16), jax.ShapeDtypeStruct(cv.shape, BF16)],
        compiler_params=_params("arbitrary"),
        name="cache_roll",
    )(ck, cv, kn.reshape(nb, 1, n), vn.reshape(nb, 1, n))


def _attn_sample_kernel(att_layer, tb, q_ref, k_ref, v_ref, sink_ref, o_ref):
    heads, head_dim = q_ref.shape[1], q_ref.shape[2]
    nkv = k_ref.shape[2]
    group = heads // (nkv // head_dim)
    r = lax.broadcasted_iota(jnp.int32, (heads, nkv), 0) // group
    c = lax.broadcasted_iota(jnp.int32, (heads, nkv), 1) // head_dim
    own = r == c
    sink = sink_ref[:, att_layer:att_layer + 1][None]
    q = q_ref[...]
    qe = jnp.where(own[None], jnp.concatenate([q] * (nkv // head_dim), axis=-1), 0.0).astype(BF16)
    s = jnp.einsum("thc,tjc->thj", qe, k_ref[...], preferred_element_type=F32)
    m = jnp.maximum(jnp.max(s, axis=-1, keepdims=True), sink)
    p = jnp.exp(s - m)
    denom = jnp.sum(p, axis=-1, keepdims=True) + jnp.exp(sink - m)
    pv = jnp.where(own[None], jnp.einsum("thj,tjc->thc", p.astype(BF16), v_ref[...],
                                         preferred_element_type=F32), 0.0)
    o = pv[:, :, 0:head_dim]
    for j in range(1, nkv // head_dim):
        o = o + pv[:, :, j * head_dim:(j + 1) * head_dim]
    o_ref[...] = o / denom


def _attn_sample(att_layer, q3, k, v, sinks_t, tb):
    nb, heads, head_dim = q3.shape
    w, nkv = k.shape[1], k.shape[2]
    return pl.pallas_call(
        functools.partial(_attn_sample_kernel, att_layer, tb),
        grid=(nb // tb,),
        in_specs=[pl.BlockSpec((tb, heads, head_dim), lambda i: (i, 0, 0)),
                  pl.BlockSpec((tb, w, nkv), lambda i: (i, 0, 0)),
                  pl.BlockSpec((tb, w, nkv), lambda i: (i, 0, 0)),
                  _full(sinks_t.shape)],
        out_specs=pl.BlockSpec((tb, heads, head_dim), lambda i: (i, 0, 0)),
        out_shape=jax.ShapeDtypeStruct(q3.shape, F32),
        compiler_params=_params("arbitrary"),
        name="attn_sample",
    )(q3, k, v, sinks_t)


def _rope_tables(pos, head_dim):
    half = head_dim // 2
    inv = ROPE_THETA ** (-jnp.arange(half, dtype=F32) / half)
    ang = pos.astype(F32)[:, None] * inv[None, :]
    cos, sin = jnp.cos(ang), jnp.sin(ang)
    reps = LANES // head_dim
    return (jnp.tile(jnp.concatenate([cos, cos], axis=1), (1, reps)),
            jnp.tile(jnp.concatenate([-sin, sin], axis=1), (1, reps)))


def kernel(x_prompt, x_sample, c_prompt, c_sample, state_hgrn, cache_k, cache_v, w_ada, b_ada, norm1_g, norm2_g, hg_w_in, hg_w_out, hg_lower_bounds, hg_gn_g, kv_w_ada, kv_b_ada, kv_norm_g, w_kv, k_norm_g, w_q, q_norm_g, sinks, w_o, w_up, w_down):
    bp, Lp, d = x_prompt.shape
    bs = x_sample.shape[0]
    depth = w_ada.shape[0]
    n_a = hg_w_in.shape[0]
    heads = state_hgrn.shape[2]
    window, kv_heads, head_dim = cache_k.shape[1], cache_k.shape[2], cache_k.shape[3]
    assert LANES == 2 * head_dim and kv_heads % 2 == 0 and Lp % window == 0
    assert window & (window - 1) == 0 and (d // head_dim) // kv_heads == 4
    nkv = kv_heads * head_dim
    reps = LANES // head_dim
    bf = lambda t: t.astype(BF16)

    pad = (-(bs + bp)) % 16
    c_all = jnp.concatenate([c_sample, c_prompt, jnp.zeros((pad, d), F32)], axis=0)
    mods_s, mods_p = _ada(c_all, bs, bp, w_ada, b_ada, T.ada_cols)
    kv_mods_s, kv_mods_p = _ada(c_all, bs, bp, kv_w_ada[None], kv_b_ada[None], T.kv_ada_cols)

    cos_p, sin_p = _rope_tables(jnp.arange(Lp), head_dim)
    cos_s, sin_s = _rope_tables(jnp.full((1,), PAST_LEN), head_dim)
    kg = jnp.tile(k_norm_g, reps)[None]
    qg = jnp.tile(q_norm_g, (1, reps))
    sinks_t = sinks.T
    w_kv_b, w_in_b, w_out_b, w_q_b, w_o_b = bf(w_kv), bf(hg_w_in), bf(hg_w_out), bf(w_q), bf(w_o)
    w_up_b, w_dn_b = bf(w_up), bf(w_down)

    xp = x_prompt
    xs = x_sample.reshape(bs, d)
    hg_p, hg_s = [], None
    k_p = v_p = kx_p = vx_p = k_s = v_s = None
    for l in range(depth):
        if l == n_a:
            k_p, v_p, kx_p, vx_p = _kv(xp, kv_mods_p, kv_norm_g[None], w_kv_b, kg, cos_p, sin_p,
                                       head_dim, window, T.kv_rows)
            k_n, v_n, _, _ = _kv(xs[None], kv_mods_s, kv_norm_g[None], w_kv_b, kg, cos_s, sin_s,
                                 head_dim, bs, bs)
            k_s, v_s, kb_s, vb_s = _cache_roll(cache_k.reshape(bs, window, nkv),
                                               cache_v.reshape(bs, window, nkv), k_n[0], v_n[0], T.cache_tokens)
        if l < n_a:
            w_mix_b, mix_layer = w_out_b, l
            a_p, s_p = _hgrn_prompt(l, xp, mods_p, norm1_g, w_in_b, hg_lower_bounds, hg_gn_g, heads,
                                    T.hgrn_chunk, T.hgrn_rows)
            hg_p.append(s_p)
            proj_s = _proj(l, xs, mods_s, norm1_g, w_in_b, T.proj_cols)
            a_s, hg_s = _hgrn_sample(l, proj_s, hg_lower_bounds, hg_gn_g, state_hgrn, hg_s, SUBLANES)
        else:
            j = l - n_a
            w_mix_b, mix_layer = w_o_b, j
            a_p = _attn_prompt(l, j, xp, mods_p, norm1_g, w_q_b, qg, cos_p, sin_p, kx_p, vx_p, sinks,
                               head_dim, kv_heads, window, T.attn_blocks)
            q_s = _q_sample(l, j, xs, mods_s, norm1_g, w_q_b, qg, cos_s, sin_s, head_dim)
            a_s = _attn_sample(j, q_s.reshape(bs, d // head_dim, head_dim), kb_s, vb_s, sinks_t,
                               T.attn_sample_tokens)
            a_s = a_s.reshape(bs, d)
        xp = _post(l, xp, a_p, mods_p, norm2_g, w_mix_b, mix_layer, w_up_b, w_dn_b, T.mlp_rows)
        xs = _post_sample(l, xs, a_s, mods_s, norm2_g, w_mix_b, mix_layer, w_up_b, w_dn_b, T.mlp_sample_slab)

    shape4 = lambda t: t.reshape(t.shape[0], window, kv_heads, head_dim)
    return (xp, xs.reshape(bs, 1, d), jnp.stack(hg_p), shape4(k_p), shape4(v_p), hg_s,
            shape4(k_s), shape4(v_s))
```

```python
import functools
import math
from typing import NamedTuple

import numpy as np
import jax
import jax.numpy as jnp
from jax import lax
from jax.experimental import pallas as pl
from jax.experimental.pallas import tpu as pltpu

F32 = jnp.float32
BF16 = jnp.bfloat16

PAST_LEN = 8192
ROPE_THETA = 10000.0
EPS = 1e-6
LOG2E = 1.4426950408889634
LANES = 128
SUBLANES = 8
VMEM_LIMIT = 56 * 1024 * 1024


class _Tiles(NamedTuple):
    ada_cols: int = 3072
    kv_ada_cols: int = 1024
    hgrn_chunk: int = 128
    hgrn_rows: int = 512
    mlp_rows: int = 512
    kv_rows: int = 512
    attn_blocks: int = 8
    proj_cols: int = 1024
    cache_tokens: int = 8
    attn_sample_tokens: int = 16
    mlp_sample_slab: int = 1024


T = _Tiles()

NT_DIMS = (((1,), (1,)), ((), ()))
TN_DIMS = (((0,), (0,)), ((), ()))


def _dot(a, b):
    return jnp.dot(a, b, preferred_element_type=F32)


def _dot_nt(a, b):
    return lax.dot_general(a, b, NT_DIMS, preferred_element_type=F32)


def _sigmoid(x):
    return 1.0 / (1.0 + jnp.exp(-x))


def _rms(x, g):
    ms = jnp.mean(x * x, axis=-1, keepdims=True)
    return x * lax.rsqrt(ms + EPS) * g


def _modulated_norm(x, gain, shift, scale):
    return _rms(x, gain) * (1.0 + scale) + shift


def _params(*sem):
    return pltpu.CompilerParams(dimension_semantics=sem, vmem_limit_bytes=VMEM_LIMIT)


def _full(shape):
    n = len(shape)
    return pl.BlockSpec(shape, lambda *_: (0,) * n)


def _ada_kernel(c_ref, w_ref, b_ref, os_ref, op_ref):
    bs, bp = os_ref.shape[1], op_ref.shape[1]
    c = c_ref[...]
    a = (c * _sigmoid(c)).astype(BF16)
    res = _dot(a, w_ref[0].astype(BF16)) + b_ref[0]
    os_ref[0] = res[0:bs]
    for r in range(bp):
        op_ref[0, r] = jnp.broadcast_to(res[bs + r:bs + r + 1], op_ref.shape[2:])


def _ada(c_all, bs, bp, w, b, tn):
    nl, d, n = w.shape
    r = c_all.shape[0]
    return pl.pallas_call(
        _ada_kernel,
        grid=(nl, n // tn),
        in_specs=[pl.BlockSpec((r, d), lambda l, j: (0, 0)),
                  pl.BlockSpec((1, d, tn), lambda l, j: (l, 0, j)),
                  pl.BlockSpec((1, 1, tn), lambda l, j: (l, 0, j))],
        out_specs=[pl.BlockSpec((1, bs, tn), lambda l, j: (l, 0, j)),
                   pl.BlockSpec((1, bp, SUBLANES, tn), lambda l, j: (l, 0, 0, j))],
        out_shape=[jax.ShapeDtypeStruct((nl, bs, n), F32),
                   jax.ShapeDtypeStruct((nl, bp, SUBLANES, n), F32)],
        compiler_params=_params("arbitrary", "arbitrary"),
        name="ada",
    )(c_all, w, b.reshape(nl, 1, n))


def _mod_rows(mod_ref):
    return mod_ref[0, 0, 0:1, :] if len(mod_ref.shape) == 4 else mod_ref[0]


def _mod_spec(mods, layer):
    if mods.ndim == 4:
        return pl.BlockSpec((1, 1) + mods.shape[2:], lambda i, *_: (layer, i, 0, 0))
    return pl.BlockSpec((1,) + mods.shape[1:], lambda *_: (layer, 0, 0))


def _layer_spec(stack, layer):
    zeros = (0,) * (stack.ndim - 1)
    return pl.BlockSpec((1,) + stack.shape[1:], lambda *_: (layer,) + zeros, pipeline_mode=pl.Buffered(1))


def _group_mean_matrix(group, width):
    r = lax.broadcasted_iota(jnp.int32, (width, width), 0) // group
    c = lax.broadcasted_iota(jnp.int32, (width, width), 1) // group
    return jnp.where(r == c, 1.0 / group, 0.0).astype(BF16)


def _heads_norm_rope(y, gain, cos, sin, head_dim, out_scale):
    rows, n = y.shape
    width = 2 * LANES if n % (2 * LANES) == 0 else LANES
    mean = _group_mean_matrix(head_dim, width)
    gain = gain * out_scale
    half = head_dim // 2
    first = (lax.broadcasted_iota(jnp.int32, (rows, LANES), 1) % head_dim) < half
    out = []
    for j in range(0, n, width):
        ms = _dot(jnp.square(y[:, j:j + width]).astype(BF16), mean)
        for i in range(0, width, LANES):
            yn = y[:, j + i:j + i + LANES] * lax.rsqrt(ms[:, i:i + LANES] + EPS) * gain
            rot = jnp.where(first, pltpu.roll(yn, LANES - half, 1), pltpu.roll(yn, half, 1))
            out.append(yn * cos + rot * sin)
    return out


def _lower_bound(raw, layer):
    m = jnp.max(raw, axis=0, keepdims=True)
    e = jnp.exp(raw - m)
    sm = e / jnp.sum(e, axis=0, keepdims=True)
    acc = sm[0:1]
    for j in range(1, layer + 1):
        acc = acc + sm[j:j + 1]
    return acc - sm[0:1]


def _hgrn_gates(proj, lb, d):
    qa = proj[:, 0:d]
    q = qa * _sigmoid(qa)
    fg = lb + (1.0 - lb) * _sigmoid(proj[:, d:2 * d])
    return q, fg


def _gated_group_norm(o, gain, gate_pre):
    ms = jnp.mean(o * o, axis=-1, keepdims=True)
    return o * lax.rsqrt(ms + EPS) * gain * (gate_pre * _sigmoid(gate_pre))


def _hgrn_prompt_kernel(layer, heads, x_ref, mod_ref, n1g_ref, win_ref, lbraw_ref, gng_ref,
                        lev_ref, sums_ref, o_ref, s_ref, st_scr, d_scr, xs_scr, att_scr):
    c = pl.program_id(1)
    total, d = x_ref.shape[1], x_ref.shape[2]
    rows = lev_ref.shape[0]
    dk = d // heads
    n_levels = int(math.log2(rows))

    @pl.when(c == 0)
    def _():
        st_scr[...] = jnp.zeros_like(st_scr)

    mod = _mod_rows(mod_ref)
    h = _modulated_norm(x_ref[0], n1g_ref[layer:layer + 1, :], mod[:, 0:d], mod[:, d:2 * d])
    proj_all = _dot(h.astype(BF16), win_ref[0])
    lb = _lower_bound(lbraw_ref[...], layer)
    q_all, fg_all = _hgrn_gates(proj_all, lb, d)
    lf2_all = jnp.log(fg_all) * LOG2E
    n_table = d_scr.shape[0] // rows - 1
    lev = lev_ref[...]
    row_id = lax.broadcasted_iota(jnp.int32, (rows, dk), 0)
    upper = [((row_id >> p) & 1) == 1 for p in range(n_table + 1)]
    col = lax.broadcasted_iota(jnp.int32, (1, rows), 1)
    gng = gng_ref[layer:layer + 1, :]

    for r0 in range(0, total, rows):
        proj, q, fg = (t[r0:r0 + rows] for t in (proj_all, q_all, fg_all))
        k = 1.0 - fg
        lf2 = lf2_all[r0:r0 + rows]
        hi = lf2.astype(BF16)
        lo = (lf2 - hi.astype(F32)).astype(BF16)
        d_scr[...] = _dot(sums_ref[...], jnp.concatenate([hi, lo], axis=0))

        for hh in range(heads):
            sl = slice(hh * dk, (hh + 1) * dk)
            q_h, k_h = q[:, sl], k[:, sl]
            cum = d_scr[n_table * rows:(n_table + 1) * rows, sl]
            xs_scr[hh, 0] = jnp.where(upper[0], q_h * fg[:, sl], k_h).astype(BF16)
            for p in range(1, n_levels):
                m = 1 << p
                if p <= n_table:
                    e = jnp.exp2(d_scr[(p - 1) * rows:p * rows, sl])
                    xs_scr[hh, p] = (jnp.where(upper[p], q_h, k_h) * e).astype(BF16)
                    continue
                pieces = []
                for r in range(0, rows, m):
                    ref = (r // (2 * m)) * 2 * m + m - 1
                    if (r // m) % 2:
                        pieces.append(q_h[r:r + m] * jnp.exp2(cum[r:r + m] - cum[ref:ref + 1]))
                    else:
                        pieces.append(k_h[r:r + m] * jnp.exp2(cum[ref:ref + 1] - cum[r:r + m]))
                xs_scr[hh, p] = jnp.concatenate(pieces, axis=0).astype(BF16)
            cum_last = cum[rows - 1:rows, :]
            xs_scr[hh, n_levels] = (q_h * jnp.exp2(cum)).astype(BF16)
            xs_scr[hh, n_levels + 1] = (k_h * jnp.exp2(cum_last - cum)).astype(BF16)

        for hh in range(heads):
            xs = xs_scr[hh, 0]
            att = jnp.where(lev == 0, _dot_nt(xs, xs), 0.0)
            for p in range(1, n_levels):
                m = 1 << p
                xs = xs_scr[hh, p]
                pp = _dot_nt(xs, xs)
                if p <= n_table:
                    att = jnp.where(lev == p, pp, att)
                else:
                    att = jnp.concatenate(
                        [jnp.where((col >= r - m) & (col < r), pp[r:r + m], att[r:r + m]) if (r // m) % 2
                         else att[r:r + m] for r in range(0, rows, m)], axis=0)
            att_scr[hh] = att.astype(BF16)

        for hh in range(heads):
            sl = slice(hh * dk, (hh + 1) * dk)
            v_f = proj[:, 2 * d + hh * dk:2 * d + (hh + 1) * dk]
            v_h = v_f.astype(BF16)
            cum_last = d_scr[(n_table + 1) * rows - 1:(n_table + 1) * rows, sl]
            st = st_scr[hh]
            o_h = (_dot(jnp.concatenate([att_scr[hh], xs_scr[hh, n_levels]], axis=1),
                        jnp.concatenate([v_h, st.T.astype(BF16)], axis=0))
                   + jnp.sum(q[:, sl] * k[:, sl], axis=-1, keepdims=True) * v_f)
            st_scr[hh] = st * jnp.exp2(cum_last) + lax.dot_general(
                v_h, xs_scr[hh, n_levels + 1], TN_DIMS, preferred_element_type=F32)
            g_pre = proj[:, 3 * d + hh * dk:3 * d + (hh + 1) * dk]
            o_ref[0, r0:r0 + rows, sl] = _gated_group_norm(o_h, gng, g_pre)

    @pl.when(c == pl.num_programs(1) - 1)
    def _():
        for hh in range(heads):
            s_ref[0, hh] = st_scr[hh].T


def _level_table(rows):
    t = np.arange(rows)[:, None]
    s = np.arange(rows)[None, :]
    x = t ^ s
    lev = np.where(t > s, np.floor(np.log2(np.maximum(x, 1))).astype(np.int32), np.where(t == s, -1, -2))
    return jnp.asarray(lev, dtype=jnp.int32)


def _sum_table(rows):
    t = np.arange(rows)[:, None]
    j = np.arange(rows)[None, :]
    blocks = []
    for p in range(1, int(math.log2(SUBLANES))):
        m = 1 << p
        ref = (t // (2 * m)) * (2 * m) + m - 1
        up = ((t >> p) & 1) == 1
        blocks.append(np.where(up, (j > ref) & (j <= t), (j > t) & (j <= ref)))
    blocks.append(j <= t)
    table = np.concatenate(blocks, axis=0).astype(np.float32)
    return jnp.asarray(np.concatenate([table, table], axis=1), dtype=BF16)


def _hgrn_prompt(layer, x, mods, n1g, w_in, lb_raw, gn_g, heads, chunk, step_rows):
    b, L, d = x.shape
    dk = d // heads
    sums = _sum_table(chunk)
    return pl.pallas_call(
        functools.partial(_hgrn_prompt_kernel, layer, heads),
        grid=(b, L // step_rows),
        in_specs=[pl.BlockSpec((1, step_rows, d), lambda i, c: (i, c, 0)),
                  _mod_spec(mods, layer),
                  _full(n1g.shape), _layer_spec(w_in, layer), _full(lb_raw.shape), _full(gn_g.shape),
                  _full((chunk, chunk)), _full(sums.shape)],
        out_specs=[pl.BlockSpec((1, step_rows, d), lambda i, c: (i, c, 0)),
                   pl.BlockSpec((1, heads, dk, dk), lambda i, c: (i, 0, 0, 0))],
        out_shape=[jax.ShapeDtypeStruct((b, L, d), F32),
                   jax.ShapeDtypeStruct((b, heads, dk, dk), F32)],
        scratch_shapes=[pltpu.VMEM((heads, dk, dk), F32), pltpu.VMEM((sums.shape[0], d), F32),
                        pltpu.VMEM((heads, int(math.log2(chunk)) + 2, chunk, dk), BF16),
                        pltpu.VMEM((heads, chunk, chunk), BF16)],
        compiler_params=_params("arbitrary", "arbitrary"),
        name=f"hgrn_prompt_{layer}",
    )(x, mods, n1g, w_in, lb_raw, gn_g, _level_table(chunk), sums)


def _post_kernel(layer, x_ref, a_ref, mod_ref, n2g_ref, wo_ref, wup_ref, wdn_ref, o_ref):
    d = x_ref.shape[2]
    mod = _mod_rows(mod_ref)
    g1, sh2, sc2, g2 = (mod[:, j * d:(j + 1) * d] for j in range(2, 6))
    x1 = x_ref[0] + g1 * _dot(a_ref[0].astype(BF16), wo_ref[0])
    h2 = _modulated_norm(x1, n2g_ref[layer:layer + 1, :], sh2, sc2).astype(BF16)
    y = None
    for c in range(0, wup_ref.shape[2], d):
        u = jnp.square(jnp.maximum(_dot(h2, wup_ref[0, :, c:c + d]), 0.0)).astype(BF16)
        part = _dot(u, wdn_ref[0, c:c + d, :])
        y = part if y is None else y + part
    o_ref[0] = x1 + g2 * y


def _post_sample_kernel(layer, x_ref, a_ref, mod_ref, n2g_ref, wo_ref, wup_ref, wdn_ref, o_ref,
                        x1_scr, h2_scr, y_scr):
    c = pl.program_id(0)
    d = x_ref.shape[1]
    mod = _mod_rows(mod_ref)

    @pl.when(c == 0)
    def _():
        g1, sh2, sc2 = (mod[:, j * d:(j + 1) * d] for j in range(2, 5))
        x1 = x_ref[...] + g1 * _dot(a_ref[...].astype(BF16), wo_ref[0])
        x1_scr[...] = x1
        h2_scr[...] = _modulated_norm(x1, n2g_ref[layer:layer + 1, :], sh2, sc2).astype(BF16)
        y_scr[...] = jnp.zeros_like(y_scr)

    u = jnp.square(jnp.maximum(_dot(h2_scr[...], wup_ref[0]), 0.0)).astype(BF16)
    y_scr[...] += _dot(u, wdn_ref[0])

    @pl.when(c == pl.num_programs(0) - 1)
    def _():
        o_ref[...] = x1_scr[...] + mod[:, 5 * d:6 * d] * y_scr[...]


def _post_sample(layer, x, a, mods, n2g, w_o, o_layer, w_up, w_dn, slab):
    r, d = x.shape
    ff = w_up.shape[2]
    return pl.pallas_call(
        functools.partial(_post_sample_kernel, layer),
        grid=(ff // slab,),
        in_specs=[_full((r, d)), _full((r, d)), _mod_spec(mods, layer), _full(n2g.shape),
                  _layer_spec(w_o, o_layer),
                  pl.BlockSpec((1, d, slab), lambda c: (layer, 0, c)),
                  pl.BlockSpec((1, slab, d), lambda c: (layer, c, 0))],
        out_specs=_full((r, d)),
        out_shape=jax.ShapeDtypeStruct((r, d), F32),
        scratch_shapes=[pltpu.VMEM((r, d), F32), pltpu.VMEM((r, d), BF16), pltpu.VMEM((r, d), F32)],
        compiler_params=_params("arbitrary"),
        name="post_mlp_sample",
    )(x, a, mods, n2g, w_o, w_up, w_dn)


def _post(layer, x, a, mods, n2g, w_o, o_layer, w_up, w_dn, tm):
    nb, L, d = x.shape
    return pl.pallas_call(
        functools.partial(_post_kernel, layer),
        grid=(nb, L // tm),
        in_specs=[pl.BlockSpec((1, tm, d), lambda i, j: (i, j, 0)),
                  pl.BlockSpec((1, tm, d), lambda i, j: (i, j, 0)),
                  _mod_spec(mods, layer), _full(n2g.shape),
                  _layer_spec(w_o, o_layer), _layer_spec(w_up, layer), _layer_spec(w_dn, layer)],
        out_specs=pl.BlockSpec((1, tm, d), lambda i, j: (i, j, 0)),
        out_shape=jax.ShapeDtypeStruct((nb, L, d), F32),
        compiler_params=_params("arbitrary", "arbitrary"),
        name="post_mlp",
    )(x, a, mods, n2g, w_o, w_up, w_dn)


def _kv_kernel(head_dim, window, x_ref, mod_ref, g_ref, w_ref, kg_ref, cos_ref, sin_ref,
               k_ref, v_ref, kx_ref, vx_ref):
    tm, d = x_ref.shape[1], x_ref.shape[2]
    mod = _mod_rows(mod_ref)
    hk = _modulated_norm(x_ref[0], g_ref[...], mod[:, 0:d], mod[:, d:2 * d])
    y = _dot(hk.astype(BF16), w_ref[...])
    n = y.shape[1] // 2
    cos, sin, kg = cos_ref[...], sin_ref[...], kg_ref[...]
    low = lax.broadcasted_iota(jnp.int32, (tm, LANES), 1) < head_dim
    k_blocks, v_blocks = _heads_norm_rope(y[:, 0:n], kg, cos, sin, head_dim, 1.0), []
    for j in range(n // LANES):
        kb = k_blocks[j]
        vb = y[:, n + j * LANES:n + (j + 1) * LANES]
        v_blocks.append(vb)
        kx_ref[0, :, j * LANES:(j + 1) * LANES] = kb.astype(BF16)
        kx_ref[0, :, n + j * LANES:n + (j + 1) * LANES] = pltpu.roll(kb, head_dim, 1).astype(BF16)
        vs = pltpu.roll(vb, head_dim, 1)
        variants = (jnp.where(low, vb, 0.0), jnp.where(low, 0.0, vs),
                    jnp.where(low, vs, 0.0), jnp.where(low, 0.0, vb))
        for i, var in enumerate(variants):
            off = (4 * j + i) * LANES
            vx_ref[0, off:off + LANES, :] = var.astype(BF16).T

    @pl.when(pl.program_id(1) == pl.num_programs(1) - 1)
    def _():
        for j in range(n // LANES):
            k_ref[0, :, j * LANES:(j + 1) * LANES] = k_blocks[j][tm - window:, :]
            v_ref[0, :, j * LANES:(j + 1) * LANES] = v_blocks[j][tm - window:, :]


def _kv(x, mods, g, w_kv, kg, cos, sin, head_dim, window, tm):
    nb, L, d = x.shape
    n = w_kv.shape[1] // 2
    rope_rows = tm if cos.shape[0] > 1 else 1
    rope_map = (lambda i, j: (j, 0)) if cos.shape[0] > 1 else (lambda i, j: (0, 0))
    last = lambda i, j: (i, 0, 0)
    tile = lambda i, j: (i, j, 0)
    return pl.pallas_call(
        functools.partial(_kv_kernel, head_dim, window),
        grid=(nb, L // tm),
        in_specs=[pl.BlockSpec((1, tm, d), tile), _mod_spec(mods, 0),
                  _full((1, d)), _full(w_kv.shape), _full((1, LANES)),
                  pl.BlockSpec((rope_rows, LANES), rope_map),
                  pl.BlockSpec((rope_rows, LANES), rope_map)],
        out_specs=[pl.BlockSpec((1, window, n), last), pl.BlockSpec((1, window, n), last),
                   pl.BlockSpec((1, tm, 2 * n), tile), pl.BlockSpec((1, 4 * n, tm), lambda i, j: (i, 0, j))],
        out_shape=[jax.ShapeDtypeStruct((nb, window, n), F32), jax.ShapeDtypeStruct((nb, window, n), F32),
                   jax.ShapeDtypeStruct((nb, L, 2 * n), BF16), jax.ShapeDtypeStruct((nb, 4 * n, L), BF16)],
        compiler_params=_params("arbitrary", "arbitrary"),
        name="kv_proj",
    )(x, mods, g, w_kv, kg, cos, sin)


def _q_proj(x, mod, n1g, wq, qg, cos, sin, head_dim, extra_scale=1.0):
    d = x.shape[1]
    h = _modulated_norm(x, n1g, mod[:, 0:d], mod[:, d:2 * d])
    y = _dot(h.astype(BF16), wq)
    return _heads_norm_rope(y, qg, cos, sin, head_dim, extra_scale / math.sqrt(head_dim))


def _attn_prompt_kernel(layer, att_layer, head_dim, kv_heads, nq, x_ref, mod_ref, n1g_ref, wq_ref, qg_ref,
                        cos_ref, sin_ref, *rest):
    k_refs, v_refs = rest[0:nq + 1], rest[nq + 1:2 * nq + 2]
    sink_ref, o_ref, s_scr, p_scr = rest[2 * nq + 2:]
    w = x_ref.shape[1] // nq
    nkv = kv_heads * head_dim
    q_blocks = _q_proj(x_ref[0], _mod_rows(mod_ref), n1g_ref[layer:layer + 1, :], wq_ref[0],
                       qg_ref[att_layer:att_layer + 1, :], cos_ref[...], sin_ref[...], head_dim, LOG2E)
    group = (2 * len(q_blocks)) // kv_heads
    key = lax.broadcasted_iota(jnp.int32, (w, 2 * w), 0)
    qry = lax.broadcasted_iota(jnp.int32, (w, 2 * w), 1)
    own = key <= (qry & (w - 1))
    first = lax.broadcasted_iota(jnp.int32, (1, 2 * w), 1) < w
    low = lax.broadcasted_iota(jnp.int32, (w, LANES), 1) < head_dim
    no_prev = jnp.where(pl.program_id(1) == 0, -jnp.inf, 0.0)
    tiles = [(u, kh, half) for u in range(nq) for kh in range(kv_heads) for half in range(2)]
    for n, (u, kh, half) in enumerate(tiles):
        ha, hb = kh * group + half, kh * group + 2 + half
        mask = (lambda t: jnp.where(low, t, 0.0)) if half == 0 else (lambda t: jnp.where(low, 0.0, t))
        rows = slice(u * w, (u + 1) * w)
        qcat = jnp.concatenate([mask(q_blocks[ha // 2][rows]), mask(q_blocks[hb // 2][rows])],
                               axis=0).astype(BF16)
        koff = (kh // 2) * LANES + (0 if kh % 2 == half else nkv)
        kcat = jnp.concatenate([k_refs[u][0, :, koff:koff + LANES], k_refs[u + 1][0, :, koff:koff + LANES]],
                               axis=0)
        s_t = _dot_nt(kcat, qcat)
        s_prev = s_t[:w] + no_prev if u == 0 else s_t[:w]
        s_scr[n] = jnp.where(own, s_t[w:], s_prev)
    for n, (u, kh, half) in enumerate(tiles):
        sc = s_scr[n]
        sink = jnp.where(first, sink_ref[att_layer, kh * group + half],
                         sink_ref[att_layer, kh * group + 2 + half]) * LOG2E
        m = jnp.maximum(jnp.max(sc, axis=0, keepdims=True), sink)
        p = jnp.exp2(sc - m)
        pn = p * (1.0 / (jnp.sum(p, axis=0, keepdims=True) + jnp.exp2(sink - m)))
        p_scr[n, 0:w] = jnp.where(own, pn, 0.0).astype(BF16)
        p_scr[n, w:2 * w] = jnp.where(own, 0.0, pn).astype(BF16)
    for u in range(nq):
        for kh in range(kv_heads):
            out_t = None
            for half in range(2):
                voff = (2 * kh + half) * LANES
                v_t = jnp.concatenate([v_refs[u + 1][0, voff:voff + LANES, :],
                                       v_refs[u][0, voff:voff + LANES, :]], axis=1)
                part = _dot(v_t, p_scr[(u * kv_heads + kh) * 2 + half])
                out_t = part if out_t is None else out_t + part
            for i in range(2):
                o_ref[0, u * w:(u + 1) * w, (2 * kh + i) * LANES:(2 * kh + i + 1) * LANES] = (
                    out_t[:, i * w:(i + 1) * w].T)


def _attn_prompt(layer, att_layer, x, mods, n1g, wq, qg, cos, sin, kx, vx, sinks, head_dim, kv_heads, w, nq):
    b, L, d = x.shape
    rows = lambda i, j: (i, j, 0)
    k_specs = [pl.BlockSpec((1, w, kx.shape[2]), lambda i, j, o=o: (i, jnp.maximum(nq * j + o, 0), 0))
               for o in range(-1, nq)]
    v_specs = [pl.BlockSpec((1, vx.shape[1], w), lambda i, j, o=o: (i, 0, jnp.maximum(nq * j + o, 0)))
               for o in range(-1, nq)]
    n_tiles = nq * 2 * kv_heads
    return pl.pallas_call(
        functools.partial(_attn_prompt_kernel, layer, att_layer, head_dim, kv_heads, nq),
        grid=(b, L // (nq * w)),
        in_specs=[pl.BlockSpec((1, nq * w, d), rows), _mod_spec(mods, layer),
                  _full(n1g.shape), _layer_spec(wq, att_layer), _full(qg.shape),
                  pl.BlockSpec((nq * w, LANES), lambda i, j: (j, 0)),
                  pl.BlockSpec((nq * w, LANES), lambda i, j: (j, 0))]
                 + k_specs + v_specs + [pl.BlockSpec(memory_space=pltpu.SMEM)],
        out_specs=pl.BlockSpec((1, nq * w, d), rows),
        out_shape=jax.ShapeDtypeStruct((b, L, d), F32),
        scratch_shapes=[pltpu.VMEM((n_tiles, w, 2 * w), F32), pltpu.VMEM((n_tiles, 2 * w, 2 * w), BF16)],
        compiler_params=_params("arbitrary", "arbitrary"),
        name="attn_prompt",
    )(x, mods, n1g, wq, qg, cos, sin, *([kx] * (nq + 1)), *([vx] * (nq + 1)), sinks)


def _proj_kernel(layer, x_ref, mod_ref, g_ref, w_ref, o_ref):
    d = x_ref.shape[1]
    mod = _mod_rows(mod_ref)
    h = _modulated_norm(x_ref[...], g_ref[layer:layer + 1, :], mod[:, 0:d], mod[:, d:2 * d])
    o_ref[...] = _dot(h.astype(BF16), w_ref[0])


def _proj(layer, x, mods, g, w, tn):
    r, d = x.shape
    n = w.shape[2]
    return pl.pallas_call(
        functools.partial(_proj_kernel, layer),
        grid=(n // tn,),
        in_specs=[_full((r, d)), _mod_spec(mods, layer), _full(g.shape),
                  pl.BlockSpec((1, d, tn), lambda j: (layer, 0, j))],
        out_specs=pl.BlockSpec((r, tn), lambda j: (0, j)),
        out_shape=jax.ShapeDtypeStruct((r, n), F32),
        compiler_params=_params("arbitrary"),
        name="proj_sample",
    )(x, mods, g, w)


def _q_sample_kernel(layer, att_layer, head_dim, x_ref, mod_ref, n1g_ref, wq_ref, qg_ref, cos_ref, sin_ref,
                     o_ref):
    blocks = _q_proj(x_ref[...], _mod_rows(mod_ref), n1g_ref[layer:layer + 1, :], wq_ref[0],
                     qg_ref[att_layer:att_layer + 1, :], cos_ref[...], sin_ref[...], head_dim)
    for j, blk in enumerate(blocks):
        o_ref[:, j * LANES:(j + 1) * LANES] = blk


def _q_sample(layer, att_layer, x, mods, n1g, wq, qg, cos, sin, head_dim):
    r, d = x.shape
    return pl.pallas_call(
        functools.partial(_q_sample_kernel, layer, att_layer, head_dim),
        grid=(1,),
        in_specs=[_full((r, d)), _mod_spec(mods, layer), _full(n1g.shape), _layer_spec(wq, att_layer),
                  _full(qg.shape), _full((1, LANES)), _full((1, LANES))],
        out_specs=_full((r, wq.shape[2])),
        out_shape=jax.ShapeDtypeStruct((r, wq.shape[2]), F32),
        compiler_params=_params("arbitrary"),
        name="q_sample",
    )(x, mods, n1g, wq, qg, cos, sin)


def _hgrn_sample_kernel(layer, heads, tb, steps, aliased, proj_ref, lbraw_ref, gng_ref, s_ref, *rest):
    o_ref, so_ref, stack_scr, acc_scr = rest[1:] if aliased else rest
    i = pl.program_id(0)
    rows = proj_ref.shape[0]
    d = proj_ref.shape[1] // 4
    dk = d // heads

    def update():
        @pl.when(i == 0)
        def _():
            lb = _lower_bound(lbraw_ref[...], layer)
            q, fg = _hgrn_gates(proj_ref[:, 0:2 * d], lb, d)
            for hh in range(heads):
                sl = slice(hh * dk, (hh + 1) * dk)
                ft = fg[:, sl].T
                hi = ft.astype(BF16)
                stack_scr[hh, 0:dk] = hi
                stack_scr[hh, dk:2 * dk] = (ft - hi.astype(F32)).astype(BF16)
                stack_scr[hh, 2 * dk:3 * dk] = (1.0 - ft).astype(BF16)
                stack_scr[hh, 3 * dk:4 * dk] = q[:, sl].T.astype(BF16)

        token_row = lax.broadcasted_iota(jnp.int32, (rows, dk), 0)
        sub = lax.broadcasted_iota(jnp.int32, (tb, dk), 0)
        base = pl.multiple_of(i * tb, tb)
        v_rows = proj_ref[pl.ds(base, tb), 2 * d:3 * d]
        o_rows = [jnp.zeros((tb, dk), F32) for _ in range(heads)]
        for t0 in range(0, tb, 2):
            onehot = jnp.concatenate([jnp.where(token_row == base + t, 1.0, 0.0) for t in (t0, t0 + 1)],
                                     axis=1).astype(BF16)
            for hh in range(heads):
                bc2 = _dot(stack_scr[hh], onehot)
                for t in (t0, t0 + 1):
                    bc = bc2[:, (t - t0) * dk:(t - t0 + 1) * dk]
                    f_b = bc[0:dk] + bc[dk:2 * dk]
                    k_b = bc[2 * dk:3 * dk]
                    q_b = bc[3 * dk:4 * dk]
                    s_new = f_b * s_ref[0, t, hh] + k_b * v_rows[t:t + 1, hh * dk:(hh + 1) * dk]
                    so_ref[0, t, hh] = s_new
                    o_rows[hh] = jnp.where(sub == t, jnp.sum(q_b * s_new, axis=0, keepdims=True), o_rows[hh])
        for hh in range(heads):
            acc_scr[pl.ds(base, tb), hh * dk:(hh + 1) * dk] = o_rows[hh]

        @pl.when(i == steps - 1)
        def _():
            gng = gng_ref[layer:layer + 1, :]
            for hh in range(heads):
                sl = slice(hh * dk, (hh + 1) * dk)
                o_ref[:, sl] = _gated_group_norm(acc_scr[:, sl], gng,
                                                 proj_ref[:, 3 * d + hh * dk:3 * d + (hh + 1) * dk])

    if aliased:
        update()
    else:
        pl.when(i < steps)(update)

        @pl.when(i >= steps)
        def _():
            so_ref[...] = jnp.zeros_like(so_ref)


def _hgrn_sample(layer, proj, lb_raw, gn_g, state, new_state, tb):
    n_layers, nb, heads, dk, dv = state.shape
    d = proj.shape[1] // 4
    steps = nb // tb
    aliased = new_state is not None
    assert aliased == (layer > 0) and tb == SUBLANES
    if aliased:
        grid = (steps,)
        s_in = s_out = pl.BlockSpec((1, tb, heads, dk, dv), lambda i: (layer, i, 0, 0, 0))
    else:
        grid = (n_layers * steps,)
        s_in = pl.BlockSpec((1, tb, heads, dk, dv), lambda i: (0, jnp.minimum(i, steps - 1), 0, 0, 0))
        s_out = pl.BlockSpec((1, tb, heads, dk, dv), lambda i: (i // steps, i % steps, 0, 0, 0))
    in_specs = [_full(proj.shape), _full(lb_raw.shape), _full(gn_g.shape), s_in]
    args = [proj, lb_raw, gn_g, state]
    if aliased:
        in_specs.append(pl.BlockSpec(memory_space=pl.ANY))
        args.append(new_state)
    return pl.pallas_call(
        functools.partial(_hgrn_sample_kernel, layer, heads, tb, steps, aliased),
        grid=grid,
        in_specs=in_specs,
        out_specs=[_full((nb, d)), s_out],
        out_shape=[jax.ShapeDtypeStruct((nb, d), F32), jax.ShapeDtypeStruct(state.shape, F32)],
        scratch_shapes=[pltpu.VMEM((heads, 4 * dk, nb), BF16), pltpu.VMEM((nb, d), F32)],
        input_output_aliases={4: 1} if aliased else {},
        compiler_params=_params("arbitrary"),
        name=f"hgrn_sample_{layer}",
    )(*args)


def _cache_roll_kernel(ck_ref, cv_ref, kn_ref, vn_ref, ko_ref, vo_ref, kb_ref, vb_ref):
    w = ck_ref.shape[1]
    ko_ref[:, 0:w - 1, :] = ck_ref[:, 1:w, :]
    ko_ref[:, w - 1:w, :] = kn_ref[...]
    vo_ref[:, 0:w - 1, :] = cv_ref[:, 1:w, :]
    vo_ref[:, w - 1:w, :] = vn_ref[...]
    kb_ref[...] = ko_ref[...].astype(BF16)
    vb_ref[...] = vo_ref[...].astype(BF16)


def _cache_roll(ck, cv, kn, vn, tb):
    nb, w, n = ck.shape
    blk = pl.BlockSpec((tb, w, n), lambda i: (i, 0, 0))
    new = pl.BlockSpec((tb, 1, n), lambda i: (i, 0, 0))
    return pl.pallas_call(
        _cache_roll_kernel,
        grid=(nb // tb,),
        in_specs=[blk, blk, new, new],
        out_specs=[blk, blk, blk, blk],
        out_shape=[jax.ShapeDtypeStruct(ck.shape, F32), jax.ShapeDtypeStruct(cv.shape, F32),
                   jax.ShapeDtypeStruct(ck.shape, BF16), jax.ShapeDtypeStruct(cv.shape, BF16)],
        compiler_params=_params("arbitrary"),
        name="cache_roll",
    )(ck, cv, kn.reshape(nb, 1, n), vn.reshape(nb, 1, n))


def _attn_sample_kernel(att_layer, tb, q_ref, k_ref, v_ref, sink_ref, o_ref):
    heads, head_dim = q_ref.shape[1], q_ref.shape[2]
    nkv = k_ref.shape[2]
    group = heads // (nkv // head_dim)
    r = lax.broadcasted_iota(jnp.int32, (heads, nkv), 0) // group
    c = lax.broadcasted_iota(jnp.int32, (heads, nkv), 1) // head_dim
    own = r == c
    sink = sink_ref[:, att_layer:att_layer + 1][None]
    q = q_ref[...]
    qe = jnp.where(own[None], jnp.concatenate([q] * (nkv // head_dim), axis=-1), 0.0).astype(BF16)
    s = jnp.einsum("thc,tjc->thj", qe, k_ref[...], preferred_element_type=F32)
    m = jnp.maximum(jnp.max(s, axis=-1, keepdims=True), sink)
    p = jnp.exp(s - m)
    denom = jnp.sum(p, axis=-1, keepdims=True) + jnp.exp(sink - m)
    pv = jnp.where(own[None], jnp.einsum("thj,tjc->thc", p.astype(BF16), v_ref[...],
                                         preferred_element_type=F32), 0.0)
    o = pv[:, :, 0:head_dim]
    for j in range(1, nkv // head_dim):
        o = o + pv[:, :, j * head_dim:(j + 1) * head_dim]
    o_ref[...] = o / denom


def _attn_sample(att_layer, q3, k, v, sinks_t, tb):
    nb, heads, head_dim = q3.shape
    w, nkv = k.shape[1], k.shape[2]
    return pl.pallas_call(
        functools.partial(_attn_sample_kernel, att_layer, tb),
        grid=(nb // tb,),
        in_specs=[pl.BlockSpec((tb, heads, head_dim), lambda i: (i, 0, 0)),
                  pl.BlockSpec((tb, w, nkv), lambda i: (i, 0, 0)),
                  pl.BlockSpec((tb, w, nkv), lambda i: (i, 0, 0)),
                  _full(sinks_t.shape)],
        out_specs=pl.BlockSpec((tb, heads, head_dim), lambda i: (i, 0, 0)),
        out_shape=jax.ShapeDtypeStruct(q3.shape, F32),
        compiler_params=_params("arbitrary"),
        name="attn_sample",
    )(q3, k, v, sinks_t)


def _rope_tables(pos, head_dim):
    half = head_dim // 2
    inv = ROPE_THETA ** (-jnp.arange(half, dtype=F32) / half)
    ang = pos.astype(F32)[:, None] * inv[None, :]
    cos, sin = jnp.cos(ang), jnp.sin(ang)
    reps = LANES // head_dim
    return (jnp.tile(jnp.concatenate([cos, cos], axis=1), (1, reps)),
            jnp.tile(jnp.concatenate([-sin, sin], axis=1), (1, reps)))


def kernel(x_prompt, x_sample, c_prompt, c_sample, state_hgrn, cache_k, cache_v, w_ada, b_ada, norm1_g, norm2_g, hg_w_in, hg_w_out, hg_lower_bounds, hg_gn_g, kv_w_ada, kv_b_ada, kv_norm_g, w_kv, k_norm_g, w_q, q_norm_g, sinks, w_o, w_up, w_down):
    bp, Lp, d = x_prompt.shape
    bs = x_sample.shape[0]
    depth = w_ada.shape[0]
    n_a = hg_w_in.shape[0]
    heads = state_hgrn.shape[2]
    window, kv_heads, head_dim = cache_k.shape[1], cache_k.shape[2], cache_k.shape[3]
    assert LANES == 2 * head_dim and kv_heads % 2 == 0 and Lp % window == 0
    assert window & (window - 1) == 0 and (d // head_dim) // kv_heads == 4
    nkv = kv_heads * head_dim
    reps = LANES // head_dim
    bf = lambda t: t.astype(BF16)

    pad = (-(bs + bp)) % 16
    c_all = jnp.concatenate([c_sample, c_prompt, jnp.zeros((pad, d), F32)], axis=0)
    mods_s, mods_p = _ada(c_all, bs, bp, w_ada, b_ada, T.ada_cols)
    kv_mods_s, kv_mods_p = _ada(c_all, bs, bp, kv_w_ada[None], kv_b_ada[None], T.kv_ada_cols)

    cos_p, sin_p = _rope_tables(jnp.arange(Lp), head_dim)
    cos_s, sin_s = _rope_tables(jnp.full((1,), PAST_LEN), head_dim)
    kg = jnp.tile(k_norm_g, reps)[None]
    qg = jnp.tile(q_norm_g, (1, reps))
    sinks_t = sinks.T
    w_kv_b, w_in_b, w_out_b, w_q_b, w_o_b = bf(w_kv), bf(hg_w_in), bf(hg_w_out), bf(w_q), bf(w_o)
    w_up_b, w_dn_b = bf(w_up), bf(w_down)

    xp = x_prompt
    xs = x_sample.reshape(bs, d)
    hg_p, hg_s = [], None
    k_p = v_p = kx_p = vx_p = k_s = v_s = None
    for l in range(depth):
        if l == n_a:
            k_p, v_p, kx_p, vx_p = _kv(xp, kv_mods_p, kv_norm_g[None], w_kv_b, kg, cos_p, sin_p,
                                       head_dim, window, T.kv_rows)
            k_n, v_n, _, _ = _kv(xs[None], kv_mods_s, kv_norm_g[None], w_kv_b, kg, cos_s, sin_s,
                                 head_dim, bs, bs)
            k_s, v_s, kb_s, vb_s = _cache_roll(cache_k.reshape(bs, window, nkv),
                                               cache_v.reshape(bs, window, nkv), k_n[0], v_n[0], T.cache_tokens)
        if l < n_a:
            w_mix_b, mix_layer = w_out_b, l
            a_p, s_p = _hgrn_prompt(l, xp, mods_p, norm1_g, w_in_b, hg_lower_bounds, hg_gn_g, heads,
                                    T.hgrn_chunk, T.hgrn_rows)
            hg_p.append(s_p)
            proj_s = _proj(l, xs, mods_s, norm1_g, w_in_b, T.proj_cols)
            a_s, hg_s = _hgrn_sample(l, proj_s, hg_lower_bounds, hg_gn_g, state_hgrn, hg_s, SUBLANES)
        else:
            j = l - n_a
            w_mix_b, mix_layer = w_o_b, j
            a_p = _attn_prompt(l, j, xp, mods_p, norm1_g, w_q_b, qg, cos_p, sin_p, kx_p, vx_p, sinks,
                               head_dim, kv_heads, window, T.attn_blocks)
            q_s = _q_sample(l, j, xs, mods_s, norm1_g, w_q_b, qg, cos_s, sin_s, head_dim)
            a_s = _attn_sample(j, q_s.reshape(bs, d // head_dim, head_dim), kb_s, vb_s, sinks_t,
                               T.attn_sample_tokens)
            a_s = a_s.reshape(bs, d)
        xp = _post(l, xp, a_p, mods_p, norm2_g, w_mix_b, mix_layer, w_up_b, w_dn_b, T.mlp_rows)
        xs = _post_sample(l, xs, a_s, mods_s, norm2_g, w_mix_b, mix_layer, w_up_b, w_dn_b, T.mlp_sample_slab)

    shape4 = lambda t: t.reshape(t.shape[0], window, kv_heads, head_dim)
    return (xp, xs.reshape(bs, 1, d), jnp.stack(hg_p), shape4(k_p), shape4(v_p), hg_s,
            shape4(k_s), shape4(v_s))
```

```python
import functools
import math
from typing import NamedTuple

import numpy as np
import jax
import jax.numpy as jnp
from jax import lax
from jax.experimental import pallas as pl
from jax.experimental.pallas import tpu as pltpu

F32 = jnp.float32
BF16 = jnp.bfloat16

PAST_LEN = 8192
ROPE_THETA = 10000.0
EPS = 1e-6
LOG2E = 1.4426950408889634
LANES = 128
SUBLANES = 8
FIRST_TABLE_LEVEL = 2
VMEM_LIMIT = 56 * 1024 * 1024


class _Tiles(NamedTuple):
    ada_cols: int = 3072
    kv_ada_cols: int = 1024
    hgrn_chunk: int = 128
    hgrn_rows: int = 512
    mlp_rows: int = 512
    kv_rows: int = 512
    attn_blocks: int = 8
    proj_cols: int = 1024
    cache_tokens: int = 8
    attn_sample_tokens: int = 16
    mlp_sample_slab: int = 1024


T = _Tiles()

NT_DIMS = (((1,), (1,)), ((), ()))
TN_DIMS = (((0,), (0,)), ((), ()))


def _dot(a, b):
    return jnp.dot(a, b, preferred_element_type=F32)


def _dot_nt(a, b):
    return lax.dot_general(a, b, NT_DIMS, preferred_element_type=F32)


def _sigmoid(x):
    return 1.0 / (1.0 + jnp.exp(-x))


def _silu(x):
    return x * (0.5 + 0.5 * jnp.tanh(0.5 * x))


def _rms(x, g):
    ms = jnp.mean(x * x, axis=-1, keepdims=True)
    return x * lax.rsqrt(ms + EPS) * g


def _modulated_norm(x, gain, shift, scale):
    return _rms(x, gain) * (1.0 + scale) + shift


def _params(*sem):
    return pltpu.CompilerParams(dimension_semantics=sem, vmem_limit_bytes=VMEM_LIMIT)


def _full(shape):
    n = len(shape)
    return pl.BlockSpec(shape, lambda *_: (0,) * n)


def _ada_kernel(c_ref, w_ref, b_ref, os_ref, op_ref):
    bs, bp = os_ref.shape[1], op_ref.shape[1]
    c = c_ref[...]
    a = _silu(c).astype(BF16)
    res = _dot(a, w_ref[0].astype(BF16)) + b_ref[0]
    os_ref[0] = res[0:bs]
    for r in range(bp):
        op_ref[0, r] = jnp.broadcast_to(res[bs + r:bs + r + 1], op_ref.shape[2:])


def _ada(c_all, bs, bp, w, b, tn):
    nl, d, n = w.shape
    r = c_all.shape[0]
    return pl.pallas_call(
        _ada_kernel,
        grid=(nl, n // tn),
        in_specs=[pl.BlockSpec((r, d), lambda l, j: (0, 0)),
                  pl.BlockSpec((1, d, tn), lambda l, j: (l, 0, j)),
                  pl.BlockSpec((1, 1, tn), lambda l, j: (l, 0, j))],
        out_specs=[pl.BlockSpec((1, bs, tn), lambda l, j: (l, 0, j)),
                   pl.BlockSpec((1, bp, SUBLANES, tn), lambda l, j: (l, 0, 0, j))],
        out_shape=[jax.ShapeDtypeStruct((nl, bs, n), F32),
                   jax.ShapeDtypeStruct((nl, bp, SUBLANES, n), F32)],
        compiler_params=_params("arbitrary", "arbitrary"),
        name="ada",
    )(c_all, w, b.reshape(nl, 1, n))


def _mod_rows(mod_ref):
    return mod_ref[0, 0, 0:1, :] if len(mod_ref.shape) == 4 else mod_ref[0]


def _mod_spec(mods, layer):
    if mods.ndim == 4:
        return pl.BlockSpec((1, 1) + mods.shape[2:], lambda i, *_: (layer, i, 0, 0))
    return pl.BlockSpec((1,) + mods.shape[1:], lambda *_: (layer, 0, 0))


def _layer_spec(stack, layer):
    zeros = (0,) * (stack.ndim - 1)
    return pl.BlockSpec((1,) + stack.shape[1:], lambda *_: (layer,) + zeros, pipeline_mode=pl.Buffered(1))


def _group_mean_matrix(group, width):
    r = lax.broadcasted_iota(jnp.int32, (width, width), 0) // group
    c = lax.broadcasted_iota(jnp.int32, (width, width), 1) // group
    return jnp.where(r == c, 1.0 / group, 0.0).astype(BF16)


def _heads_norm_rope(y, gain, cos, sin, head_dim, out_scale):
    rows, n = y.shape
    width = 2 * LANES if n % (2 * LANES) == 0 else LANES
    mean = _group_mean_matrix(head_dim, width)
    gain = gain * out_scale
    half = head_dim // 2
    first = (lax.broadcasted_iota(jnp.int32, (rows, LANES), 1) % head_dim) < half
    out = []
    for j in range(0, n, width):
        ms = _dot(jnp.square(y[:, j:j + width]).astype(BF16), mean)
        for i in range(0, width, LANES):
            yn = y[:, j + i:j + i + LANES] * lax.rsqrt(ms[:, i:i + LANES] + EPS) * gain
            rot = jnp.where(first, pltpu.roll(yn, LANES - half, 1), pltpu.roll(yn, half, 1))
            out.append(yn * cos + rot * sin)
    return out


def _lower_bound(raw, layer):
    m = jnp.max(raw, axis=0, keepdims=True)
    e = jnp.exp(raw - m)
    sm = e / jnp.sum(e, axis=0, keepdims=True)
    acc = sm[0:1]
    for j in range(1, layer + 1):
        acc = acc + sm[j:j + 1]
    return acc - sm[0:1]


def _hgrn_gates(proj, lb, d):
    qa = proj[:, 0:d]
    q = _silu(qa)
    fg = lb + (1.0 - lb) * _sigmoid(proj[:, d:2 * d])
    return q, fg


def _gated_group_norm(o, gain, gate_pre):
    ms = jnp.mean(o * o, axis=-1, keepdims=True)
    return o * lax.rsqrt(ms + EPS) * gain * _silu(gate_pre)


def _hgrn_prompt_kernel(layer, heads, x_ref, mod_ref, n1g_ref, win_ref, lbraw_ref, gng_ref,
                        lev_ref, sums_ref, o_ref, s_ref, st_scr, d_scr, xs_scr, att_scr):
    c = pl.program_id(1)
    total, d = x_ref.shape[1], x_ref.shape[2]
    rows = lev_ref.shape[0]
    dk = d // heads
    n_levels = int(math.log2(rows))

    @pl.when(c == 0)
    def _():
        st_scr[...] = jnp.zeros_like(st_scr)

    mod = _mod_rows(mod_ref)
    h = _modulated_norm(x_ref[0], n1g_ref[layer:layer + 1, :], mod[:, 0:d], mod[:, d:2 * d])
    proj_all = _dot(h.astype(BF16), win_ref[0])
    lb = _lower_bound(lbraw_ref[...], layer)
    q_all, fg_all = _hgrn_gates(proj_all, lb, d)
    lf2_all = jnp.log(fg_all) * LOG2E
    n_table = d_scr.shape[0] // rows - 1 + FIRST_TABLE_LEVEL - 1
    cum_row = d_scr.shape[0] - rows
    lev = lev_ref[...]
    row_id = lax.broadcasted_iota(jnp.int32, (rows, dk), 0)
    upper = [((row_id >> p) & 1) == 1 for p in range(n_table + 1)]
    tile_row = lax.broadcasted_iota(jnp.int32, (rows // SUBLANES, SUBLANES, dk), 1) & 3
    col = lax.broadcasted_iota(jnp.int32, (1, rows), 1)
    gng = gng_ref[layer:layer + 1, :]

    for r0 in range(0, total, rows):
        proj, q, fg = (t[r0:r0 + rows] for t in (proj_all, q_all, fg_all))
        k = 1.0 - fg
        lf2 = lf2_all[r0:r0 + rows]
        hi = lf2.astype(BF16)
        lo = (lf2 - hi.astype(F32)).astype(BF16)
        d_scr[...] = _dot(sums_ref[...], jnp.concatenate([hi, lo], axis=0))

        for hh in range(heads):
            sl = slice(hh * dk, (hh + 1) * dk)
            q_h, k_h = q[:, sl], k[:, sl]
            cum = d_scr[cum_row:cum_row + rows, sl]
            f_h = fg[:, sl]
            xs_scr[hh, 0] = jnp.where(upper[0], q_h * f_h, k_h).astype(BF16)
            f_t = f_h.reshape(rows // SUBLANES, SUBLANES, dk)
            e1 = jnp.where(tile_row == 3, f_t * pltpu.roll(f_t, 1, 1),
                           jnp.where(tile_row == 2, f_t,
                                     jnp.where(tile_row == 1, 1.0, pltpu.roll(f_t, SUBLANES - 1, 1))))
            xs_scr[hh, 1] = (jnp.where(upper[1], q_h, k_h) * e1.reshape(rows, dk)).astype(BF16)
            for p in range(FIRST_TABLE_LEVEL, n_levels):
                m = 1 << p
                if p <= n_table:
                    e = jnp.exp2(d_scr[(p - FIRST_TABLE_LEVEL) * rows:(p - FIRST_TABLE_LEVEL + 1) * rows, sl])
                    xs_scr[hh, p] = (jnp.where(upper[p], q_h, k_h) * e).astype(BF16)
                    continue
                pieces = []
                for r in range(0, rows, m):
                    ref = (r // (2 * m)) * 2 * m + m - 1
                    if (r // m) % 2:
                        pieces.append(q_h[r:r + m] * jnp.exp2(cum[r:r + m] - cum[ref:ref + 1]))
                    else:
                        pieces.append(k_h[r:r + m] * jnp.exp2(cum[ref:ref + 1] - cum[r:r + m]))
                xs_scr[hh, p] = jnp.concatenate(pieces, axis=0).astype(BF16)
            cum_last = cum[rows - 1:rows, :]
            xs_scr[hh, n_levels] = (q_h * jnp.exp2(cum)).astype(BF16)
            xs_scr[hh, n_levels + 1] = (k_h * jnp.exp2(cum_last - cum)).astype(BF16)

        for hh in range(heads):
            xs = xs_scr[hh, 0]
            att = jnp.where(lev == 0, _dot_nt(xs, xs), 0.0)
            for p in range(1, n_levels):
                m = 1 << p
                xs = xs_scr[hh, p]
                pp = _dot_nt(xs, xs)
                if p <= n_table:
                    att = jnp.where(lev == p, pp, att)
                else:
                    att = jnp.concatenate(
                        [jnp.where((col >= r - m) & (col < r), pp[r:r + m], att[r:r + m]) if (r // m) % 2
                         else att[r:r + m] for r in range(0, rows, m)], axis=0)
            att_scr[hh] = att.astype(BF16)

        for hh in range(heads):
            sl = slice(hh * dk, (hh + 1) * dk)
            v_f = proj[:, 2 * d + hh * dk:2 * d + (hh + 1) * dk]
            v_h = v_f.astype(BF16)
            cum_last = d_scr[cum_row + rows - 1:cum_row + rows, sl]
            st = st_scr[hh]
            o_h = (_dot(jnp.concatenate([att_scr[hh], xs_scr[hh, n_levels]], axis=1),
                        jnp.concatenate([v_h, st.T.astype(BF16)], axis=0))
                   + jnp.sum(q[:, sl] * k[:, sl], axis=-1, keepdims=True) * v_f)
            st_scr[hh] = st * jnp.exp2(cum_last) + lax.dot_general(
                v_h, xs_scr[hh, n_levels + 1], TN_DIMS, preferred_element_type=F32)
            g_pre = proj[:, 3 * d + hh * dk:3 * d + (hh + 1) * dk]
            o_ref[0, r0:r0 + rows, sl] = _gated_group_norm(o_h, gng, g_pre)

    @pl.when(c == pl.num_programs(1) - 1)
    def _():
        for hh in range(heads):
            s_ref[0, hh] = st_scr[hh].T


def _level_table(rows):
    t = np.arange(rows)[:, None]
    s = np.arange(rows)[None, :]
    x = t ^ s
    lev = np.where(t > s, np.floor(np.log2(np.maximum(x, 1))).astype(np.int32), np.where(t == s, -1, -2))
    return jnp.asarray(lev, dtype=jnp.int32)


def _sum_table(rows):
    t = np.arange(rows)[:, None]
    j = np.arange(rows)[None, :]
    blocks = []
    for p in range(FIRST_TABLE_LEVEL, int(math.log2(SUBLANES))):
        m = 1 << p
        ref = (t // (2 * m)) * (2 * m) + m - 1
        up = ((t >> p) & 1) == 1
        blocks.append(np.where(up, (j > ref) & (j <= t), (j > t) & (j <= ref)))
    blocks.append(j <= t)
    table = np.concatenate(blocks, axis=0).astype(np.float32)
    return jnp.asarray(np.concatenate([table, table], axis=1), dtype=BF16)


def _hgrn_prompt(layer, x, mods, n1g, w_in, lb_raw, gn_g, heads, chunk, step_rows):
    b, L, d = x.shape
    dk = d // heads
    sums = _sum_table(chunk)
    return pl.pallas_call(
        functools.partial(_hgrn_prompt_kernel, layer, heads),
        grid=(b, L // step_rows),
        in_specs=[pl.BlockSpec((1, step_rows, d), lambda i, c: (i, c, 0)),
                  _mod_spec(mods, layer),
                  _full(n1g.shape), _layer_spec(w_in, layer), _full(lb_raw.shape), _full(gn_g.shape),
                  _full((chunk, chunk)), _full(sums.shape)],
        out_specs=[pl.BlockSpec((1, step_rows, d), lambda i, c: (i, c, 0)),
                   pl.BlockSpec((1, heads, dk, dk), lambda i, c: (i, 0, 0, 0))],
        out_shape=[jax.ShapeDtypeStruct((b, L, d), F32),
                   jax.ShapeDtypeStruct((b, heads, dk, dk), F32)],
        scratch_shapes=[pltpu.VMEM((heads, dk, dk), F32), pltpu.VMEM((sums.shape[0], d), F32),
                        pltpu.VMEM((heads, int(math.log2(chunk)) + 2, chunk, dk), BF16),
                        pltpu.VMEM((heads, chunk, chunk), BF16)],
        compiler_params=_params("arbitrary", "arbitrary"),
        name=f"hgrn_prompt_{layer}",
    )(x, mods, n1g, w_in, lb_raw, gn_g, _level_table(chunk), sums)


def _post_kernel(layer, x_ref, a_ref, mod_ref, n2g_ref, wo_ref, wup_ref, wdn_ref, o_ref):
    d = x_ref.shape[2]
    mod = _mod_rows(mod_ref)
    g1, sh2, sc2, g2 = (mod[:, j * d:(j + 1) * d] for j in range(2, 6))
    x1 = x_ref[0] + g1 * _dot(a_ref[0].astype(BF16), wo_ref[0])
    h2 = _modulated_norm(x1, n2g_ref[layer:layer + 1, :], sh2, sc2).astype(BF16)
    y = None
    for c in range(0, wup_ref.shape[2], d):
        u = jnp.square(jnp.maximum(_dot(h2, wup_ref[0, :, c:c + d]), 0.0)).astype(BF16)
        part = _dot(u, wdn_ref[0, c:c + d, :])
        y = part if y is None else y + part
    o_ref[0] = x1 + g2 * y


def _post_sample_kernel(layer, x_ref, a_ref, mod_ref, n2g_ref, wo_ref, wup_ref, wdn_ref, o_ref,
                        x1_scr, h2_scr, y_scr):
    c = pl.program_id(0)
    d = x_ref.shape[1]
    mod = _mod_rows(mod_ref)

    @pl.when(c == 0)
    def _():
        g1, sh2, sc2 = (mod[:, j * d:(j + 1) * d] for j in range(2, 5))
        x1 = x_ref[...] + g1 * _dot(a_ref[...].astype(BF16), wo_ref[0])
        x1_scr[...] = x1
        h2_scr[...] = _modulated_norm(x1, n2g_ref[layer:layer + 1, :], sh2, sc2).astype(BF16)
        y_scr[...] = jnp.zeros_like(y_scr)

    u = jnp.square(jnp.maximum(_dot(h2_scr[...], wup_ref[0]), 0.0)).astype(BF16)
    y_scr[...] += _dot(u, wdn_ref[0])

    @pl.when(c == pl.num_programs(0) - 1)
    def _():
        o_ref[...] = x1_scr[...] + mod[:, 5 * d:6 * d] * y_scr[...]


def _post_sample(layer, x, a, mods, n2g, w_o, o_layer, w_up, w_dn, slab):
    r, d = x.shape
    ff = w_up.shape[2]
    return pl.pallas_call(
        functools.partial(_post_sample_kernel, layer),
        grid=(ff // slab,),
        in_specs=[_full((r, d)), _full((r, d)), _mod_spec(mods, layer), _full(n2g.shape),
                  _layer_spec(w_o, o_layer),
                  pl.BlockSpec((1, d, slab), lambda c: (layer, 0, c)),
                  pl.BlockSpec((1, slab, d), lambda c: (layer, c, 0))],
        out_specs=_full((r, d)),
        out_shape=jax.ShapeDtypeStruct((r, d), F32),
        scratch_shapes=[pltpu.VMEM((r, d), F32), pltpu.VMEM((r, d), BF16), pltpu.VMEM((r, d), F32)],
        compiler_params=_params("arbitrary"),
        name="post_mlp_sample",
    )(x, a, mods, n2g, w_o, w_up, w_dn)


def _post(layer, x, a, mods, n2g, w_o, o_layer, w_up, w_dn, tm):
    nb, L, d = x.shape
    return pl.pallas_call(
        functools.partial(_post_kernel, layer),
        grid=(nb, L // tm),
        in_specs=[pl.BlockSpec((1, tm, d), lambda i, j: (i, j, 0)),
                  pl.BlockSpec((1, tm, d), lambda i, j: (i, j, 0)),
                  _mod_spec(mods, layer), _full(n2g.shape),
                  _layer_spec(w_o, o_layer), _layer_spec(w_up, layer), _layer_spec(w_dn, layer)],
        out_specs=pl.BlockSpec((1, tm, d), lambda i, j: (i, j, 0)),
        out_shape=jax.ShapeDtypeStruct((nb, L, d), F32),
        compiler_params=_params("arbitrary", "arbitrary"),
        name="post_mlp",
    )(x, a, mods, n2g, w_o, w_up, w_dn)


def _kv_kernel(head_dim, window, x_ref, mod_ref, g_ref, w_ref, kg_ref, cos_ref, sin_ref,
               k_ref, v_ref, kx_ref, vx_ref):
    tm, d = x_ref.shape[1], x_ref.shape[2]
    mod = _mod_rows(mod_ref)
    hk = _modulated_norm(x_ref[0], g_ref[...], mod[:, 0:d], mod[:, d:2 * d])
    y = _dot(hk.astype(BF16), w_ref[...])
    n = y.shape[1] // 2
    cos, sin, kg = cos_ref[...], sin_ref[...], kg_ref[...]
    low = lax.broadcasted_iota(jnp.int32, (tm, LANES), 1) < head_dim
    k_blocks, v_blocks = _heads_norm_rope(y[:, 0:n], kg, cos, sin, head_dim, 1.0), []
    for j in range(n // LANES):
        kb = k_blocks[j]
        vb = y[:, n + j * LANES:n + (j + 1) * LANES]
        v_blocks.append(vb)
        kx_ref[0, :, j * LANES:(j + 1) * LANES] = kb.astype(BF16)
        kx_ref[0, :, n + j * LANES:n + (j + 1) * LANES] = pltpu.roll(kb, head_dim, 1).astype(BF16)
        vs = pltpu.roll(vb, head_dim, 1)
        variants = (jnp.where(low, vb, 0.0), jnp.where(low, 0.0, vs),
                    jnp.where(low, vs, 0.0), jnp.where(low, 0.0, vb))
        for i, var in enumerate(variants):
            off = (4 * j + i) * LANES
            vx_ref[0, off:off + LANES, :] = var.astype(BF16).T

    @pl.when(pl.program_id(1) == pl.num_programs(1) - 1)
    def _():
        for j in range(n // LANES):
            k_ref[0, :, j * LANES:(j + 1) * LANES] = k_blocks[j][tm - window:, :]
            v_ref[0, :, j * LANES:(j + 1) * LANES] = v_blocks[j][tm - window:, :]


def _kv(x, mods, g, w_kv, kg, cos, sin, head_dim, window, tm):
    nb, L, d = x.shape
    n = w_kv.shape[1] // 2
    rope_rows = tm if cos.shape[0] > 1 else 1
    rope_map = (lambda i, j: (j, 0)) if cos.shape[0] > 1 else (lambda i, j: (0, 0))
    last = lambda i, j: (i, 0, 0)
    tile = lambda i, j: (i, j, 0)
    return pl.pallas_call(
        functools.partial(_kv_kernel, head_dim, window),
        grid=(nb, L // tm),
        in_specs=[pl.BlockSpec((1, tm, d), tile), _mod_spec(mods, 0),
                  _full((1, d)), _full(w_kv.shape), _full((1, LANES)),
                  pl.BlockSpec((rope_rows, LANES), rope_map),
                  pl.BlockSpec((rope_rows, LANES), rope_map)],
        out_specs=[pl.BlockSpec((1, window, n), last), pl.BlockSpec((1, window, n), last),
                   pl.BlockSpec((1, tm, 2 * n), tile), pl.BlockSpec((1, 4 * n, tm), lambda i, j: (i, 0, j))],
        out_shape=[jax.ShapeDtypeStruct((nb, window, n), F32), jax.ShapeDtypeStruct((nb, window, n), F32),
                   jax.ShapeDtypeStruct((nb, L, 2 * n), BF16), jax.ShapeDtypeStruct((nb, 4 * n, L), BF16)],
        compiler_params=_params("arbitrary", "arbitrary"),
        name="kv_proj",
    )(x, mods, g, w_kv, kg, cos, sin)


def _q_proj(x, mod, n1g, wq, qg, cos, sin, head_dim, extra_scale=1.0):
    d = x.shape[1]
    h = _modulated_norm(x, n1g, mod[:, 0:d], mod[:, d:2 * d])
    y = _dot(h.astype(BF16), wq)
    return _heads_norm_rope(y, qg, cos, sin, head_dim, extra_scale / math.sqrt(head_dim))


def _attn_prompt_kernel(layer, att_layer, head_dim, kv_heads, nq, x_ref, mod_ref, n1g_ref, wq_ref, qg_ref,
                        cos_ref, sin_ref, *rest):
    k_refs, v_refs = rest[0:nq + 1], rest[nq + 1:2 * nq + 2]
    sink_ref, o_ref, s_scr, p_scr = rest[2 * nq + 2:]
    w = x_ref.shape[1] // nq
    nkv = kv_heads * head_dim
    q_blocks = _q_proj(x_ref[0], _mod_rows(mod_ref), n1g_ref[layer:layer + 1, :], wq_ref[0],
                       qg_ref[att_layer:att_layer + 1, :], cos_ref[...], sin_ref[...], head_dim, LOG2E)
    group = (2 * len(q_blocks)) // kv_heads
    key = lax.broadcasted_iota(jnp.int32, (w, 2 * w), 0)
    qry = lax.broadcasted_iota(jnp.int32, (w, 2 * w), 1)
    own = key <= (qry & (w - 1))
    first = lax.broadcasted_iota(jnp.int32, (1, 2 * w), 1) < w
    low = lax.broadcasted_iota(jnp.int32, (w, LANES), 1) < head_dim
    no_prev = jnp.where(pl.program_id(1) == 0, -jnp.inf, 0.0)
    tiles = [(u, kh, half) for u in range(nq) for kh in range(kv_heads) for half in range(2)]
    for n, (u, kh, half) in enumerate(tiles):
        ha, hb = kh * group + half, kh * group + 2 + half
        mask = (lambda t: jnp.where(low, t, 0.0)) if half == 0 else (lambda t: jnp.where(low, 0.0, t))
        rows = slice(u * w, (u + 1) * w)
        qcat = jnp.concatenate([mask(q_blocks[ha // 2][rows]), mask(q_blocks[hb // 2][rows])],
                               axis=0).astype(BF16)
        koff = (kh // 2) * LANES + (0 if kh % 2 == half else nkv)
        kcat = jnp.concatenate([k_refs[u][0, :, koff:koff + LANES], k_refs[u + 1][0, :, koff:koff + LANES]],
                               axis=0)
        s_t = _dot_nt(kcat, qcat)
        s_prev = s_t[:w] + no_prev if u == 0 else s_t[:w]
        s_scr[n] = jnp.where(own, s_t[w:], s_prev)
    for n, (u, kh, half) in enumerate(tiles):
        sc = s_scr[n]
        sink = jnp.where(first, sink_ref[att_layer, kh * group + half],
                         sink_ref[att_layer, kh * group + 2 + half]) * LOG2E
        m = jnp.maximum(jnp.max(sc, axis=0, keepdims=True), sink)
        p = jnp.exp2(sc - m)
        pn = p * (1.0 / (jnp.sum(p, axis=0, keepdims=True) + jnp.exp2(sink - m)))
        p_scr[n, 0:w] = jnp.where(own, pn, 0.0).astype(BF16)
        p_scr[n, w:2 * w] = jnp.where(own, 0.0, pn).astype(BF16)
    for u in range(nq):
        for kh in range(kv_heads):
            out_t = None
            for half in range(2):
                voff = (2 * kh + half) * LANES
                v_t = jnp.concatenate([v_refs[u + 1][0, voff:voff + LANES, :],
                                       v_refs[u][0, voff:voff + LANES, :]], axis=1)
                part = _dot(v_t, p_scr[(u * kv_heads + kh) * 2 + half])
                out_t = part if out_t is None else out_t + part
            for i in range(2):
                o_ref[0, u * w:(u + 1) * w, (2 * kh + i) * LANES:(2 * kh + i + 1) * LANES] = (
                    out_t[:, i * w:(i + 1) * w].T)


def _attn_prompt(layer, att_layer, x, mods, n1g, wq, qg, cos, sin, kx, vx, sinks, head_dim, kv_heads, w, nq):
    b, L, d = x.shape
    rows = lambda i, j: (i, j, 0)
    k_specs = [pl.BlockSpec((1, w, kx.shape[2]), lambda i, j, o=o: (i, jnp.maximum(nq * j + o, 0), 0))
               for o in range(-1, nq)]
    v_specs = [pl.BlockSpec((1, vx.shape[1], w), lambda i, j, o=o: (i, 0, jnp.maximum(nq * j + o, 0)))
               for o in range(-1, nq)]
    n_tiles = nq * 2 * kv_heads
    return pl.pallas_call(
        functools.partial(_attn_prompt_kernel, layer, att_layer, head_dim, kv_heads, nq),
        grid=(b, L // (nq * w)),
        in_specs=[pl.BlockSpec((1, nq * w, d), rows), _mod_spec(mods, layer),
                  _full(n1g.shape), _layer_spec(wq, att_layer), _full(qg.shape),
                  pl.BlockSpec((nq * w, LANES), lambda i, j: (j, 0)),
                  pl.BlockSpec((nq * w, LANES), lambda i, j: (j, 0))]
                 + k_specs + v_specs + [pl.BlockSpec(memory_space=pltpu.SMEM)],
        out_specs=pl.BlockSpec((1, nq * w, d), rows),
        out_shape=jax.ShapeDtypeStruct((b, L, d), F32),
        scratch_shapes=[pltpu.VMEM((n_tiles, w, 2 * w), F32), pltpu.VMEM((n_tiles, 2 * w, 2 * w), BF16)],
        compiler_params=_params("arbitrary", "arbitrary"),
        name="attn_prompt",
    )(x, mods, n1g, wq, qg, cos, sin, *([kx] * (nq + 1)), *([vx] * (nq + 1)), sinks)


def _proj_kernel(layer, x_ref, mod_ref, g_ref, w_ref, o_ref):
    d = x_ref.shape[1]
    mod = _mod_rows(mod_ref)
    h = _modulated_norm(x_ref[...], g_ref[layer:layer + 1, :], mod[:, 0:d], mod[:, d:2 * d])
    o_ref[...] = _dot(h.astype(BF16), w_ref[0])


def _proj(layer, x, mods, g, w, tn):
    r, d = x.shape
    n = w.shape[2]
    return pl.pallas_call(
        functools.partial(_proj_kernel, layer),
        grid=(n // tn,),
        in_specs=[_full((r, d)), _mod_spec(mods, layer), _full(g.shape),
                  pl.BlockSpec((1, d, tn), lambda j: (layer, 0, j))],
        out_specs=pl.BlockSpec((r, tn), lambda j: (0, j)),
        out_shape=jax.ShapeDtypeStruct((r, n), F32),
        compiler_params=_params("arbitrary"),
        name="proj_sample",
    )(x, mods, g, w)


def _q_sample_kernel(layer, att_layer, head_dim, x_ref, mod_ref, n1g_ref, wq_ref, qg_ref, cos_ref, sin_ref,
                     o_ref):
    blocks = _q_proj(x_ref[...], _mod_rows(mod_ref), n1g_ref[layer:layer + 1, :], wq_ref[0],
                     qg_ref[att_layer:att_layer + 1, :], cos_ref[...], sin_ref[...], head_dim)
    for j, blk in enumerate(blocks):
        o_ref[:, j * LANES:(j + 1) * LANES] = blk


def _q_sample(layer, att_layer, x, mods, n1g, wq, qg, cos, sin, head_dim):
    r, d = x.shape
    return pl.pallas_call(
        functools.partial(_q_sample_kernel, layer, att_layer, head_dim),
        grid=(1,),
        in_specs=[_full((r, d)), _mod_spec(mods, layer), _full(n1g.shape), _layer_spec(wq, att_layer),
                  _full(qg.shape), _full((1, LANES)), _full((1, LANES))],
        out_specs=_full((r, wq.shape[2])),
        out_shape=jax.ShapeDtypeStruct((r, wq.shape[2]), F32),
        compiler_params=_params("arbitrary"),
        name="q_sample",
    )(x, mods, n1g, wq, qg, cos, sin)


def _hgrn_sample_kernel(layer, heads, tb, steps, aliased, proj_ref, lbraw_ref, gng_ref, s_ref, *rest):
    o_ref, so_ref, stack_scr, acc_scr = rest[1:] if aliased else rest
    i = pl.program_id(0)
    rows = proj_ref.shape[0]
    d = proj_ref.shape[1] // 4
    dk = d // heads

    def update():
        @pl.when(i == 0)
        def _():
            lb = _lower_bound(lbraw_ref[...], layer)
            q, fg = _hgrn_gates(proj_ref[:, 0:2 * d], lb, d)
            for hh in range(heads):
                sl = slice(hh * dk, (hh + 1) * dk)
                ft = fg[:, sl].T
                hi = ft.astype(BF16)
                stack_scr[hh, 0:dk] = hi
                stack_scr[hh, dk:2 * dk] = (ft - hi.astype(F32)).astype(BF16)
                stack_scr[hh, 2 * dk:3 * dk] = (1.0 - ft).astype(BF16)
                stack_scr[hh, 3 * dk:4 * dk] = q[:, sl].T.astype(BF16)

        token_row = lax.broadcasted_iota(jnp.int32, (rows, dk), 0)
        sub = lax.broadcasted_iota(jnp.int32, (tb, dk), 0)
        base = pl.multiple_of(i * tb, tb)
        v_rows = proj_ref[pl.ds(base, tb), 2 * d:3 * d]
        o_rows = [jnp.zeros((tb, dk), F32) for _ in range(heads)]
        for t0 in range(0, tb, 2):
            onehot = jnp.concatenate([jnp.where(token_row == base + t, 1.0, 0.0) for t in (t0, t0 + 1)],
                                     axis=1).astype(BF16)
            for hh in range(heads):
                bc2 = _dot(stack_scr[hh], onehot)
                for t in (t0, t0 + 1):
                    bc = bc2[:, (t - t0) * dk:(t - t0 + 1) * dk]
                    f_b = bc[0:dk] + bc[dk:2 * dk]
                    k_b = bc[2 * dk:3 * dk]
                    q_b = bc[3 * dk:4 * dk]
                    s_new = f_b * s_ref[0, t, hh] + k_b * v_rows[t:t + 1, hh * dk:(hh + 1) * dk]
                    so_ref[0, t, hh] = s_new
                    o_rows[hh] = jnp.where(sub == t, jnp.sum(q_b * s_new, axis=0, keepdims=True), o_rows[hh])
        for hh in range(heads):
            acc_scr[pl.ds(base, tb), hh * dk:(hh + 1) * dk] = o_rows[hh]

        @pl.when(i == steps - 1)
        def _():
            gng = gng_ref[layer:layer + 1, :]
            for hh in range(heads):
                sl = slice(hh * dk, (hh + 1) * dk)
                o_ref[:, sl] = _gated_group_norm(acc_scr[:, sl], gng,
                                                 proj_ref[:, 3 * d + hh * dk:3 * d + (hh + 1) * dk])

    if aliased:
        update()
    else:
        pl.when(i < steps)(update)

        @pl.when(i >= steps)
        def _():
            so_ref[...] = jnp.zeros_like(so_ref)


def _hgrn_sample(layer, proj, lb_raw, gn_g, state, new_state, tb):
    n_layers, nb, heads, dk, dv = state.shape
    d = proj.shape[1] // 4
    steps = nb // tb
    aliased = new_state is not None
    assert aliased == (layer > 0) and tb == SUBLANES
    if aliased:
        grid = (steps,)
        s_in = s_out = pl.BlockSpec((1, tb, heads, dk, dv), lambda i: (layer, i, 0, 0, 0))
    else:
        grid = (n_layers * steps,)
        s_in = pl.BlockSpec((1, tb, heads, dk, dv), lambda i: (0, jnp.minimum(i, steps - 1), 0, 0, 0))
        s_out = pl.BlockSpec((1, tb, heads, dk, dv), lambda i: (i // steps, i % steps, 0, 0, 0))
    in_specs = [_full(proj.shape), _full(lb_raw.shape), _full(gn_g.shape), s_in]
    args = [proj, lb_raw, gn_g, state]
    if aliased:
        in_specs.append(pl.BlockSpec(memory_space=pl.ANY))
        args.append(new_state)
    return pl.pallas_call(
        functools.partial(_hgrn_sample_kernel, layer, heads, tb, steps, aliased),
        grid=grid,
        in_specs=in_specs,
        out_specs=[_full((nb, d)), s_out],
        out_shape=[jax.ShapeDtypeStruct((nb, d), F32), jax.ShapeDtypeStruct(state.shape, F32)],
        scratch_shapes=[pltpu.VMEM((heads, 4 * dk, nb), BF16), pltpu.VMEM((nb, d), F32)],
        input_output_aliases={4: 1} if aliased else {},
        compiler_params=_params("arbitrary"),
        name=f"hgrn_sample_{layer}",
    )(*args)


def _cache_roll_kernel(ck_ref, cv_ref, kn_ref, vn_ref, ko_ref, vo_ref, kb_ref, vb_ref):
    w = ck_ref.shape[1]
    ko_ref[:, 0:w - 1, :] = ck_ref[:, 1:w, :]
    ko_ref[:, w - 1:w, :] = kn_ref[...]
    vo_ref[:, 0:w - 1, :] = cv_ref[:, 1:w, :]
    vo_ref[:, w - 1:w, :] = vn_ref[...]
    kb_ref[...] = ko_ref[...].astype(BF16)
    vb_ref[...] = vo_ref[...].astype(BF16)


def _cache_roll(ck, cv, kn, vn, tb):
    nb, w, n = ck.shape
    blk = pl.BlockSpec((tb, w, n), lambda i: (i, 0, 0))
    new = pl.BlockSpec((tb, 1, n), lambda i: (i, 0, 0))
    return pl.pallas_call(
        _cache_roll_kernel,
        grid=(nb // tb,),
        in_specs=[blk, blk, new, new],
        out_specs=[blk, blk, blk, blk],
        out_shape=[jax.ShapeDtypeStruct(ck.shape, F32), jax.ShapeDtypeStruct(cv.shape, F32),
                   jax.ShapeDtypeStruct(ck.shape, BF16), jax.ShapeDtypeStruct(cv.shape, BF16)],
        compiler_params=_params("arbitrary"),
        name="cache_roll",
    )(ck, cv, kn.reshape(nb, 1, n), vn.reshape(nb, 1, n))


def _attn_sample_kernel(att_layer, tb, q_ref, k_ref, v_ref, sink_ref, o_ref):
    heads, head_dim = q_ref.shape[1], q_ref.shape[2]
    nkv = k_ref.shape[2]
    group = heads // (nkv // head_dim)
    r = lax.broadcasted_iota(jnp.int32, (heads, nkv), 0) // group
    c = lax.broadcasted_iota(jnp.int32, (heads, nkv), 1) // head_dim
    own = r == c
    sink = sink_ref[:, att_layer:att_layer + 1][None]
    q = q_ref[...]
    qe = jnp.where(own[None], jnp.concatenate([q] * (nkv // head_dim), axis=-1), 0.0).astype(BF16)
    s = jnp.einsum("thc,tjc->thj", qe, k_ref[...], preferred_element_type=F32)
    m = jnp.maximum(jnp.max(s, axis=-1, keepdims=True), sink)
    p = jnp.exp(s - m)
    denom = jnp.sum(p, axis=-1, keepdims=True) + jnp.exp(sink - m)
    pv = jnp.where(own[None], jnp.einsum("thj,tjc->thc", p.astype(BF16), v_ref[...],
                                         preferred_element_type=F32), 0.0)
    o = pv[:, :, 0:head_dim]
    for j in range(1, nkv // head_dim):
        o = o + pv[:, :, j * head_dim:(j + 1) * head_dim]
    o_ref[...] = o / denom


def _attn_sample(att_layer, q3, k, v, sinks_t, tb):
    nb, heads, head_dim = q3.shape
    w, nkv = k.shape[1], k.shape[2]
    return pl.pallas_call(
        functools.partial(_attn_sample_kernel, att_layer, tb),
        grid=(nb // tb,),
        in_specs=[pl.BlockSpec((tb, heads, head_dim), lambda i: (i, 0, 0)),
                  pl.BlockSpec((tb, w, nkv), lambda i: (i, 0, 0)),
                  pl.BlockSpec((tb, w, nkv), lambda i: (i, 0, 0)),
                  _full(sinks_t.shape)],
        out_specs=pl.BlockSpec((tb, heads, head_dim), lambda i: (i, 0, 0)),
        out_shape=jax.ShapeDtypeStruct(q3.shape, F32),
        compiler_params=_params("arbitrary"),
        name="attn_sample",
    )(q3, k, v, sinks_t)


def _rope_tables(pos, head_dim):
    half = head_dim // 2
    inv = ROPE_THETA ** (-jnp.arange(half, dtype=F32) / half)
    ang = pos.astype(F32)[:, None] * inv[None, :]
    cos, sin = jnp.cos(ang), jnp.sin(ang)
    reps = LANES // head_dim
    return (jnp.tile(jnp.concatenate([cos, cos], axis=1), (1, reps)),
            jnp.tile(jnp.concatenate([-sin, sin], axis=1), (1, reps)))


def kernel(x_prompt, x_sample, c_prompt, c_sample, state_hgrn, cache_k, cache_v, w_ada, b_ada, norm1_g, norm2_g, hg_w_in, hg_w_out, hg_lower_bounds, hg_gn_g, kv_w_ada, kv_b_ada, kv_norm_g, w_kv, k_norm_g, w_q, q_norm_g, sinks, w_o, w_up, w_down):
    bp, Lp, d = x_prompt.shape
    bs = x_sample.shape[0]
    depth = w_ada.shape[0]
    n_a = hg_w_in.shape[0]
    heads = state_hgrn.shape[2]
    window, kv_heads, head_dim = cache_k.shape[1], cache_k.shape[2], cache_k.shape[3]
    assert LANES == 2 * head_dim and kv_heads % 2 == 0 and Lp % window == 0
    assert window & (window - 1) == 0 and (d // head_dim) // kv_heads == 4
    nkv = kv_heads * head_dim
    reps = LANES // head_dim
    bf = lambda t: t.astype(BF16)

    pad = (-(bs + bp)) % 16
    c_all = jnp.concatenate([c_sample, c_prompt, jnp.zeros((pad, d), F32)], axis=0)
    mods_s, mods_p = _ada(c_all, bs, bp, w_ada, b_ada, T.ada_cols)
    kv_mods_s, kv_mods_p = _ada(c_all, bs, bp, kv_w_ada[None], kv_b_ada[None], T.kv_ada_cols)

    cos_p, sin_p = _rope_tables(jnp.arange(Lp), head_dim)
    cos_s, sin_s = _rope_tables(jnp.full((1,), PAST_LEN), head_dim)
    kg = jnp.tile(k_norm_g, reps)[None]
    qg = jnp.tile(q_norm_g, (1, reps))
    sinks_t = sinks.T
    w_kv_b, w_in_b, w_out_b, w_q_b, w_o_b = bf(w_kv), bf(hg_w_in), bf(hg_w_out), bf(w_q), bf(w_o)
    w_up_b, w_dn_b = bf(w_up), bf(w_down)

    xp = x_prompt
    xs = x_sample.reshape(bs, d)
    hg_p, hg_s = [], None
    k_p = v_p = kx_p = vx_p = k_s = v_s = None
    for l in range(depth):
        if l == n_a:
            k_p, v_p, kx_p, vx_p = _kv(xp, kv_mods_p, kv_norm_g[None], w_kv_b, kg, cos_p, sin_p,
                                       head_dim, window, T.kv_rows)
            k_n, v_n, _, _ = _kv(xs[None], kv_mods_s, kv_norm_g[None], w_kv_b, kg, cos_s, sin_s,
                                 head_dim, bs, bs)
            k_s, v_s, kb_s, vb_s = _cache_roll(cache_k.reshape(bs, window, nkv),
                                               cache_v.reshape(bs, window, nkv), k_n[0], v_n[0], T.cache_tokens)
        if l < n_a:
            w_mix_b, mix_layer = w_out_b, l
            a_p, s_p = _hgrn_prompt(l, xp, mods_p, norm1_g, w_in_b, hg_lower_bounds, hg_gn_g, heads,
                                    T.hgrn_chunk, T.hgrn_rows)
            hg_p.append(s_p)
            proj_s = _proj(l, xs, mods_s, norm1_g, w_in_b, T.proj_cols)
            a_s, hg_s = _hgrn_sample(l, proj_s, hg_lower_bounds, hg_gn_g, state_hgrn, hg_s, SUBLANES)
        else:
            j = l - n_a
            w_mix_b, mix_layer = w_o_b, j
            a_p = _attn_prompt(l, j, xp, mods_p, norm1_g, w_q_b, qg, cos_p, sin_p, kx_p, vx_p, sinks,
                               head_dim, kv_heads, window, T.attn_blocks)
            q_s = _q_sample(l, j, xs, mods_s, norm1_g, w_q_b, qg, cos_s, sin_s, head_dim)
            a_s = _attn_sample(j, q_s.reshape(bs, d // head_dim, head_dim), kb_s, vb_s, sinks_t,
                               T.attn_sample_tokens)
            a_s = a_s.reshape(bs, d)
        xp = _post(l, xp, a_p, mods_p, norm2_g, w_mix_b, mix_layer, w_up_b, w_dn_b, T.mlp_rows)
        xs = _post_sample(l, xs, a_s, mods_s, norm2_g, w_mix_b, mix_layer, w_up_b, w_dn_b, T.mlp_sample_slab)

    shape4 = lambda t: t.reshape(t.shape[0], window, kv_heads, head_dim)
    return (xp, xs.reshape(bs, 1, d), jnp.stack(hg_p), shape4(k_p), shape4(v_p), hg_s,
            shape4(k_s), shape4(v_s))
```

```python
import functools
import math
from typing import NamedTuple

import numpy as np
import jax
import jax.numpy as jnp
from jax import lax
from jax.experimental import pallas as pl
from jax.experimental.pallas import tpu as pltpu

F32 = jnp.float32
BF16 = jnp.bfloat16

PAST_LEN = 8192
ROPE_THETA = 10000.0
EPS = 1e-6
LOG2E = 1.4426950408889634
LANES = 128
SUBLANES = 8
FIRST_TABLE_LEVEL = 3
assert (1 << FIRST_TABLE_LEVEL) == SUBLANES
VMEM_LIMIT = 56 * 1024 * 1024


class _Tiles(NamedTuple):
    ada_cols: int = 3072
    kv_ada_cols: int = 1024
    hgrn_chunk: int = 128
    hgrn_rows: int = 512
    mlp_rows: int = 512
    kv_rows: int = 512
    attn_blocks: int = 8
    proj_cols: int = 1024
    cache_tokens: int = 8
    attn_sample_tokens: int = 16
    mlp_sample_slab: int = 1024


T = _Tiles()

NT_DIMS = (((1,), (1,)), ((), ()))
TN_DIMS = (((0,), (0,)), ((), ()))


def _dot(a, b):
    return jnp.dot(a, b, preferred_element_type=F32)


def _dot_nt(a, b):
    return lax.dot_general(a, b, NT_DIMS, preferred_element_type=F32)


def _sigmoid(x):
    return 1.0 / (1.0 + jnp.exp(-x))


def _silu(x):
    return x * (0.5 + 0.5 * jnp.tanh(0.5 * x))


def _rms(x, g):
    ms = jnp.mean(x * x, axis=-1, keepdims=True)
    return x * lax.rsqrt(ms + EPS) * g


def _modulated_norm(x, gain, shift, scale):
    return _rms(x, gain) * (1.0 + scale) + shift


def _params(*sem):
    return pltpu.CompilerParams(dimension_semantics=sem, vmem_limit_bytes=VMEM_LIMIT)


def _full(shape):
    n = len(shape)
    return pl.BlockSpec(shape, lambda *_: (0,) * n)


def _ada_kernel(c_ref, w_ref, b_ref, os_ref, op_ref):
    bs, bp = os_ref.shape[1], op_ref.shape[1]
    c = c_ref[...]
    a = _silu(c).astype(BF16)
    res = _dot(a, w_ref[0].astype(BF16)) + b_ref[0]
    os_ref[0] = res[0:bs]
    for r in range(bp):
        op_ref[0, r] = jnp.broadcast_to(res[bs + r:bs + r + 1], op_ref.shape[2:])


def _ada(c_all, bs, bp, w, b, tn):
    nl, d, n = w.shape
    r = c_all.shape[0]
    return pl.pallas_call(
        _ada_kernel,
        grid=(nl, n // tn),
        in_specs=[pl.BlockSpec((r, d), lambda l, j: (0, 0)),
                  pl.BlockSpec((1, d, tn), lambda l, j: (l, 0, j)),
                  pl.BlockSpec((1, 1, tn), lambda l, j: (l, 0, j))],
        out_specs=[pl.BlockSpec((1, bs, tn), lambda l, j: (l, 0, j)),
                   pl.BlockSpec((1, bp, SUBLANES, tn), lambda l, j: (l, 0, 0, j))],
        out_shape=[jax.ShapeDtypeStruct((nl, bs, n), F32),
                   jax.ShapeDtypeStruct((nl, bp, SUBLANES, n), F32)],
        compiler_params=_params("arbitrary", "arbitrary"),
        name="ada",
    )(c_all, w, b.reshape(nl, 1, n))


def _mod_rows(mod_ref):
    return mod_ref[0, 0, 0:1, :] if len(mod_ref.shape) == 4 else mod_ref[0]


def _mod_spec(mods, layer):
    if mods.ndim == 4:
        return pl.BlockSpec((1, 1) + mods.shape[2:], lambda i, *_: (layer, i, 0, 0))
    return pl.BlockSpec((1,) + mods.shape[1:], lambda *_: (layer, 0, 0))


def _layer_spec(stack, layer):
    zeros = (0,) * (stack.ndim - 1)
    return pl.BlockSpec((1,) + stack.shape[1:], lambda *_: (layer,) + zeros, pipeline_mode=pl.Buffered(1))


def _group_mean_matrix(group, width):
    r = lax.broadcasted_iota(jnp.int32, (width, width), 0) // group
    c = lax.broadcasted_iota(jnp.int32, (width, width), 1) // group
    return jnp.where(r == c, 1.0 / group, 0.0).astype(BF16)


def _heads_norm_rope(y, gain, cos, sin, head_dim, out_scale):
    rows, n = y.shape
    width = 2 * LANES if n % (2 * LANES) == 0 else LANES
    mean = _group_mean_matrix(head_dim, width)
    gain = gain * out_scale
    half = head_dim // 2
    first = (lax.broadcasted_iota(jnp.int32, (rows, LANES), 1) % head_dim) < half
    out = []
    for j in range(0, n, width):
        ms = _dot(jnp.square(y[:, j:j + width]).astype(BF16), mean)
        for i in range(0, width, LANES):
            yn = y[:, j + i:j + i + LANES] * lax.rsqrt(ms[:, i:i + LANES] + EPS) * gain
            rot = jnp.where(first, pltpu.roll(yn, LANES - half, 1), pltpu.roll(yn, half, 1))
            out.append(yn * cos + rot * sin)
    return out


def _lower_bound(raw, layer):
    m = jnp.max(raw, axis=0, keepdims=True)
    e = jnp.exp(raw - m)
    sm = e / jnp.sum(e, axis=0, keepdims=True)
    acc = sm[0:1]
    for j in range(1, layer + 1):
        acc = acc + sm[j:j + 1]
    return acc - sm[0:1]


def _hgrn_gates(proj, lb, d):
    qa = proj[:, 0:d]
    q = _silu(qa)
    fg = lb + (1.0 - lb) * _sigmoid(proj[:, d:2 * d])
    return q, fg


def _gated_group_norm(o, gain, gate_pre):
    ms = jnp.mean(o * o, axis=-1, keepdims=True)
    return o * lax.rsqrt(ms + EPS) * gain * _silu(gate_pre)


def _hgrn_prompt_kernel(layer, heads, x_ref, mod_ref, n1g_ref, win_ref, lbraw_ref, gng_ref,
                        lev_ref, sums_ref, o_ref, s_ref, st_scr, d_scr, xs_scr, att_scr):
    c = pl.program_id(1)
    total, d = x_ref.shape[1], x_ref.shape[2]
    rows = lev_ref.shape[0]
    dk = d // heads
    n_levels = int(math.log2(rows))

    @pl.when(c == 0)
    def _():
        st_scr[...] = jnp.zeros_like(st_scr)

    mod = _mod_rows(mod_ref)
    h = _modulated_norm(x_ref[0], n1g_ref[layer:layer + 1, :], mod[:, 0:d], mod[:, d:2 * d])
    proj_all = _dot(h.astype(BF16), win_ref[0])
    lb = _lower_bound(lbraw_ref[...], layer)
    q_all, fg_all = _hgrn_gates(proj_all, lb, d)
    lf2_all = jnp.log(fg_all) * LOG2E
    n_table = d_scr.shape[0] // rows - 1 + FIRST_TABLE_LEVEL - 1
    cum_row = d_scr.shape[0] - rows
    lev = lev_ref[...]
    row_id = lax.broadcasted_iota(jnp.int32, (rows, dk), 0)
    upper = [((row_id >> p) & 1) == 1 for p in range(n_table + 1)]
    tile_row8 = lax.broadcasted_iota(jnp.int32, (rows // SUBLANES, SUBLANES, dk), 1)
    tile_row = tile_row8 & 3
    col = lax.broadcasted_iota(jnp.int32, (1, rows), 1)
    gng = gng_ref[layer:layer + 1, :]

    for r0 in range(0, total, rows):
        proj, q, fg = (t[r0:r0 + rows] for t in (proj_all, q_all, fg_all))
        k = 1.0 - fg
        lf2 = lf2_all[r0:r0 + rows]
        hi = lf2.astype(BF16)
        lo = (lf2 - hi.astype(F32)).astype(BF16)
        d_scr[...] = _dot(sums_ref[...], jnp.concatenate([hi, lo], axis=0))

        for hh in range(heads):
            sl = slice(hh * dk, (hh + 1) * dk)
            q_h, k_h = q[:, sl], k[:, sl]
            cum = d_scr[cum_row:cum_row + rows, sl]
            f_h = fg[:, sl]
            xs_scr[hh, 0] = jnp.where(upper[0], q_h * f_h, k_h).astype(BF16)
            f_t = f_h.reshape(rows // SUBLANES, SUBLANES, dk)
            e1 = jnp.where(tile_row == 3, f_t * pltpu.roll(f_t, 1, 1),
                           jnp.where(tile_row == 2, f_t,
                                     jnp.where(tile_row == 1, 1.0, pltpu.roll(f_t, SUBLANES - 1, 1))))
            xs_scr[hh, 1] = (jnp.where(upper[1], q_h, k_h) * e1.reshape(rows, dk)).astype(BF16)
            p2 = f_t * pltpu.roll(f_t, 1, 1)
            nxt = pltpu.roll(f_t, SUBLANES - 1, 1)
            e2 = jnp.where(tile_row8 == 7, p2 * pltpu.roll(p2, 2, 1),
                 jnp.where(tile_row8 == 6, f_t * pltpu.roll(p2, 1, 1),
                 jnp.where(tile_row8 == 5, p2,
                 jnp.where(tile_row8 == 4, f_t,
                 jnp.where(tile_row8 == 3, 1.0,
                 jnp.where(tile_row8 == 2, nxt,
                 jnp.where(tile_row8 == 1, pltpu.roll(p2, SUBLANES - 2, 1),
                           nxt * pltpu.roll(p2, SUBLANES - 3, 1))))))))
            xs_scr[hh, 2] = (jnp.where(upper[2], q_h, k_h) * e2.reshape(rows, dk)).astype(BF16)
            cum_last = cum[rows - 1:rows, :]
            edge = 2 << n_table
            q_dec = [q_h[0:edge] * jnp.exp2(cum[0:edge])]
            k_dec = [k_h[rows - edge:rows] * jnp.exp2(cum_last - cum[rows - edge:rows])]
            for p in range(FIRST_TABLE_LEVEL, n_levels):
                m = 1 << p
                if p <= n_table:
                    e = jnp.exp2(d_scr[(p - FIRST_TABLE_LEVEL) * rows:(p - FIRST_TABLE_LEVEL + 1) * rows, sl])
                    xs_scr[hh, p] = (jnp.where(upper[p], q_h, k_h) * e).astype(BF16)
                    continue
                pieces = []
                for r in range(0, rows, m):
                    ref = (r // (2 * m)) * 2 * m + m - 1
                    if (r // m) % 2:
                        pieces.append(q_h[r:r + m] * jnp.exp2(cum[r:r + m] - cum[ref:ref + 1]))
                    else:
                        pieces.append(k_h[r:r + m] * jnp.exp2(cum[ref:ref + 1] - cum[r:r + m]))
                xs_scr[hh, p] = jnp.concatenate(pieces, axis=0).astype(BF16)
                q_dec.append(pieces[1] * jnp.exp2(cum[m - 1:m]))
                k_dec.insert(0, pieces[rows // m - 2] * jnp.exp2(cum_last - cum[rows - m - 1:rows - m]))
            xs_scr[hh, n_levels] = jnp.concatenate(q_dec, axis=0).astype(BF16)
            xs_scr[hh, n_levels + 1] = jnp.concatenate(k_dec, axis=0).astype(BF16)

        for hh in range(heads):
            xs = xs_scr[hh, 0]
            att = jnp.where(lev == 0, _dot_nt(xs, xs), 0.0)
            for p in range(1, n_levels):
                m = 1 << p
                xs = xs_scr[hh, p]
                pp = _dot_nt(xs, xs)
                if p <= n_table:
                    att = jnp.where(lev == p, pp, att)
                else:
                    att = jnp.concatenate(
                        [jnp.where((col >= r - m) & (col < r), pp[r:r + m], att[r:r + m]) if (r // m) % 2
                         else att[r:r + m] for r in range(0, rows, m)], axis=0)
            att_scr[hh] = att.astype(BF16)

        for hh in range(heads):
            sl = slice(hh * dk, (hh + 1) * dk)
            v_f = proj[:, 2 * d + hh * dk:2 * d + (hh + 1) * dk]
            v_h = v_f.astype(BF16)
            cum_last = d_scr[cum_row + rows - 1:cum_row + rows, sl]
            st = st_scr[hh]
            o_h = (_dot(jnp.concatenate([att_scr[hh], xs_scr[hh, n_levels]], axis=1),
                        jnp.concatenate([v_h, st.T.astype(BF16)], axis=0))
                   + jnp.sum(q[:, sl] * k[:, sl], axis=-1, keepdims=True) * v_f)
            st_scr[hh] = st * jnp.exp2(cum_last) + lax.dot_general(
                v_h, xs_scr[hh, n_levels + 1], TN_DIMS, preferred_element_type=F32)
            g_pre = proj[:, 3 * d + hh * dk:3 * d + (hh + 1) * dk]
            o_ref[0, r0:r0 + rows, sl] = _gated_group_norm(o_h, gng, g_pre)

    @pl.when(c == pl.num_programs(1) - 1)
    def _():
        for hh in range(heads):
            s_ref[0, hh] = st_scr[hh].T


def _level_table(rows):
    t = np.arange(rows)[:, None]
    s = np.arange(rows)[None, :]
    x = t ^ s
    lev = np.where(t > s, np.floor(np.log2(np.maximum(x, 1))).astype(np.int32), np.where(t == s, -1, -2))
    return jnp.asarray(lev, dtype=jnp.int32)


def _sum_table(rows):
    t = np.arange(rows)[:, None]
    j = np.arange(rows)[None, :]
    blocks = []
    for p in range(FIRST_TABLE_LEVEL, int(math.log2(SUBLANES))):
        m = 1 << p
        ref = (t // (2 * m)) * (2 * m) + m - 1
        up = ((t >> p) & 1) == 1
        blocks.append(np.where(up, (j > ref) & (j <= t), (j > t) & (j <= ref)))
    blocks.append(j <= t)
    table = np.concatenate(blocks, axis=0).astype(np.float32)
    return jnp.asarray(np.concatenate([table, table], axis=1), dtype=BF16)


def _hgrn_prompt(layer, x, mods, n1g, w_in, lb_raw, gn_g, heads, chunk, step_rows):
    b, L, d = x.shape
    dk = d // heads
    sums = _sum_table(chunk)
    return pl.pallas_call(
        functools.partial(_hgrn_prompt_kernel, layer, heads),
        grid=(b, L // step_rows),
        in_specs=[pl.BlockSpec((1, step_rows, d), lambda i, c: (i, c, 0)),
                  _mod_spec(mods, layer),
                  _full(n1g.shape), _layer_spec(w_in, layer), _full(lb_raw.shape), _full(gn_g.shape),
                  _full((chunk, chunk)), _full(sums.shape)],
        out_specs=[pl.BlockSpec((1, step_rows, d), lambda i, c: (i, c, 0)),
                   pl.BlockSpec((1, heads, dk, dk), lambda i, c: (i, 0, 0, 0))],
        out_shape=[jax.ShapeDtypeStruct((b, L, d), F32),
                   jax.ShapeDtypeStruct((b, heads, dk, dk), F32)],
        scratch_shapes=[pltpu.VMEM((heads, dk, dk), F32), pltpu.VMEM((sums.shape[0], d), F32),
                        pltpu.VMEM((heads, int(math.log2(chunk)) + 2, chunk, dk), BF16),
                        pltpu.VMEM((heads, chunk, chunk), BF16)],
        compiler_params=_params("arbitrary", "arbitrary"),
        name=f"hgrn_prompt_{layer}",
    )(x, mods, n1g, w_in, lb_raw, gn_g, _level_table(chunk), sums)


def _post_kernel(layer, x_ref, a_ref, mod_ref, n2g_ref, wo_ref, wup_ref, wdn_ref, o_ref):
    d = x_ref.shape[2]
    mod = _mod_rows(mod_ref)
    g1, sh2, sc2, g2 = (mod[:, j * d:(j + 1) * d] for j in range(2, 6))
    x1 = x_ref[0] + g1 * _dot(a_ref[0].astype(BF16), wo_ref[0])
    h2 = _modulated_norm(x1, n2g_ref[layer:layer + 1, :], sh2, sc2).astype(BF16)
    y = None
    for c in range(0, wup_ref.shape[2], d):
        u = jnp.square(jnp.maximum(_dot(h2, wup_ref[0, :, c:c + d]), 0.0)).astype(BF16)
        part = _dot(u, wdn_ref[0, c:c + d, :])
        y = part if y is None else y + part
    o_ref[0] = x1 + g2 * y


def _post_sample_kernel(layer, x_ref, a_ref, mod_ref, n2g_ref, wo_ref, wup_ref, wdn_ref, o_ref,
                        x1_scr, h2_scr, y_scr):
    c = pl.program_id(0)
    d = x_ref.shape[1]
    mod = _mod_rows(mod_ref)

    @pl.when(c == 0)
    def _():
        g1, sh2, sc2 = (mod[:, j * d:(j + 1) * d] for j in range(2, 5))
        x1 = x_ref[...] + g1 * _dot(a_ref[...].astype(BF16), wo_ref[0])
        x1_scr[...] = x1
        h2_scr[...] = _modulated_norm(x1, n2g_ref[layer:layer + 1, :], sh2, sc2).astype(BF16)
        y_scr[...] = jnp.zeros_like(y_scr)

    u = jnp.square(jnp.maximum(_dot(h2_scr[...], wup_ref[0]), 0.0)).astype(BF16)
    y_scr[...] += _dot(u, wdn_ref[0])

    @pl.when(c == pl.num_programs(0) - 1)
    def _():
        o_ref[...] = x1_scr[...] + mod[:, 5 * d:6 * d] * y_scr[...]


def _post_sample(layer, x, a, mods, n2g, w_o, o_layer, w_up, w_dn, slab):
    r, d = x.shape
    ff = w_up.shape[2]
    return pl.pallas_call(
        functools.partial(_post_sample_kernel, layer),
        grid=(ff // slab,),
        in_specs=[_full((r, d)), _full((r, d)), _mod_spec(mods, layer), _full(n2g.shape),
                  _layer_spec(w_o, o_layer),
                  pl.BlockSpec((1, d, slab), lambda c: (layer, 0, c)),
                  pl.BlockSpec((1, slab, d), lambda c: (layer, c, 0))],
        out_specs=_full((r, d)),
        out_shape=jax.ShapeDtypeStruct((r, d), F32),
        scratch_shapes=[pltpu.VMEM((r, d), F32), pltpu.VMEM((r, d), BF16), pltpu.VMEM((r, d), F32)],
        compiler_params=_params("arbitrary"),
        name="post_mlp_sample",
    )(x, a, mods, n2g, w_o, w_up, w_dn)


def _post(layer, x, a, mods, n2g, w_o, o_layer, w_up, w_dn, tm):
    nb, L, d = x.shape
    return pl.pallas_call(
        functools.partial(_post_kernel, layer),
        grid=(nb, L // tm),
        in_specs=[pl.BlockSpec((1, tm, d), lambda i, j: (i, j, 0)),
                  pl.BlockSpec((1, tm, d), lambda i, j: (i, j, 0)),
                  _mod_spec(mods, layer), _full(n2g.shape),
                  _layer_spec(w_o, o_layer), _layer_spec(w_up, layer), _layer_spec(w_dn, layer)],
        out_specs=pl.BlockSpec((1, tm, d), lambda i, j: (i, j, 0)),
        out_shape=jax.ShapeDtypeStruct((nb, L, d), F32),
        compiler_params=_params("arbitrary", "arbitrary"),
        name="post_mlp",
    )(x, a, mods, n2g, w_o, w_up, w_dn)


def _kv_kernel(head_dim, window, x_ref, mod_ref, g_ref, w_ref, kg_ref, cos_ref, sin_ref,
               k_ref, v_ref, kx_ref, vx_ref):
    tm, d = x_ref.shape[1], x_ref.shape[2]
    mod = _mod_rows(mod_ref)
    hk = _modulated_norm(x_ref[0], g_ref[...], mod[:, 0:d], mod[:, d:2 * d])
    y = _dot(hk.astype(BF16), w_ref[...])
    n = y.shape[1] // 2
    cos, sin, kg = cos_ref[...], sin_ref[...], kg_ref[...]
    low = lax.broadcasted_iota(jnp.int32, (tm, LANES), 1) < head_dim
    k_blocks, v_blocks = _heads_norm_rope(y[:, 0:n], kg, cos, sin, head_dim, 1.0), []
    for j in range(n // LANES):
        kb = k_blocks[j]
        vb = y[:, n + j * LANES:n + (j + 1) * LANES]
        v_blocks.append(vb)
        kx_ref[0, :, j * LANES:(j + 1) * LANES] = kb.astype(BF16)
        kx_ref[0, :, n + j * LANES:n + (j + 1) * LANES] = pltpu.roll(kb, head_dim, 1).astype(BF16)
        vs = pltpu.roll(vb, head_dim, 1)
        variants = (jnp.where(low, vb, 0.0), jnp.where(low, 0.0, vs),
                    jnp.where(low, vs, 0.0), jnp.where(low, 0.0, vb))
        for i, var in enumerate(variants):
            off = (4 * j + i) * LANES
            vx_ref[0, off:off + LANES, :] = var.astype(BF16).T

    @pl.when(pl.program_id(1) == pl.num_programs(1) - 1)
    def _():
        for j in range(n // LANES):
            k_ref[0, :, j * LANES:(j + 1) * LANES] = k_blocks[j][tm - window:, :]
            v_ref[0, :, j * LANES:(j + 1) * LANES] = v_blocks[j][tm - window:, :]


def _kv(x, mods, g, w_kv, kg, cos, sin, head_dim, window, tm):
    nb, L, d = x.shape
    n = w_kv.shape[1] // 2
    rope_rows = tm if cos.shape[0] > 1 else 1
    rope_map = (lambda i, j: (j, 0)) if cos.shape[0] > 1 else (lambda i, j: (0, 0))
    last = lambda i, j: (i, 0, 0)
    tile = lambda i, j: (i, j, 0)
    return pl.pallas_call(
        functools.partial(_kv_kernel, head_dim, window),
        grid=(nb, L // tm),
        in_specs=[pl.BlockSpec((1, tm, d), tile), _mod_spec(mods, 0),
                  _full((1, d)), _full(w_kv.shape), _full((1, LANES)),
                  pl.BlockSpec((rope_rows, LANES), rope_map),
                  pl.BlockSpec((rope_rows, LANES), rope_map)],
        out_specs=[pl.BlockSpec((1, window, n), last), pl.BlockSpec((1, window, n), last),
                   pl.BlockSpec((1, tm, 2 * n), tile), pl.BlockSpec((1, 4 * n, tm), lambda i, j: (i, 0, j))],
        out_shape=[jax.ShapeDtypeStruct((nb, window, n), F32), jax.ShapeDtypeStruct((nb, window, n), F32),
                   jax.ShapeDtypeStruct((nb, L, 2 * n), BF16), jax.ShapeDtypeStruct((nb, 4 * n, L), BF16)],
        compiler_params=_params("arbitrary", "arbitrary"),
        name="kv_proj",
    )(x, mods, g, w_kv, kg, cos, sin)


def _q_proj(x, mod, n1g, wq, qg, cos, sin, head_dim, extra_scale=1.0):
    d = x.shape[1]
    h = _modulated_norm(x, n1g, mod[:, 0:d], mod[:, d:2 * d])
    y = _dot(h.astype(BF16), wq)
    return _heads_norm_rope(y, qg, cos, sin, head_dim, extra_scale / math.sqrt(head_dim))


def _attn_prompt_kernel(layer, att_layer, head_dim, kv_heads, nq, x_ref, mod_ref, n1g_ref, wq_ref, qg_ref,
                        cos_ref, sin_ref, *rest):
    k_refs, v_refs = rest[0:nq + 1], rest[nq + 1:2 * nq + 2]
    sink_ref, o_ref, s_scr, p_scr = rest[2 * nq + 2:]
    w = x_ref.shape[1] // nq
    nkv = kv_heads * head_dim
    q_blocks = _q_proj(x_ref[0], _mod_rows(mod_ref), n1g_ref[layer:layer + 1, :], wq_ref[0],
                       qg_ref[att_layer:att_layer + 1, :], cos_ref[...], sin_ref[...], head_dim, LOG2E)
    group = (2 * len(q_blocks)) // kv_heads
    key = lax.broadcasted_iota(jnp.int32, (w, 2 * w), 0)
    qry = lax.broadcasted_iota(jnp.int32, (w, 2 * w), 1)
    own = key <= (qry & (w - 1))
    first = lax.broadcasted_iota(jnp.int32, (1, 2 * w), 1) < w
    low = lax.broadcasted_iota(jnp.int32, (w, LANES), 1) < head_dim
    no_prev = jnp.where(pl.program_id(1) == 0, -jnp.inf, 0.0)
    tiles = [(u, kh, half) for u in range(nq) for kh in range(kv_heads) for half in range(2)]
    for n, (u, kh, half) in enumerate(tiles):
        ha, hb = kh * group + half, kh * group + 2 + half
        mask = (lambda t: jnp.where(low, t, 0.0)) if half == 0 else (lambda t: jnp.where(low, 0.0, t))
        rows = slice(u * w, (u + 1) * w)
        qcat = jnp.concatenate([mask(q_blocks[ha // 2][rows]), mask(q_blocks[hb // 2][rows])],
                               axis=0).astype(BF16)
        koff = (kh // 2) * LANES + (0 if kh % 2 == half else nkv)
        kcat = jnp.concatenate([k_refs[u][0, :, koff:koff + LANES], k_refs[u + 1][0, :, koff:koff + LANES]],
                               axis=0)
        s_t = _dot_nt(kcat, qcat)
        s_prev = s_t[:w] + no_prev if u == 0 else s_t[:w]
        s_scr[n] = jnp.where(own, s_t[w:], s_prev)
    for n, (u, kh, half) in enumerate(tiles):
        sc = s_scr[n]
        sink = jnp.where(first, sink_ref[att_layer, kh * group + half],
                         sink_ref[att_layer, kh * group + 2 + half]) * LOG2E
        m = jnp.maximum(jnp.max(sc, axis=0, keepdims=True), sink)
        p = jnp.exp2(sc - m)
        pn = p * (1.0 / (jnp.sum(p, axis=0, keepdims=True) + jnp.exp2(sink - m)))
        p_scr[n, 0:w] = jnp.where(own, pn, 0.0).astype(BF16)
        p_scr[n, w:2 * w] = jnp.where(own, 0.0, pn).astype(BF16)
    for u in range(nq):
        for kh in range(kv_heads):
            out_t = None
            for half in range(2):
                voff = (2 * kh + half) * LANES
                v_t = jnp.concatenate([v_refs[u + 1][0, voff:voff + LANES, :],
                                       v_refs[u][0, voff:voff + LANES, :]], axis=1)
                part = _dot(v_t, p_scr[(u * kv_heads + kh) * 2 + half])
                out_t = part if out_t is None else out_t + part
            for i in range(2):
                o_ref[0, u * w:(u + 1) * w, (2 * kh + i) * LANES:(2 * kh + i + 1) * LANES] = (
                    out_t[:, i * w:(i + 1) * w].T)


def _attn_prompt(layer, att_layer, x, mods, n1g, wq, qg, cos, sin, kx, vx, sinks, head_dim, kv_heads, w, nq):
    b, L, d = x.shape
    rows = lambda i, j: (i, j, 0)
    k_specs = [pl.BlockSpec((1, w, kx.shape[2]), lambda i, j, o=o: (i, jnp.maximum(nq * j + o, 0), 0))
               for o in range(-1, nq)]
    v_specs = [pl.BlockSpec((1, vx.shape[1], w), lambda i, j, o=o: (i, 0, jnp.maximum(nq * j + o, 0)))
               for o in range(-1, nq)]
    n_tiles = nq * 2 * kv_heads
    return pl.pallas_call(
        functools.partial(_attn_prompt_kernel, layer, att_layer, head_dim, kv_heads, nq),
        grid=(b, L // (nq * w)),
        in_specs=[pl.BlockSpec((1, nq * w, d), rows), _mod_spec(mods, layer),
                  _full(n1g.shape), _layer_spec(wq, att_layer), _full(qg.shape),
                  pl.BlockSpec((nq * w, LANES), lambda i, j: (j, 0)),
                  pl.BlockSpec((nq * w, LANES), lambda i, j: (j, 0))]
                 + k_specs + v_specs + [pl.BlockSpec(memory_space=pltpu.SMEM)],
        out_specs=pl.BlockSpec((1, nq * w, d), rows),
        out_shape=jax.ShapeDtypeStruct((b, L, d), F32),
        scratch_shapes=[pltpu.VMEM((n_tiles, w, 2 * w), F32), pltpu.VMEM((n_tiles, 2 * w, 2 * w), BF16)],
        compiler_params=_params("arbitrary", "arbitrary"),
        name="attn_prompt",
    )(x, mods, n1g, wq, qg, cos, sin, *([kx] * (nq + 1)), *([vx] * (nq + 1)), sinks)


def _proj_kernel(layer, x_ref, mod_ref, g_ref, w_ref, o_ref):
    d = x_ref.shape[1]
    mod = _mod_rows(mod_ref)
    h = _modulated_norm(x_ref[...], g_ref[layer:layer + 1, :], mod[:, 0:d], mod[:, d:2 * d])
    o_ref[...] = _dot(h.astype(BF16), w_ref[0])


def _proj(layer, x, mods, g, w, tn):
    r, d = x.shape
    n = w.shape[2]
    return pl.pallas_call(
        functools.partial(_proj_kernel, layer),
        grid=(n // tn,),
        in_specs=[_full((r, d)), _mod_spec(mods, layer), _full(g.shape),
                  pl.BlockSpec((1, d, tn), lambda j: (layer, 0, j))],
        out_specs=pl.BlockSpec((r, tn), lambda j: (0, j)),
        out_shape=jax.ShapeDtypeStruct((r, n), F32),
        compiler_params=_params("arbitrary"),
        name="proj_sample",
    )(x, mods, g, w)


def _q_sample_kernel(layer, att_layer, head_dim, x_ref, mod_ref, n1g_ref, wq_ref, qg_ref, cos_ref, sin_ref,
                     o_ref):
    blocks = _q_proj(x_ref[...], _mod_rows(mod_ref), n1g_ref[layer:layer + 1, :], wq_ref[0],
                     qg_ref[att_layer:att_layer + 1, :], cos_ref[...], sin_ref[...], head_dim)
    for j, blk in enumerate(blocks):
        o_ref[:, j * LANES:(j + 1) * LANES] = blk


def _q_sample(layer, att_layer, x, mods, n1g, wq, qg, cos, sin, head_dim):
    r, d = x.shape
    return pl.pallas_call(
        functools.partial(_q_sample_kernel, layer, att_layer, head_dim),
        grid=(1,),
        in_specs=[_full((r, d)), _mod_spec(mods, layer), _full(n1g.shape), _layer_spec(wq, att_layer),
                  _full(qg.shape), _full((1, LANES)), _full((1, LANES))],
        out_specs=_full((r, wq.shape[2])),
        out_shape=jax.ShapeDtypeStruct((r, wq.shape[2]), F32),
        compiler_params=_params("arbitrary"),
        name="q_sample",
    )(x, mods, n1g, wq, qg, cos, sin)


def _hgrn_sample_kernel(layer, heads, tb, steps, aliased, proj_ref, lbraw_ref, gng_ref, s_ref, *rest):
    o_ref, so_ref, stack_scr, acc_scr = rest[1:] if aliased else rest
    i = pl.program_id(0)
    rows = proj_ref.shape[0]
    d = proj_ref.shape[1] // 4
    dk = d // heads

    def update():
        @pl.when(i == 0)
        def _():
            lb = _lower_bound(lbraw_ref[...], layer)
            q, fg = _hgrn_gates(proj_ref[:, 0:2 * d], lb, d)
            for hh in range(heads):
                sl = slice(hh * dk, (hh + 1) * dk)
                ft = fg[:, sl].T
                hi = ft.astype(BF16)
                stack_scr[hh, 0:dk] = hi
                stack_scr[hh, dk:2 * dk] = (ft - hi.astype(F32)).astype(BF16)
                stack_scr[hh, 2 * dk:3 * dk] = (1.0 - ft).astype(BF16)
                stack_scr[hh, 3 * dk:4 * dk] = q[:, sl].T.astype(BF16)

        token_row = lax.broadcasted_iota(jnp.int32, (rows, dk), 0)
        sub = lax.broadcasted_iota(jnp.int32, (tb, dk), 0)
        base = pl.multiple_of(i * tb, tb)
        v_rows = proj_ref[pl.ds(base, tb), 2 * d:3 * d]
        o_rows = [jnp.zeros((tb, dk), F32) for _ in range(heads)]
        for t0 in range(0, tb, 2):
            onehot = jnp.concatenate([jnp.where(token_row == base + t, 1.0, 0.0) for t in (t0, t0 + 1)],
                                     axis=1).astype(BF16)
            for hh in range(heads):
                bc2 = _dot(stack_scr[hh], onehot)
                for t in (t0, t0 + 1):
                    bc = bc2[:, (t - t0) * dk:(t - t0 + 1) * dk]
                    f_b = bc[0:dk] + bc[dk:2 * dk]
                    k_b = bc[2 * dk:3 * dk]
                    q_b = bc[3 * dk:4 * dk]
                    s_new = f_b * s_ref[0, t, hh] + k_b * v_rows[t:t + 1, hh * dk:(hh + 1) * dk]
                    so_ref[0, t, hh] = s_new
                    o_rows[hh] = jnp.where(sub == t, jnp.sum(q_b * s_new, axis=0, keepdims=True), o_rows[hh])
        for hh in range(heads):
            acc_scr[pl.ds(base, tb), hh * dk:(hh + 1) * dk] = o_rows[hh]

        @pl.when(i == steps - 1)
        def _():
            gng = gng_ref[layer:layer + 1, :]
            for hh in range(heads):
                sl = slice(hh * dk, (hh + 1) * dk)
                o_ref[:, sl] = _gated_group_norm(acc_scr[:, sl], gng,
                                                 proj_ref[:, 3 * d + hh * dk:3 * d + (hh + 1) * dk])

    if aliased:
        update()
    else:
        pl.when(i < steps)(update)

        @pl.when(i >= steps)
        def _():
            so_ref[...] = jnp.zeros_like(so_ref)


def _hgrn_sample(layer, proj, lb_raw, gn_g, state, new_state, tb):
    n_layers, nb, heads, dk, dv = state.shape
    d = proj.shape[1] // 4
    steps = nb // tb
    aliased = new_state is not None
    assert aliased == (layer > 0) and tb == SUBLANES
    if aliased:
        grid = (steps,)
        s_in = s_out = pl.BlockSpec((1, tb, heads, dk, dv), lambda i: (layer, i, 0, 0, 0))
    else:
        grid = (n_layers * steps,)
        s_in = pl.BlockSpec((1, tb, heads, dk, dv), lambda i: (0, jnp.minimum(i, steps - 1), 0, 0, 0))
        s_out = pl.BlockSpec((1, tb, heads, dk, dv), lambda i: (i // steps, i % steps, 0, 0, 0))
    in_specs = [_full(proj.shape), _full(lb_raw.shape), _full(gn_g.shape), s_in]
    args = [proj, lb_raw, gn_g, state]
    if aliased:
        in_specs.append(pl.BlockSpec(memory_space=pl.ANY))
        args.append(new_state)
    return pl.pallas_call(
        functools.partial(_hgrn_sample_kernel, layer, heads, tb, steps, aliased),
        grid=grid,
        in_specs=in_specs,
        out_specs=[_full((nb, d)), s_out],
        out_shape=[jax.ShapeDtypeStruct((nb, d), F32), jax.ShapeDtypeStruct(state.shape, F32)],
        scratch_shapes=[pltpu.VMEM((heads, 4 * dk, nb), BF16), pltpu.VMEM((nb, d), F32)],
        input_output_aliases={4: 1} if aliased else {},
        compiler_params=_params("arbitrary"),
        name=f"hgrn_sample_{layer}",
    )(*args)


def _cache_roll_kernel(ck_ref, cv_ref, kn_ref, vn_ref, ko_ref, vo_ref, kb_ref, vb_ref):
    w = ck_ref.shape[1]
    ko_ref[:, 0:w - 1, :] = ck_ref[:, 1:w, :]
    ko_ref[:, w - 1:w, :] = kn_ref[...]
    vo_ref[:, 0:w - 1, :] = cv_ref[:, 1:w, :]
    vo_ref[:, w - 1:w, :] = vn_ref[...]
    kb_ref[...] = ko_ref[...].astype(BF16)
    vb_ref[...] = vo_ref[...].astype(BF16)


def _cache_roll(ck, cv, kn, vn, tb):
    nb, w, n = ck.shape
    blk = pl.BlockSpec((tb, w, n), lambda i: (i, 0, 0))
    new = pl.BlockSpec((tb, 1, n), lambda i: (i, 0, 0))
    return pl.pallas_call(
        _cache_roll_kernel,
        grid=(nb // tb,),
        in_specs=[blk, blk, new, new],
        out_specs=[blk, blk, blk, blk],
        out_shape=[jax.ShapeDtypeStruct(ck.shape, F32), jax.ShapeDtypeStruct(cv.shape, F32),
                   jax.ShapeDtypeStruct(ck.shape, BF16), jax.ShapeDtypeStruct(cv.shape, BF16)],
        compiler_params=_params("arbitrary"),
        name="cache_roll",
    )(ck, cv, kn.reshape(nb, 1, n), vn.reshape(nb, 1, n))


def _attn_sample_kernel(att_layer, tb, q_ref, k_ref, v_ref, sink_ref, o_ref):
    heads, head_dim = q_ref.shape[1], q_ref.shape[2]
    nkv = k_ref.shape[2]
    group = heads // (nkv // head_dim)
    r = lax.broadcasted_iota(jnp.int32, (heads, nkv), 0) // group
    c = lax.broadcasted_iota(jnp.int32, (heads, nkv), 1) // head_dim
    own = r == c
    sink = sink_ref[:, att_layer:att_layer + 1][None]
    q = q_ref[...]
    qe = jnp.where(own[None], jnp.concatenate([q] * (nkv // head_dim), axis=-1), 0.0).astype(BF16)
    s = jnp.einsum("thc,tjc->thj", qe, k_ref[...], preferred_element_type=F32)
    m = jnp.maximum(jnp.max(s, axis=-1, keepdims=True), sink)
    p = jnp.exp(s - m)
    denom = jnp.sum(p, axis=-1, keepdims=True) + jnp.exp(sink - m)
    pv = jnp.where(own[None], jnp.einsum("thj,tjc->thc", p.astype(BF16), v_ref[...],
                                         preferred_element_type=F32), 0.0)
    o = pv[:, :, 0:head_dim]
    for j in range(1, nkv // head_dim):
        o = o + pv[:, :, j * head_dim:(j + 1) * head_dim]
    o_ref[...] = o / denom


def _attn_sample(att_layer, q3, k, v, sinks_t, tb):
    nb, heads, head_dim = q3.shape
    w, nkv = k.shape[1], k.shape[2]
    return pl.pallas_call(
        functools.partial(_attn_sample_kernel, att_layer, tb),
        grid=(nb // tb,),
        in_specs=[pl.BlockSpec((tb, heads, head_dim), lambda i: (i, 0, 0)),
                  pl.BlockSpec((tb, w, nkv), lambda i: (i, 0, 0)),
                  pl.BlockSpec((tb, w, nkv), lambda i: (i, 0, 0)),
                  _full(sinks_t.shape)],
        out_specs=pl.BlockSpec((tb, heads, head_dim), lambda i: (i, 0, 0)),
        out_shape=jax.ShapeDtypeStruct(q3.shape, F32),
        compiler_params=_params("arbitrary"),
        name="attn_sample",
    )(q3, k, v, sinks_t)


def _rope_tables(pos, head_dim):
    half = head_dim // 2
    inv = ROPE_THETA ** (-jnp.arange(half, dtype=F32) / half)
    ang = pos.astype(F32)[:, None] * inv[None, :]
    cos, sin = jnp.cos(ang), jnp.sin(ang)
    reps = LANES // head_dim
    return (jnp.tile(jnp.concatenate([cos, cos], axis=1), (1, reps)),
            jnp.tile(jnp.concatenate([-sin, sin], axis=1), (1, reps)))


def kernel(x_prompt, x_sample, c_prompt, c_sample, state_hgrn, cache_k, cache_v, w_ada, b_ada, norm1_g, norm2_g, hg_w_in, hg_w_out, hg_lower_bounds, hg_gn_g, kv_w_ada, kv_b_ada, kv_norm_g, w_kv, k_norm_g, w_q, q_norm_g, sinks, w_o, w_up, w_down):
    bp, Lp, d = x_prompt.shape
    bs = x_sample.shape[0]
    depth = w_ada.shape[0]
    n_a = hg_w_in.shape[0]
    heads = state_hgrn.shape[2]
    window, kv_heads, head_dim = cache_k.shape[1], cache_k.shape[2], cache_k.shape[3]
    assert LANES == 2 * head_dim and kv_heads % 2 == 0 and Lp % window == 0
    assert window & (window - 1) == 0 and (d // head_dim) // kv_heads == 4
    nkv = kv_heads * head_dim
    reps = LANES // head_dim
    bf = lambda t: t.astype(BF16)

    pad = (-(bs + bp)) % 16
    c_all = jnp.concatenate([c_sample, c_prompt, jnp.zeros((pad, d), F32)], axis=0)
    mods_s, mods_p = _ada(c_all, bs, bp, w_ada, b_ada, T.ada_cols)
    kv_mods_s, kv_mods_p = _ada(c_all, bs, bp, kv_w_ada[None], kv_b_ada[None], T.kv_ada_cols)

    cos_p, sin_p = _rope_tables(jnp.arange(Lp), head_dim)
    cos_s, sin_s = _rope_tables(jnp.full((1,), PAST_LEN), head_dim)
    kg = jnp.tile(k_norm_g, reps)[None]
    qg = jnp.tile(q_norm_g, (1, reps))
    sinks_t = sinks.T
    w_kv_b, w_in_b, w_out_b, w_q_b, w_o_b = bf(w_kv), bf(hg_w_in), bf(hg_w_out), bf(w_q), bf(w_o)
    w_up_b, w_dn_b = bf(w_up), bf(w_down)

    xp = x_prompt
    xs = x_sample.reshape(bs, d)
    hg_p, hg_s = [], None
    k_p = v_p = kx_p = vx_p = k_s = v_s = None
    for l in range(depth):
        if l == n_a:
            k_p, v_p, kx_p, vx_p = _kv(xp, kv_mods_p, kv_norm_g[None], w_kv_b, kg, cos_p, sin_p,
                                       head_dim, window, T.kv_rows)
            k_n, v_n, _, _ = _kv(xs[None], kv_mods_s, kv_norm_g[None], w_kv_b, kg, cos_s, sin_s,
                                 head_dim, bs, bs)
            k_s, v_s, kb_s, vb_s = _cache_roll(cache_k.reshape(bs, window, nkv),
                                               cache_v.reshape(bs, window, nkv), k_n[0], v_n[0], T.cache_tokens)
        if l < n_a:
            w_mix_b, mix_layer = w_out_b, l
            a_p, s_p = _hgrn_prompt(l, xp, mods_p, norm1_g, w_in_b, hg_lower_bounds, hg_gn_g, heads,
                                    T.hgrn_chunk, T.hgrn_rows)
            hg_p.append(s_p)
            proj_s = _proj(l, xs, mods_s, norm1_g, w_in_b, T.proj_cols)
            a_s, hg_s = _hgrn_sample(l, proj_s, hg_lower_bounds, hg_gn_g, state_hgrn, hg_s, SUBLANES)
        else:
            j = l - n_a
            w_mix_b, mix_layer = w_o_b, j
            a_p = _attn_prompt(l, j, xp, mods_p, norm1_g, w_q_b, qg, cos_p, sin_p, kx_p, vx_p, sinks,
                               head_dim, kv_heads, window, T.attn_blocks)
            q_s = _q_sample(l, j, xs, mods_s, norm1_g, w_q_b, qg, cos_s, sin_s, head_dim)
            a_s = _attn_sample(j, q_s.reshape(bs, d // head_dim, head_dim), kb_s, vb_s, sinks_t,
                               T.attn_sample_tokens)
            a_s = a_s.reshape(bs, d)
        xp = _post(l, xp, a_p, mods_p, norm2_g, w_mix_b, mix_layer, w_up_b, w_dn_b, T.mlp_rows)
        xs = _post_sample(l, xs, a_s, mods_s, norm2_g, w_mix_b, mix_layer, w_up_b, w_dn_b, T.mlp_sample_slab)

    shape4 = lambda t: t.reshape(t.shape[0], window, kv_heads, head_dim)
    return (xp, xs.reshape(bs, 1, d), jnp.stack(hg_p), shape4(k_p), shape4(v_p), hg_s,
            shape4(k_s), shape4(v_s))
```

```python
import functools
import math
from typing import NamedTuple

import numpy as np
import jax
import jax.numpy as jnp
from jax import lax
from jax.experimental import pallas as pl
from jax.experimental.pallas import tpu as pltpu

F32 = jnp.float32
BF16 = jnp.bfloat16

PAST_LEN = 8192
ROPE_THETA = 10000.0
EPS = 1e-6
LOG2E = 1.4426950408889634
LANES = 128
SUBLANES = 8
FIRST_TABLE_LEVEL = 2
VMEM_LIMIT = 56 * 1024 * 1024


class _Tiles(NamedTuple):
    ada_cols: int = 3072
    kv_ada_cols: int = 1024
    hgrn_chunk: int = 128
    hgrn_rows: int = 512
    mlp_rows: int = 512
    kv_rows: int = 512
    attn_blocks: int = 8
    proj_cols: int = 1024
    cache_tokens: int = 8
    attn_sample_tokens: int = 16
    mlp_sample_slab: int = 1024


T = _Tiles()

NT_DIMS = (((1,), (1,)), ((), ()))
TN_DIMS = (((0,), (0,)), ((), ()))


def _dot(a, b):
    return jnp.dot(a, b, preferred_element_type=F32)


def _dot_nt(a, b):
    return lax.dot_general(a, b, NT_DIMS, preferred_element_type=F32)


def _sigmoid(x):
    return 1.0 / (1.0 + jnp.exp(-x))


def _silu(x):
    return x * (0.5 + 0.5 * jnp.tanh(0.5 * x))


def _rms(x, g):
    ms = jnp.mean(x * x, axis=-1, keepdims=True)
    return x * lax.rsqrt(ms + EPS) * g


def _modulated_norm(x, gain, shift, scale):
    return _rms(x, gain) * (1.0 + scale) + shift


def _params(*sem):
    return pltpu.CompilerParams(dimension_semantics=sem, vmem_limit_bytes=VMEM_LIMIT)


def _full(shape):
    n = len(shape)
    return pl.BlockSpec(shape, lambda *_: (0,) * n)


def _ada_kernel(c_ref, w_ref, b_ref, os_ref, op_ref):
    bs, bp = os_ref.shape[1], op_ref.shape[1]
    c = c_ref[...]
    a = _silu(c).astype(BF16)
    res = _dot(a, w_ref[0].astype(BF16)) + b_ref[0]
    os_ref[0] = res[0:bs]
    for r in range(bp):
        op_ref[0, r] = jnp.broadcast_to(res[bs + r:bs + r + 1], op_ref.shape[2:])


def _ada(c_all, bs, bp, w, b, tn):
    nl, d, n = w.shape
    r = c_all.shape[0]
    return pl.pallas_call(
        _ada_kernel,
        grid=(nl, n // tn),
        in_specs=[pl.BlockSpec((r, d), lambda l, j: (0, 0)),
                  pl.BlockSpec((1, d, tn), lambda l, j: (l, 0, j)),
                  pl.BlockSpec((1, 1, tn), lambda l, j: (l, 0, j))],
        out_specs=[pl.BlockSpec((1, bs, tn), lambda l, j: (l, 0, j)),
                   pl.BlockSpec((1, bp, SUBLANES, tn), lambda l, j: (l, 0, 0, j))],
        out_shape=[jax.ShapeDtypeStruct((nl, bs, n), F32),
                   jax.ShapeDtypeStruct((nl, bp, SUBLANES, n), F32)],
        compiler_params=_params("arbitrary", "arbitrary"),
        name="ada",
    )(c_all, w, b.reshape(nl, 1, n))


def _mod_rows(mod_ref):
    return mod_ref[0, 0, 0:1, :] if len(mod_ref.shape) == 4 else mod_ref[0]


def _mod_spec(mods, layer):
    if mods.ndim == 4:
        return pl.BlockSpec((1, 1) + mods.shape[2:], lambda i, *_: (layer, i, 0, 0))
    return pl.BlockSpec((1,) + mods.shape[1:], lambda *_: (layer, 0, 0))


def _layer_spec(stack, layer):
    zeros = (0,) * (stack.ndim - 1)
    return pl.BlockSpec((1,) + stack.shape[1:], lambda *_: (layer,) + zeros, pipeline_mode=pl.Buffered(1))


def _group_mean_matrix(group, width):
    r = lax.broadcasted_iota(jnp.int32, (width, width), 0) // group
    c = lax.broadcasted_iota(jnp.int32, (width, width), 1) // group
    return jnp.where(r == c, 1.0 / group, 0.0).astype(BF16)


def _heads_norm_rope(y, gain, cos, sin, head_dim, out_scale):
    rows, n = y.shape
    width = 2 * LANES if n % (2 * LANES) == 0 else LANES
    mean = _group_mean_matrix(head_dim, width)
    gain = gain * out_scale
    half = head_dim // 2
    first = (lax.broadcasted_iota(jnp.int32, (rows, LANES), 1) % head_dim) < half
    out = []
    for j in range(0, n, width):
        ms = _dot(jnp.square(y[:, j:j + width]).astype(BF16), mean)
        for i in range(0, width, LANES):
            yn = y[:, j + i:j + i + LANES] * lax.rsqrt(ms[:, i:i + LANES] + EPS) * gain
            rot = jnp.where(first, pltpu.roll(yn, LANES - half, 1), pltpu.roll(yn, half, 1))
            out.append(yn * cos + rot * sin)
    return out


def _lower_bound(raw, layer):
    m = jnp.max(raw, axis=0, keepdims=True)
    e = jnp.exp(raw - m)
    sm = e / jnp.sum(e, axis=0, keepdims=True)
    acc = sm[0:1]
    for j in range(1, layer + 1):
        acc = acc + sm[j:j + 1]
    return acc - sm[0:1]


def _hgrn_gates(proj, lb, d):
    qa = proj[:, 0:d]
    q = _silu(qa)
    fg = lb + (1.0 - lb) * _sigmoid(proj[:, d:2 * d])
    return q, fg


def _gated_group_norm(o, gain, gate_pre):
    ms = jnp.mean(o * o, axis=-1, keepdims=True)
    return o * lax.rsqrt(ms + EPS) * gain * _silu(gate_pre)


def _hgrn_prompt_kernel(layer, heads, x_ref, mod_ref, n1g_ref, win_ref, lbraw_ref, gng_ref,
                        lev_ref, sums_ref, o_ref, s_ref, st_scr, d_scr, xs_scr, att_scr):
    c = pl.program_id(1)
    total, d = x_ref.shape[1], x_ref.shape[2]
    rows = lev_ref.shape[0]
    dk = d // heads
    n_levels = int(math.log2(rows))

    @pl.when(c == 0)
    def _():
        st_scr[...] = jnp.zeros_like(st_scr)

    mod = _mod_rows(mod_ref)
    h = _modulated_norm(x_ref[0], n1g_ref[layer:layer + 1, :], mod[:, 0:d], mod[:, d:2 * d])
    proj_all = _dot(h.astype(BF16), win_ref[0])
    lb = _lower_bound(lbraw_ref[...], layer)
    q_all, fg_all = _hgrn_gates(proj_all, lb, d)
    lf2_all = jnp.log(fg_all) * LOG2E
    n_table = d_scr.shape[0] // rows - 1 + FIRST_TABLE_LEVEL - 1
    cum_row = d_scr.shape[0] - rows
    lev = lev_ref[...]
    row_id = lax.broadcasted_iota(jnp.int32, (rows, dk), 0)
    upper = [((row_id >> p) & 1) == 1 for p in range(n_table + 1)]
    tile_row = lax.broadcasted_iota(jnp.int32, (rows // SUBLANES, SUBLANES, dk), 1) & 3
    col = lax.broadcasted_iota(jnp.int32, (1, rows), 1)
    gng = gng_ref[layer:layer + 1, :]

    for r0 in range(0, total, rows):
        proj, q, fg = (t[r0:r0 + rows] for t in (proj_all, q_all, fg_all))
        k = 1.0 - fg
        lf2 = lf2_all[r0:r0 + rows]
        hi = lf2.astype(BF16)
        lo = (lf2 - hi.astype(F32)).astype(BF16)
        d_scr[...] = _dot(sums_ref[...], jnp.concatenate([hi, lo], axis=0))

        for hh in range(heads):
            sl = slice(hh * dk, (hh + 1) * dk)
            q_h, k_h = q[:, sl], k[:, sl]
            cum = d_scr[cum_row:cum_row + rows, sl]
            f_h = fg[:, sl]
            xs_scr[hh, 0] = jnp.where(upper[0], q_h * f_h, k_h).astype(BF16)
            f_t = f_h.reshape(rows // SUBLANES, SUBLANES, dk)
            e1 = jnp.where(tile_row == 3, f_t * pltpu.roll(f_t, 1, 1),
                           jnp.where(tile_row == 2, f_t,
                                     jnp.where(tile_row == 1, 1.0, pltpu.roll(f_t, SUBLANES - 1, 1))))
            xs_scr[hh, 1] = (jnp.where(upper[1], q_h, k_h) * e1.reshape(rows, dk)).astype(BF16)
            cum_last = cum[rows - 1:rows, :]
            edge = 2 << n_table
            q_dec = [q_h[0:edge] * jnp.exp2(cum[0:edge])]
            k_dec = [k_h[rows - edge:rows] * jnp.exp2(cum_last - cum[rows - edge:rows])]
            for p in range(FIRST_TABLE_LEVEL, n_levels):
                m = 1 << p
                if p <= n_table:
                    e = jnp.exp2(d_scr[(p - FIRST_TABLE_LEVEL) * rows:(p - FIRST_TABLE_LEVEL + 1) * rows, sl])
                    xs_scr[hh, p] = (jnp.where(upper[p], q_h, k_h) * e).astype(BF16)
                    continue
                pieces = []
                for r in range(0, rows, m):
                    ref = (r // (2 * m)) * 2 * m + m - 1
                    if (r // m) % 2:
                        pieces.append(q_h[r:r + m] * jnp.exp2(cum[r:r + m] - cum[ref:ref + 1]))
                    else:
                        pieces.append(k_h[r:r + m] * jnp.exp2(cum[ref:ref + 1] - cum[r:r + m]))
                xs_scr[hh, p] = jnp.concatenate(pieces, axis=0).astype(BF16)
                q_dec.append(pieces[1] * jnp.exp2(cum[m - 1:m]))
                k_dec.insert(0, pieces[rows // m - 2] * jnp.exp2(cum_last - cum[rows - m - 1:rows - m]))
            xs_scr[hh, n_levels] = jnp.concatenate(q_dec, axis=0).astype(BF16)
            xs_scr[hh, n_levels + 1] = jnp.concatenate(k_dec, axis=0).astype(BF16)

        for hh in range(heads):
            xs = xs_scr[hh, 0]
            att = jnp.where(lev == 0, _dot_nt(xs, xs), 0.0)
            for p in range(1, n_levels):
                m = 1 << p
                xs = xs_scr[hh, p]
                pp = _dot_nt(xs, xs)
                if p <= n_table:
                    att = jnp.where(lev == p, pp, att)
                else:
                    att = jnp.concatenate(
                        [jnp.where((col >= r - m) & (col < r), pp[r:r + m], att[r:r + m]) if (r // m) % 2
                         else att[r:r + m] for r in range(0, rows, m)], axis=0)
            att_scr[hh] = att.astype(BF16)

        for hh in range(heads):
            sl = slice(hh * dk, (hh + 1) * dk)
            v_f = proj[:, 2 * d + hh * dk:2 * d + (hh + 1) * dk]
            v_h = v_f.astype(BF16)
            cum_last = d_scr[cum_row + rows - 1:cum_row + rows, sl]
            st = st_scr[hh]
            o_h = (_dot(jnp.concatenate([att_scr[hh], xs_scr[hh, n_levels]], axis=1),
                        jnp.concatenate([v_h, st.T.astype(BF16)], axis=0))
                   + jnp.sum(q[:, sl] * k[:, sl], axis=-1, keepdims=True) * v_f)
            st_scr[hh] = st * jnp.exp2(cum_last) + lax.dot_general(
                v_h, xs_scr[hh, n_levels + 1], TN_DIMS, preferred_element_type=F32)
            g_pre = proj[:, 3 * d + hh * dk:3 * d + (hh + 1) * dk]
            o_ref[0, r0:r0 + rows, sl] = _gated_group_norm(o_h, gng, g_pre)

    @pl.when(c == pl.num_programs(1) - 1)
    def _():
        for hh in range(heads):
            s_ref[0, hh] = st_scr[hh].T


def _level_table(rows):
    t = np.arange(rows)[:, None]
    s = np.arange(rows)[None, :]
    x = t ^ s
    lev = np.where(t > s, np.floor(np.log2(np.maximum(x, 1))).astype(np.int32), np.where(t == s, -1, -2))
    return jnp.asarray(lev, dtype=jnp.int32)


def _sum_table(rows):
    t = np.arange(rows)[:, None]
    j = np.arange(rows)[None, :]
    blocks = []
    for p in range(FIRST_TABLE_LEVEL, int(math.log2(SUBLANES))):
        m = 1 << p
        ref = (t // (2 * m)) * (2 * m) + m - 1
        up = ((t >> p) & 1) == 1
        blocks.append(np.where(up, (j > ref) & (j <= t), (j > t) & (j <= ref)))
    blocks.append(j <= t)
    table = np.concatenate(blocks, axis=0).astype(np.float32)
    return jnp.asarray(np.concatenate([table, table], axis=1), dtype=BF16)


def _hgrn_prompt(layer, x, mods, n1g, w_in, lb_raw, gn_g, heads, chunk, step_rows):
    b, L, d = x.shape
    dk = d // heads
    sums = _sum_table(chunk)
    return pl.pallas_call(
        functools.partial(_hgrn_prompt_kernel, layer, heads),
        grid=(b, L // step_rows),
        in_specs=[pl.BlockSpec((1, step_rows, d), lambda i, c: (i, c, 0)),
                  _mod_spec(mods, layer),
                  _full(n1g.shape), _layer_spec(w_in, layer), _full(lb_raw.shape), _full(gn_g.shape),
                  _full((chunk, chunk)), _full(sums.shape)],
        out_specs=[pl.BlockSpec((1, step_rows, d), lambda i, c: (i, c, 0)),
                   pl.BlockSpec((1, heads, dk, dk), lambda i, c: (i, 0, 0, 0))],
        out_shape=[jax.ShapeDtypeStruct((b, L, d), F32),
                   jax.ShapeDtypeStruct((b, heads, dk, dk), F32)],
        scratch_shapes=[pltpu.VMEM((heads, dk, dk), F32), pltpu.VMEM((sums.shape[0], d), F32),
                        pltpu.VMEM((heads, int(math.log2(chunk)) + 2, chunk, dk), BF16),
                        pltpu.VMEM((heads, chunk, chunk), BF16)],
        compiler_params=_params("arbitrary", "arbitrary"),
        name=f"hgrn_prompt_{layer}",
    )(x, mods, n1g, w_in, lb_raw, gn_g, _level_table(chunk), sums)


def _post_kernel(layer, x_ref, a_ref, mod_ref, n2g_ref, wo_ref, wup_ref, wdn_ref, o_ref):
    d = x_ref.shape[2]
    mod = _mod_rows(mod_ref)
    g1, sh2, sc2, g2 = (mod[:, j * d:(j + 1) * d] for j in range(2, 6))
    x1 = x_ref[0] + g1 * _dot(a_ref[0].astype(BF16), wo_ref[0])
    h2 = _modulated_norm(x1, n2g_ref[layer:layer + 1, :], sh2, sc2).astype(BF16)
    y = None
    for c in range(0, wup_ref.shape[2], d):
        u = jnp.square(jnp.maximum(_dot(h2, wup_ref[0, :, c:c + d]), 0.0)).astype(BF16)
        part = _dot(u, wdn_ref[0, c:c + d, :])
        y = part if y is None else y + part
    o_ref[0] = x1 + g2 * y


def _post_sample_kernel(layer, x_ref, a_ref, mod_ref, n2g_ref, wo_ref, wup_ref, wdn_ref, o_ref,
                        x1_scr, h2_scr, y_scr):
    c = pl.program_id(0)
    d = x_ref.shape[1]
    mod = _mod_rows(mod_ref)

    @pl.when(c == 0)
    def _():
        g1, sh2, sc2 = (mod[:, j * d:(j + 1) * d] for j in range(2, 5))
        x1 = x_ref[...] + g1 * _dot(a_ref[...].astype(BF16), wo_ref[0])
        x1_scr[...] = x1
        h2_scr[...] = _modulated_norm(x1, n2g_ref[layer:layer + 1, :], sh2, sc2).astype(BF16)
        y_scr[...] = jnp.zeros_like(y_scr)

    u = jnp.square(jnp.maximum(_dot(h2_scr[...], wup_ref[0]), 0.0)).astype(BF16)
    y_scr[...] += _dot(u, wdn_ref[0])

    @pl.when(c == pl.num_programs(0) - 1)
    def _():
        o_ref[...] = x1_scr[...] + mod[:, 5 * d:6 * d] * y_scr[...]


def _post_sample(layer, x, a, mods, n2g, w_o, o_layer, w_up, w_dn, slab):
    r, d = x.shape
    ff = w_up.shape[2]
    return pl.pallas_call(
        functools.partial(_post_sample_kernel, layer),
        grid=(ff // slab,),
        in_specs=[_full((r, d)), _full((r, d)), _mod_spec(mods, layer), _full(n2g.shape),
                  _layer_spec(w_o, o_layer),
                  pl.BlockSpec((1, d, slab), lambda c: (layer, 0, c)),
                  pl.BlockSpec((1, slab, d), lambda c: (layer, c, 0))],
        out_specs=_full((r, d)),
        out_shape=jax.ShapeDtypeStruct((r, d), F32),
        scratch_shapes=[pltpu.VMEM((r, d), F32), pltpu.VMEM((r, d), BF16), pltpu.VMEM((r, d), F32)],
        compiler_params=_params("arbitrary"),
        name="post_mlp_sample",
    )(x, a, mods, n2g, w_o, w_up, w_dn)


def _post(layer, x, a, mods, n2g, w_o, o_layer, w_up, w_dn, tm):
    nb, L, d = x.shape
    return pl.pallas_call(
        functools.partial(_post_kernel, layer),
        grid=(nb, L // tm),
        in_specs=[pl.BlockSpec((1, tm, d), lambda i, j: (i, j, 0)),
                  pl.BlockSpec((1, tm, d), lambda i, j: (i, j, 0)),
                  _mod_spec(mods, layer), _full(n2g.shape),
                  _layer_spec(w_o, o_layer), _layer_spec(w_up, layer), _layer_spec(w_dn, layer)],
        out_specs=pl.BlockSpec((1, tm, d), lambda i, j: (i, j, 0)),
        out_shape=jax.ShapeDtypeStruct((nb, L, d), F32),
        compiler_params=_params("arbitrary", "arbitrary"),
        name="post_mlp",
    )(x, a, mods, n2g, w_o, w_up, w_dn)


def _kv_kernel(head_dim, window, x_ref, mod_ref, g_ref, w_ref, kg_ref, cos_ref, sin_ref,
               k_ref, v_ref, kx_ref, vx_ref):
    tm, d = x_ref.shape[1], x_ref.shape[2]
    mod = _mod_rows(mod_ref)
    hk = _modulated_norm(x_ref[0], g_ref[...], mod[:, 0:d], mod[:, d:2 * d])
    y = _dot(hk.astype(BF16), w_ref[...])
    n = y.shape[1] // 2
    cos, sin, kg = cos_ref[...], sin_ref[...], kg_ref[...]
    low = lax.broadcasted_iota(jnp.int32, (tm, LANES), 1) < head_dim
    k_blocks, v_blocks = _heads_norm_rope(y[:, 0:n], kg, cos, sin, head_dim, 1.0), []
    for j in range(n // LANES):
        kb = k_blocks[j]
        vb = y[:, n + j * LANES:n + (j + 1) * LANES]
        v_blocks.append(vb)
        kx_ref[0, :, j * LANES:(j + 1) * LANES] = kb.astype(BF16)
        kx_ref[0, :, n + j * LANES:n + (j + 1) * LANES] = pltpu.roll(kb, head_dim, 1).astype(BF16)
        vs = pltpu.roll(vb, head_dim, 1)
        variants = (jnp.where(low, vb, 0.0), jnp.where(low, 0.0, vs),
                    jnp.where(low, vs, 0.0), jnp.where(low, 0.0, vb))
        for i, var in enumerate(variants):
            off = (4 * j + i) * LANES
            vx_ref[0, off:off + LANES, :] = var.astype(BF16).T

    @pl.when(pl.program_id(1) == pl.num_programs(1) - 1)
    def _():
        for j in range(n // LANES):
            k_ref[0, :, j * LANES:(j + 1) * LANES] = k_blocks[j][tm - window:, :]
            v_ref[0, :, j * LANES:(j + 1) * LANES] = v_blocks[j][tm - window:, :]


def _kv(x, mods, g, w_kv, kg, cos, sin, head_dim, window, tm):
    nb, L, d = x.shape
    n = w_kv.shape[1] // 2
    rope_rows = tm if cos.shape[0] > 1 else 1
    rope_map = (lambda i, j: (j, 0)) if cos.shape[0] > 1 else (lambda i, j: (0, 0))
    last = lambda i, j: (i, 0, 0)
    tile = lambda i, j: (i, j, 0)
    return pl.pallas_call(
        functools.partial(_kv_kernel, head_dim, window),
        grid=(nb, L // tm),
        in_specs=[pl.BlockSpec((1, tm, d), tile), _mod_spec(mods, 0),
                  _full((1, d)), _full(w_kv.shape), _full((1, LANES)),
                  pl.BlockSpec((rope_rows, LANES), rope_map),
                  pl.BlockSpec((rope_rows, LANES), rope_map)],
        out_specs=[pl.BlockSpec((1, window, n), last), pl.BlockSpec((1, window, n), last),
                   pl.BlockSpec((1, tm, 2 * n), tile), pl.BlockSpec((1, 4 * n, tm), lambda i, j: (i, 0, j))],
        out_shape=[jax.ShapeDtypeStruct((nb, window, n), F32), jax.ShapeDtypeStruct((nb, window, n), F32),
                   jax.ShapeDtypeStruct((nb, L, 2 * n), BF16), jax.ShapeDtypeStruct((nb, 4 * n, L), BF16)],
        compiler_params=_params("arbitrary", "arbitrary"),
        name="kv_proj",
    )(x, mods, g, w_kv, kg, cos, sin)


def _q_proj(x, mod, n1g, wq, qg, cos, sin, head_dim, extra_scale=1.0):
    d = x.shape[1]
    h = _modulated_norm(x, n1g, mod[:, 0:d], mod[:, d:2 * d])
    y = _dot(h.astype(BF16), wq)
    return _heads_norm_rope(y, qg, cos, sin, head_dim, extra_scale / math.sqrt(head_dim))


def _attn_prompt_kernel(layer, att_layer, head_dim, kv_heads, nq, x_ref, mod_ref, n1g_ref, wq_ref, qg_ref,
                        cos_ref, sin_ref, *rest):
    k_refs, v_refs = rest[0:nq + 1], rest[nq + 1:2 * nq + 2]
    sink_ref, o_ref, s_scr, p_scr = rest[2 * nq + 2:]
    w = x_ref.shape[1] // nq
    nkv = kv_heads * head_dim
    q_blocks = _q_proj(x_ref[0], _mod_rows(mod_ref), n1g_ref[layer:layer + 1, :], wq_ref[0],
                       qg_ref[att_layer:att_layer + 1, :], cos_ref[...], sin_ref[...], head_dim, LOG2E)
    group = (2 * len(q_blocks)) // kv_heads
    key = lax.broadcasted_iota(jnp.int32, (w, 2 * w), 0)
    qry = lax.broadcasted_iota(jnp.int32, (w, 2 * w), 1)
    own = key <= (qry & (w - 1))
    first = lax.broadcasted_iota(jnp.int32, (1, 2 * w), 1) < w
    low = lax.broadcasted_iota(jnp.int32, (w, LANES), 1) < head_dim
    no_prev = jnp.where(pl.program_id(1) == 0, -jnp.inf, 0.0)
    tiles = [(u, kh, half) for u in range(nq) for kh in range(kv_heads) for half in range(2)]
    for n, (u, kh, half) in enumerate(tiles):
        ha, hb = kh * group + half, kh * group + 2 + half
        mask = (lambda t: jnp.where(low, t, 0.0)) if half == 0 else (lambda t: jnp.where(low, 0.0, t))
        rows = slice(u * w, (u + 1) * w)
        qcat = jnp.concatenate([mask(q_blocks[ha // 2][rows]), mask(q_blocks[hb // 2][rows])],
                               axis=0).astype(BF16)
        koff = (kh // 2) * LANES + (0 if kh % 2 == half else nkv)
        kcat = jnp.concatenate([k_refs[u][0, :, koff:koff + LANES], k_refs[u + 1][0, :, koff:koff + LANES]],
                               axis=0)
        s_t = _dot_nt(kcat, qcat)
        s_prev = s_t[:w] + no_prev if u == 0 else s_t[:w]
        s_scr[n] = jnp.where(own, s_t[w:], s_prev)
    for n, (u, kh, half) in enumerate(tiles):
        sc = s_scr[n]
        sink = jnp.where(first, sink_ref[att_layer, kh * group + half],
                         sink_ref[att_layer, kh * group + 2 + half]) * LOG2E
        m = jnp.maximum(jnp.max(sc, axis=0, keepdims=True), sink)
        p = jnp.exp2(sc - m)
        pn = p * (1.0 / (jnp.sum(p, axis=0, keepdims=True) + jnp.exp2(sink - m)))
        p_scr[n, 0:w] = jnp.where(own, pn, 0.0).astype(BF16)
        p_scr[n, w:2 * w] = jnp.where(own, 0.0, pn).astype(BF16)
    for u in range(nq):
        for kh in range(kv_heads):
            out_t = None
            for half in range(2):
                voff = (2 * kh + half) * LANES
                v_t = jnp.concatenate([v_refs[u + 1][0, voff:voff + LANES, :],
                                       v_refs[u][0, voff:voff + LANES, :]], axis=1)
                part = _dot(v_t, p_scr[(u * kv_heads + kh) * 2 + half])
                out_t = part if out_t is None else out_t + part
            for i in range(2):
                o_ref[0, u * w:(u + 1) * w, (2 * kh + i) * LANES:(2 * kh + i + 1) * LANES] = (
                    out_t[:, i * w:(i + 1) * w].T)


def _attn_prompt(layer, att_layer, x, mods, n1g, wq, qg, cos, sin, kx, vx, sinks, head_dim, kv_heads, w, nq):
    b, L, d = x.shape
    rows = lambda i, j: (i, j, 0)
    k_specs = [pl.BlockSpec((1, w, kx.shape[2]), lambda i, j, o=o: (i, jnp.maximum(nq * j + o, 0), 0))
               for o in range(-1, nq)]
    v_specs = [pl.BlockSpec((1, vx.shape[1], w), lambda i, j, o=o: (i, 0, jnp.maximum(nq * j + o, 0)))
               for o in range(-1, nq)]
    n_tiles = nq * 2 * kv_heads
    return pl.pallas_call(
        functools.partial(_attn_prompt_kernel, layer, att_layer, head_dim, kv_heads, nq),
        grid=(b, L // (nq * w)),
        in_specs=[pl.BlockSpec((1, nq * w, d), rows), _mod_spec(mods, layer),
                  _full(n1g.shape), _layer_spec(wq, att_layer), _full(qg.shape),
                  pl.BlockSpec((nq * w, LANES), lambda i, j: (j, 0)),
                  pl.BlockSpec((nq * w, LANES), lambda i, j: (j, 0))]
                 + k_specs + v_specs + [pl.BlockSpec(memory_space=pltpu.SMEM)],
        out_specs=pl.BlockSpec((1, nq * w, d), rows),
        out_shape=jax.ShapeDtypeStruct((b, L, d), F32),
        scratch_shapes=[pltpu.VMEM((n_tiles, w, 2 * w), F32), pltpu.VMEM((n_tiles, 2 * w, 2 * w), BF16)],
        compiler_params=_params("arbitrary", "arbitrary"),
        name="attn_prompt",
    )(x, mods, n1g, wq, qg, cos, sin, *([kx] * (nq + 1)), *([vx] * (nq + 1)), sinks)


def _proj_kernel(layer, x_ref, mod_ref, g_ref, w_ref, o_ref):
    d = x_ref.shape[1]
    mod = _mod_rows(mod_ref)
    h = _modulated_norm(x_ref[...], g_ref[layer:layer + 1, :], mod[:, 0:d], mod[:, d:2 * d])
    o_ref[...] = _dot(h.astype(BF16), w_ref[0])


def _proj(layer, x, mods, g, w, tn):
    r, d = x.shape
    n = w.shape[2]
    return pl.pallas_call(
        functools.partial(_proj_kernel, layer),
        grid=(n // tn,),
        in_specs=[_full((r, d)), _mod_spec(mods, layer), _full(g.shape),
                  pl.BlockSpec((1, d, tn), lambda j: (layer, 0, j))],
        out_specs=pl.BlockSpec((r, tn), lambda j: (0, j)),
        out_shape=jax.ShapeDtypeStruct((r, n), F32),
        compiler_params=_params("arbitrary"),
        name="proj_sample",
    )(x, mods, g, w)


def _q_sample_kernel(layer, att_layer, head_dim, x_ref, mod_ref, n1g_ref, wq_ref, qg_ref, cos_ref, sin_ref,
                     o_ref):
    blocks = _q_proj(x_ref[...], _mod_rows(mod_ref), n1g_ref[layer:layer + 1, :], wq_ref[0],
                     qg_ref[att_layer:att_layer + 1, :], cos_ref[...], sin_ref[...], head_dim)
    for j, blk in enumerate(blocks):
        o_ref[:, j * LANES:(j + 1) * LANES] = blk


def _q_sample(layer, att_layer, x, mods, n1g, wq, qg, cos, sin, head_dim):
    r, d = x.shape
    return pl.pallas_call(
        functools.partial(_q_sample_kernel, layer, att_layer, head_dim),
        grid=(1,),
        in_specs=[_full((r, d)), _mod_spec(mods, layer), _full(n1g.shape), _layer_spec(wq, att_layer),
                  _full(qg.shape), _full((1, LANES)), _full((1, LANES))],
        out_specs=_full((r, wq.shape[2])),
        out_shape=jax.ShapeDtypeStruct((r, wq.shape[2]), F32),
        compiler_params=_params("arbitrary"),
        name="q_sample",
    )(x, mods, n1g, wq, qg, cos, sin)


def _hgrn_sample_kernel(layer, heads, tb, steps, aliased, proj_ref, lbraw_ref, gng_ref, s_ref, *rest):
    o_ref, so_ref, stack_scr, acc_scr = rest[1:] if aliased else rest
    i = pl.program_id(0)
    rows = proj_ref.shape[0]
    d = proj_ref.shape[1] // 4
    dk = d // heads

    def update():
        @pl.when(i == 0)
        def _():
            lb = _lower_bound(lbraw_ref[...], layer)
            q, fg = _hgrn_gates(proj_ref[:, 0:2 * d], lb, d)
            for hh in range(heads):
                sl = slice(hh * dk, (hh + 1) * dk)
                ft = fg[:, sl].T
                hi = ft.astype(BF16)
                stack_scr[hh, 0:dk] = hi
                stack_scr[hh, dk:2 * dk] = (ft - hi.astype(F32)).astype(BF16)
                stack_scr[hh, 2 * dk:3 * dk] = (1.0 - ft).astype(BF16)
                stack_scr[hh, 3 * dk:4 * dk] = q[:, sl].T.astype(BF16)

        token_row = lax.broadcasted_iota(jnp.int32, (rows, dk), 0)
        sub = lax.broadcasted_iota(jnp.int32, (tb, dk), 0)
        base = pl.multiple_of(i * tb, tb)
        v_rows = proj_ref[pl.ds(base, tb), 2 * d:3 * d]
        o_rows = [jnp.zeros((tb, dk), F32) for _ in range(heads)]
        for t0 in range(0, tb, 2):
            onehot = jnp.concatenate([jnp.where(token_row == base + t, 1.0, 0.0) for t in (t0, t0 + 1)],
                                     axis=1).astype(BF16)
            for hh in range(heads):
                bc2 = _dot(stack_scr[hh], onehot)
                for t in (t0, t0 + 1):
                    bc = bc2[:, (t - t0) * dk:(t - t0 + 1) * dk]
                    f_b = bc[0:dk] + bc[dk:2 * dk]
                    k_b = bc[2 * dk:3 * dk]
                    q_b = bc[3 * dk:4 * dk]
                    s_new = f_b * s_ref[0, t, hh] + k_b * v_rows[t:t + 1, hh * dk:(hh + 1) * dk]
                    so_ref[0, t, hh] = s_new
                    o_rows[hh] = jnp.where(sub == t, jnp.sum(q_b * s_new, axis=0, keepdims=True), o_rows[hh])
        for hh in range(heads):
            acc_scr[pl.ds(base, tb), hh * dk:(hh + 1) * dk] = o_rows[hh]

        @pl.when(i == steps - 1)
        def _():
            gng = gng_ref[layer:layer + 1, :]
            for hh in range(heads):
                sl = slice(hh * dk, (hh + 1) * dk)
                o_ref[:, sl] = _gated_group_norm(acc_scr[:, sl], gng,
                                                 proj_ref[:, 3 * d + hh * dk:3 * d + (hh + 1) * dk])

    if aliased:
        update()
    else:
        pl.when(i < steps)(update)

        @pl.when(i >= steps)
        def _():
            so_ref[...] = jnp.zeros_like(so_ref)


def _hgrn_sample(layer, proj, lb_raw, gn_g, state, new_state, tb):
    n_layers, nb, heads, dk, dv = state.shape
    d = proj.shape[1] // 4
    steps = nb // tb
    aliased = new_state is not None
    assert aliased == (layer > 0) and tb == SUBLANES
    if aliased:
        grid = (steps,)
        s_in = s_out = pl.BlockSpec((1, tb, heads, dk, dv), lambda i: (layer, i, 0, 0, 0))
    else:
        grid = (n_layers * steps,)
        s_in = pl.BlockSpec((1, tb, heads, dk, dv), lambda i: (0, jnp.minimum(i, steps - 1), 0, 0, 0))
        s_out = pl.BlockSpec((1, tb, heads, dk, dv), lambda i: (i // steps, i % steps, 0, 0, 0))
    in_specs = [_full(proj.shape), _full(lb_raw.shape), _full(gn_g.shape), s_in]
    args = [proj, lb_raw, gn_g, state]
    if aliased:
        in_specs.append(pl.BlockSpec(memory_space=pl.ANY))
        args.append(new_state)
    return pl.pallas_call(
        functools.partial(_hgrn_sample_kernel, layer, heads, tb, steps, aliased),
        grid=grid,
        in_specs=in_specs,
        out_specs=[_full((nb, d)), s_out],
        out_shape=[jax.ShapeDtypeStruct((nb, d), F32), jax.ShapeDtypeStruct(state.shape, F32)],
        scratch_shapes=[pltpu.VMEM((heads, 4 * dk, nb), BF16), pltpu.VMEM((nb, d), F32)],
        input_output_aliases={4: 1} if aliased else {},
        compiler_params=_params("arbitrary"),
        name=f"hgrn_sample_{layer}",
    )(*args)


def _cache_roll_kernel(ck_ref, cv_ref, kn_ref, vn_ref, ko_ref, vo_ref, kb_ref, vb_ref):
    w = ck_ref.shape[1]
    ko_ref[:, 0:w - 1, :] = ck_ref[:, 1:w, :]
    ko_ref[:, w - 1:w, :] = kn_ref[...]
    vo_ref[:, 0:w - 1, :] = cv_ref[:, 1:w, :]
    vo_ref[:, w - 1:w, :] = vn_ref[...]
    kb_ref[...] = ko_ref[...].astype(BF16)
    vb_ref[...] = vo_ref[...].astype(BF16)


def _cache_roll(ck, cv, kn, vn, tb):
    nb, w, n = ck.shape
    blk = pl.BlockSpec((tb, w, n), lambda i: (i, 0, 0))
    new = pl.BlockSpec((tb, 1, n), lambda i: (i, 0, 0))
    return pl.pallas_call(
        _cache_roll_kernel,
        grid=(nb // tb,),
        in_specs=[blk, blk, new, new],
        out_specs=[blk, blk, blk, blk],
        out_shape=[jax.ShapeDtypeStruct(ck.shape, F32), jax.ShapeDtypeStruct(cv.shape, F32),
                   jax.ShapeDtypeStruct(ck.shape, BF16), jax.ShapeDtypeStruct(cv.shape, BF16)],
        compiler_params=_params("arbitrary"),
        name="cache_roll",
    )(ck, cv, kn.reshape(nb, 1, n), vn.reshape(nb, 1, n))


def _attn_sample_kernel(att_layer, tb, q_ref, k_ref, v_ref, sink_ref, o_ref):
    heads, head_dim = q_ref.shape[1], q_ref.shape[2]
    nkv = k_ref.shape[2]
    group = heads // (nkv // head_dim)
    r = lax.broadcasted_iota(jnp.int32, (heads, nkv), 0) // group
    c = lax.broadcasted_iota(jnp.int32, (heads, nkv), 1) // head_dim
    own = r == c
    sink = sink_ref[:, att_layer:att_layer + 1][None]
    q = q_ref[...]
    qe = jnp.where(own[None], jnp.concatenate([q] * (nkv // head_dim), axis=-1), 0.0).astype(BF16)
    s = jnp.einsum("thc,tjc->thj", qe, k_ref[...], preferred_element_type=F32)
    m = jnp.maximum(jnp.max(s, axis=-1, keepdims=True), sink)
    p = jnp.exp(s - m)
    denom = jnp.sum(p, axis=-1, keepdims=True) + jnp.exp(sink - m)
    pv = jnp.where(own[None], jnp.einsum("thj,tjc->thc", p.astype(BF16), v_ref[...],
                                         preferred_element_type=F32), 0.0)
    o = pv[:, :, 0:head_dim]
    for j in range(1, nkv // head_dim):
        o = o + pv[:, :, j * head_dim:(j + 1) * head_dim]
    o_ref[...] = o / denom


def _attn_sample(att_layer, q3, k, v, sinks_t, tb):
    nb, heads, head_dim = q3.shape
    w, nkv = k.shape[1], k.shape[2]
    return pl.pallas_call(
        functools.partial(_attn_sample_kernel, att_layer, tb),
        grid=(nb // tb,),
        in_specs=[pl.BlockSpec((tb, heads, head_dim), lambda i: (i, 0, 0)),
                  pl.BlockSpec((tb, w, nkv), lambda i: (i, 0, 0)),
                  pl.BlockSpec((tb, w, nkv), lambda i: (i, 0, 0)),
                  _full(sinks_t.shape)],
        out_specs=pl.BlockSpec((tb, heads, head_dim), lambda i: (i, 0, 0)),
        out_shape=jax.ShapeDtypeStruct(q3.shape, F32),
        compiler_params=_params("arbitrary"),
        name="attn_sample",
    )(q3, k, v, sinks_t)


def _rope_tables(pos, head_dim):
    half = head_dim // 2
    inv = ROPE_THETA ** (-jnp.arange(half, dtype=F32) / half)
    ang = pos.astype(F32)[:, None] * inv[None, :]
    cos, sin = jnp.cos(ang), jnp.sin(ang)
    reps = LANES // head_dim
    return (jnp.tile(jnp.concatenate([cos, cos], axis=1), (1, reps)),
            jnp.tile(jnp.concatenate([-sin, sin], axis=1), (1, reps)))


def kernel(x_prompt, x_sample, c_prompt, c_sample, state_hgrn, cache_k, cache_v, w_ada, b_ada, norm1_g, norm2_g, hg_w_in, hg_w_out, hg_lower_bounds, hg_gn_g, kv_w_ada, kv_b_ada, kv_norm_g, w_kv, k_norm_g, w_q, q_norm_g, sinks, w_o, w_up, w_down):
    bp, Lp, d = x_prompt.shape
    bs = x_sample.shape[0]
    depth = w_ada.shape[0]
    n_a = hg_w_in.shape[0]
    heads = state_hgrn.shape[2]
    window, kv_heads, head_dim = cache_k.shape[1], cache_k.shape[2], cache_k.shape[3]
    assert LANES == 2 * head_dim and kv_heads % 2 == 0 and Lp % window == 0
    assert window & (window - 1) == 0 and (d // head_dim) // kv_heads == 4
    nkv = kv_heads * head_dim
    reps = LANES // head_dim
    bf = lambda t: t.astype(BF16)

    pad = (-(bs + bp)) % 16
    c_all = jnp.concatenate([c_sample, c_prompt, jnp.zeros((pad, d), F32)], axis=0)
    mods_s, mods_p = _ada(c_all, bs, bp, w_ada, b_ada, T.ada_cols)
    kv_mods_s, kv_mods_p = _ada(c_all, bs, bp, kv_w_ada[None], kv_b_ada[None], T.kv_ada_cols)

    cos_p, sin_p = _rope_tables(jnp.arange(Lp), head_dim)
    cos_s, sin_s = _rope_tables(jnp.full((1,), PAST_LEN), head_dim)
    kg = jnp.tile(k_norm_g, reps)[None]
    qg = jnp.tile(q_norm_g, (1, reps))
    sinks_t = sinks.T
    w_kv_b, w_in_b, w_out_b, w_q_b, w_o_b = bf(w_kv), bf(hg_w_in), bf(hg_w_out), bf(w_q), bf(w_o)
    w_up_b, w_dn_b = bf(w_up), bf(w_down)

    xp = x_prompt
    xs = x_sample.reshape(bs, d)
    hg_p, hg_s = [], None
    k_p = v_p = kx_p = vx_p = k_s = v_s = None
    for l in range(depth):
        if l == n_a:
            k_p, v_p, kx_p, vx_p = _kv(xp, kv_mods_p, kv_norm_g[None], w_kv_b, kg, cos_p, sin_p,
                                       head_dim, window, T.kv_rows)
            k_n, v_n, _, _ = _kv(xs[None], kv_mods_s, kv_norm_g[None], w_kv_b, kg, cos_s, sin_s,
                                 head_dim, bs, bs)
            k_s, v_s, kb_s, vb_s = _cache_roll(cache_k.reshape(bs, window, nkv),
                                               cache_v.reshape(bs, window, nkv), k_n[0], v_n[0], T.cache_tokens)
        if l < n_a:
            w_mix_b, mix_layer = w_out_b, l
            a_p, s_p = _hgrn_prompt(l, xp, mods_p, norm1_g, w_in_b, hg_lower_bounds, hg_gn_g, heads,
                                    T.hgrn_chunk, T.hgrn_rows)
            hg_p.append(s_p)
            proj_s = _proj(l, xs, mods_s, norm1_g, w_in_b, T.proj_cols)
            a_s, hg_s = _hgrn_sample(l, proj_s, hg_lower_bounds, hg_gn_g, state_hgrn, hg_s, SUBLANES)
        else:
            j = l - n_a
            w_mix_b, mix_layer = w_o_b, j
            a_p = _attn_prompt(l, j, xp, mods_p, norm1_g, w_q_b, qg, cos_p, sin_p, kx_p, vx_p, sinks,
                               head_dim, kv_heads, window, T.attn_blocks)
            q_s = _q_sample(l, j, xs, mods_s, norm1_g, w_q_b, qg, cos_s, sin_s, head_dim)
            a_s = _attn_sample(j, q_s.reshape(bs, d // head_dim, head_dim), kb_s, vb_s, sinks_t,
                               T.attn_sample_tokens)
            a_s = a_s.reshape(bs, d)
        xp = _post(l, xp, a_p, mods_p, norm2_g, w_mix_b, mix_layer, w_up_b, w_dn_b, T.mlp_rows)
        xs = _post_sample(l, xs, a_s, mods_s, norm2_g, w_mix_b, mix_layer, w_up_b, w_dn_b, T.mlp_sample_slab)

    shape4 = lambda t: t.reshape(t.shape[0], window, kv_heads, head_dim)
    return (xp, xs.reshape(bs, 1, d), jnp.stack(hg_p), shape4(k_p), shape4(v_p), hg_s,
            shape4(k_s), shape4(v_s))
```

```python
import functools
import math
from typing import NamedTuple

import numpy as np
import jax
import jax.numpy as jnp
from jax import lax
from jax.experimental import pallas as pl
from jax.experimental.pallas import tpu as pltpu

F32 = jnp.float32
BF16 = jnp.bfloat16

PAST_LEN = 8192
ROPE_THETA = 10000.0
EPS = 1e-6
LOG2E = 1.4426950408889634
LANES = 128
SUBLANES = 8
FIRST_TABLE_LEVEL = 2
VMEM_LIMIT = 56 * 1024 * 1024


class _Tiles(NamedTuple):
    ada_cols: int = 3072
    kv_ada_cols: int = 1024
    hgrn_chunk: int = 128
    hgrn_rows: int = 512
    mlp_rows: int = 512
    kv_rows: int = 512
    attn_blocks: int = 8
    proj_cols: int = 1024
    cache_tokens: int = 8
    attn_sample_tokens: int = 16
    mlp_sample_slab: int = 1024


T = _Tiles()

NT_DIMS = (((1,), (1,)), ((), ()))
TN_DIMS = (((0,), (0,)), ((), ()))


def _dot(a, b):
    return jnp.dot(a, b, preferred_element_type=F32)


def _dot_nt(a, b):
    return lax.dot_general(a, b, NT_DIMS, preferred_element_type=F32)


def _sigmoid(x):
    return 1.0 / (1.0 + jnp.exp(-x))


def _silu(x):
    return x * (0.5 + 0.5 * jnp.tanh(0.5 * x))


def _rms(x, g):
    ms = jnp.mean(x * x, axis=-1, keepdims=True)
    return x * lax.rsqrt(ms + EPS) * g


def _modulated_norm(x, gain, shift, scale):
    return _rms(x, gain) * (1.0 + scale) + shift


def _params(*sem):
    return pltpu.CompilerParams(dimension_semantics=sem, vmem_limit_bytes=VMEM_LIMIT)


def _full(shape):
    n = len(shape)
    return pl.BlockSpec(shape, lambda *_: (0,) * n)


def _ada_kernel(c_ref, w_ref, b_ref, os_ref, op_ref):
    bs, bp = os_ref.shape[1], op_ref.shape[1]
    c = c_ref[...]
    a = _silu(c).astype(BF16)
    res = _dot(a, w_ref[0].astype(BF16)) + b_ref[0]
    os_ref[0] = res[0:bs]
    for r in range(bp):
        op_ref[0, r] = jnp.broadcast_to(res[bs + r:bs + r + 1], op_ref.shape[2:])


def _ada(c_all, bs, bp, w, b, tn):
    nl, d, n = w.shape
    r = c_all.shape[0]
    return pl.pallas_call(
        _ada_kernel,
        grid=(nl, n // tn),
        in_specs=[pl.BlockSpec((r, d), lambda l, j: (0, 0)),
                  pl.BlockSpec((1, d, tn), lambda l, j: (l, 0, j)),
                  pl.BlockSpec((1, 1, tn), lambda l, j: (l, 0, j))],
        out_specs=[pl.BlockSpec((1, bs, tn), lambda l, j: (l, 0, j)),
                   pl.BlockSpec((1, bp, SUBLANES, tn), lambda l, j: (l, 0, 0, j))],
        out_shape=[jax.ShapeDtypeStruct((nl, bs, n), F32),
                   jax.ShapeDtypeStruct((nl, bp, SUBLANES, n), F32)],
        compiler_params=_params("arbitrary", "arbitrary"),
        name="ada",
    )(c_all, w, b.reshape(nl, 1, n))


def _mod_rows(mod_ref):
    return mod_ref[0, 0, 0:1, :] if len(mod_ref.shape) == 4 else mod_ref[0]


def _mod_spec(mods, layer):
    if mods.ndim == 4:
        return pl.BlockSpec((1, 1) + mods.shape[2:], lambda i, *_: (layer, i, 0, 0))
    return pl.BlockSpec((1,) + mods.shape[1:], lambda *_: (layer, 0, 0))


def _layer_spec(stack, layer):
    zeros = (0,) * (stack.ndim - 1)
    return pl.BlockSpec((1,) + stack.shape[1:], lambda *_: (layer,) + zeros, pipeline_mode=pl.Buffered(1))


def _group_mean_matrix(group, width):
    r = lax.broadcasted_iota(jnp.int32, (width, width), 0) // group
    c = lax.broadcasted_iota(jnp.int32, (width, width), 1) // group
    return jnp.where(r == c, 1.0 / group, 0.0).astype(BF16)


def _heads_norm_rope(y, gain, cos, sin, head_dim, out_scale):
    rows, n = y.shape
    width = 2 * LANES if n % (2 * LANES) == 0 else LANES
    mean = _group_mean_matrix(head_dim, width)
    gain = gain * out_scale
    half = head_dim // 2
    first = (lax.broadcasted_iota(jnp.int32, (rows, LANES), 1) % head_dim) < half
    out = []
    for j in range(0, n, width):
        ms = _dot(jnp.square(y[:, j:j + width]).astype(BF16), mean)
        for i in range(0, width, LANES):
            yn = y[:, j + i:j + i + LANES] * lax.rsqrt(ms[:, i:i + LANES] + EPS) * gain
            rot = jnp.where(first, pltpu.roll(yn, LANES - half, 1), pltpu.roll(yn, half, 1))
            out.append(yn * cos + rot * sin)
    return out


def _lower_bound(raw, layer):
    m = jnp.max(raw, axis=0, keepdims=True)
    e = jnp.exp(raw - m)
    sm = e / jnp.sum(e, axis=0, keepdims=True)
    acc = sm[0:1]
    for j in range(1, layer + 1):
        acc = acc + sm[j:j + 1]
    return acc - sm[0:1]


def _hgrn_gates(proj, lb, d):
    qa = proj[:, 0:d]
    q = _silu(qa)
    fg = lb + (1.0 - lb) * _sigmoid(proj[:, d:2 * d])
    return q, fg


def _gated_group_norm(o, gain, gate_pre):
    ms = jnp.mean(o * o, axis=-1, keepdims=True)
    return o * lax.rsqrt(ms + EPS) * gain * _silu(gate_pre)


def _hgrn_prompt_kernel(layer, heads, x_ref, mod_ref, n1g_ref, win_ref, lbraw_ref, gng_ref,
                        lev_ref, sums_ref, o_ref, s_ref, st_scr, d_scr, xs_scr, att_scr):
    c = pl.program_id(1)
    total, d = x_ref.shape[1], x_ref.shape[2]
    rows = lev_ref.shape[0]
    dk = d // heads
    n_levels = int(math.log2(rows))

    @pl.when(c == 0)
    def _():
        st_scr[...] = jnp.zeros_like(st_scr)

    mod = _mod_rows(mod_ref)
    h = _modulated_norm(x_ref[0], n1g_ref[layer:layer + 1, :], mod[:, 0:d], mod[:, d:2 * d])
    proj_all = _dot(h.astype(BF16), win_ref[0])
    lb = _lower_bound(lbraw_ref[...], layer)
    q_all, fg_all = _hgrn_gates(proj_all, lb, d)
    lf2_all = jnp.log2(fg_all)
    n_table = d_scr.shape[0] // rows - 1 + FIRST_TABLE_LEVEL - 1
    cum_row = d_scr.shape[0] - rows
    lev = lev_ref[...]
    row_id = lax.broadcasted_iota(jnp.int32, (rows, dk), 0)
    upper = [((row_id >> p) & 1) == 1 for p in range(n_table + 1)]
    tile_row = lax.broadcasted_iota(jnp.int32, (rows // SUBLANES, SUBLANES, dk), 1) & 3
    col = lax.broadcasted_iota(jnp.int32, (1, rows), 1)
    gng = gng_ref[layer:layer + 1, :]

    for r0 in range(0, total, rows):
        proj, q, fg = (t[r0:r0 + rows] for t in (proj_all, q_all, fg_all))
        k = 1.0 - fg
        lf2 = lf2_all[r0:r0 + rows]
        hi = lf2.astype(BF16)
        lo = (lf2 - hi.astype(F32)).astype(BF16)
        d_scr[...] = _dot(sums_ref[...], jnp.concatenate([hi, lo], axis=0))

        for hh in range(heads):
            sl = slice(hh * dk, (hh + 1) * dk)
            q_h, k_h = q[:, sl], k[:, sl]
            cum = d_scr[cum_row:cum_row + rows, sl]
            f_h = fg[:, sl]
            xs_scr[hh, 0] = jnp.where(upper[0], q_h * f_h, k_h).astype(BF16)
            f_t = f_h.reshape(rows // SUBLANES, SUBLANES, dk)
            e1 = jnp.where(tile_row == 3, f_t * pltpu.roll(f_t, 1, 1),
                           jnp.where(tile_row == 2, f_t,
                                     jnp.where(tile_row == 1, 1.0, pltpu.roll(f_t, SUBLANES - 1, 1))))
            xs_scr[hh, 1] = (jnp.where(upper[1], q_h, k_h) * e1.reshape(rows, dk)).astype(BF16)
            cum_last = cum[rows - 1:rows, :]
            edge = 2 << n_table
            q_dec = [q_h[0:edge] * jnp.exp2(cum[0:edge])]
            k_dec = [k_h[rows - edge:rows] * jnp.exp2(cum_last - cum[rows - edge:rows])]
            for p in range(FIRST_TABLE_LEVEL, n_levels):
                m = 1 << p
                if p <= n_table:
                    e = jnp.exp2(d_scr[(p - FIRST_TABLE_LEVEL) * rows:(p - FIRST_TABLE_LEVEL + 1) * rows, sl])
                    xs_scr[hh, p] = (jnp.where(upper[p], q_h, k_h) * e).astype(BF16)
                    continue
                pieces = []
                for r in range(0, rows, m):
                    ref = (r // (2 * m)) * 2 * m + m - 1
                    if (r // m) % 2:
                        pieces.append(q_h[r:r + m] * jnp.exp2(cum[r:r + m] - cum[ref:ref + 1]))
                    else:
                        pieces.append(k_h[r:r + m] * jnp.exp2(cum[ref:ref + 1] - cum[r:r + m]))
                xs_scr[hh, p] = jnp.concatenate(pieces, axis=0).astype(BF16)
                q_dec.append(pieces[1] * jnp.exp2(cum[m - 1:m]))
                k_dec.insert(0, pieces[rows // m - 2] * jnp.exp2(cum_last - cum[rows - m - 1:rows - m]))
            xs_scr[hh, n_levels] = jnp.concatenate(q_dec, axis=0).astype(BF16)
            xs_scr[hh, n_levels + 1] = jnp.concatenate(k_dec, axis=0).astype(BF16)

        for hh in range(heads):
            xs = xs_scr[hh, 0]
            att = jnp.where(lev == 0, _dot_nt(xs, xs), 0.0)
            for p in range(1, n_levels):
                m = 1 << p
                xs = xs_scr[hh, p]
                pp = _dot_nt(xs, xs)
                if p <= n_table:
                    att = jnp.where(lev == p, pp, att)
                else:
                    att = jnp.concatenate(
                        [jnp.where((col >= r - m) & (col < r), pp[r:r + m], att[r:r + m]) if (r // m) % 2
                         else att[r:r + m] for r in range(0, rows, m)], axis=0)
            att_scr[hh] = att.astype(BF16)

        for hh in range(heads):
            sl = slice(hh * dk, (hh + 1) * dk)
            v_f = proj[:, 2 * d + hh * dk:2 * d + (hh + 1) * dk]
            v_h = v_f.astype(BF16)
            cum_last = d_scr[cum_row + rows - 1:cum_row + rows, sl]
            st = st_scr[hh]
            o_h = (_dot(jnp.concatenate([att_scr[hh], xs_scr[hh, n_levels]], axis=1),
                        jnp.concatenate([v_h, st.T.astype(BF16)], axis=0))
                   + jnp.sum(q[:, sl] * k[:, sl], axis=-1, keepdims=True) * v_f)
            st_scr[hh] = st * jnp.exp2(cum_last) + lax.dot_general(
                v_h, xs_scr[hh, n_levels + 1], TN_DIMS, preferred_element_type=F32)
            g_pre = proj[:, 3 * d + hh * dk:3 * d + (hh + 1) * dk]
            o_ref[0, r0:r0 + rows, sl] = _gated_group_norm(o_h, gng, g_pre)

    @pl.when(c == pl.num_programs(1) - 1)
    def _():
        for hh in range(heads):
            s_ref[0, hh] = st_scr[hh].T


def _level_table(rows):
    t = np.arange(rows)[:, None]
    s = np.arange(rows)[None, :]
    x = t ^ s
    lev = np.where(t > s, np.floor(np.log2(np.maximum(x, 1))).astype(np.int32), np.where(t == s, -1, -2))
    return jnp.asarray(lev, dtype=jnp.int32)


def _sum_table(rows):
    t = np.arange(rows)[:, None]
    j = np.arange(rows)[None, :]
    blocks = []
    for p in range(FIRST_TABLE_LEVEL, int(math.log2(SUBLANES))):
        m = 1 << p
        ref = (t // (2 * m)) * (2 * m) + m - 1
        up = ((t >> p) & 1) == 1
        blocks.append(np.where(up, (j > ref) & (j <= t), (j > t) & (j <= ref)))
    blocks.append(j <= t)
    table = np.concatenate(blocks, axis=0).astype(np.float32)
    return jnp.asarray(np.concatenate([table, table], axis=1), dtype=BF16)


def _hgrn_prompt(layer, x, mods, n1g, w_in, lb_raw, gn_g, heads, chunk, step_rows):
    b, L, d = x.shape
    dk = d // heads
    sums = _sum_table(chunk)
    return pl.pallas_call(
        functools.partial(_hgrn_prompt_kernel, layer, heads),
        grid=(b, L // step_rows),
        in_specs=[pl.BlockSpec((1, step_rows, d), lambda i, c: (i, c, 0)),
                  _mod_spec(mods, layer),
                  _full(n1g.shape), _layer_spec(w_in, layer), _full(lb_raw.shape), _full(gn_g.shape),
                  _full((chunk, chunk)), _full(sums.shape)],
        out_specs=[pl.BlockSpec((1, step_rows, d), lambda i, c: (i, c, 0)),
                   pl.BlockSpec((1, heads, dk, dk), lambda i, c: (i, 0, 0, 0))],
        out_shape=[jax.ShapeDtypeStruct((b, L, d), F32),
                   jax.ShapeDtypeStruct((b, heads, dk, dk), F32)],
        scratch_shapes=[pltpu.VMEM((heads, dk, dk), F32), pltpu.VMEM((sums.shape[0], d), F32),
                        pltpu.VMEM((heads, int(math.log2(chunk)) + 2, chunk, dk), BF16),
                        pltpu.VMEM((heads, chunk, chunk), BF16)],
        compiler_params=_params("arbitrary", "arbitrary"),
        name=f"hgrn_prompt_{layer}",
    )(x, mods, n1g, w_in, lb_raw, gn_g, _level_table(chunk), sums)


def _post_kernel(layer, x_ref, a_ref, mod_ref, n2g_ref, wo_ref, wup_ref, wdn_ref, o_ref):
    d = x_ref.shape[2]
    mod = _mod_rows(mod_ref)
    g1, sh2, sc2, g2 = (mod[:, j * d:(j + 1) * d] for j in range(2, 6))
    x1 = x_ref[0] + g1 * _dot(a_ref[0].astype(BF16), wo_ref[0])
    h2 = _modulated_norm(x1, n2g_ref[layer:layer + 1, :], sh2, sc2).astype(BF16)
    y = None
    for c in range(0, wup_ref.shape[2], d):
        u = jnp.square(jnp.maximum(_dot(h2, wup_ref[0, :, c:c + d]), 0.0)).astype(BF16)
        part = _dot(u, wdn_ref[0, c:c + d, :])
        y = part if y is None else y + part
    o_ref[0] = x1 + g2 * y


def _post_sample_kernel(layer, x_ref, a_ref, mod_ref, n2g_ref, wo_ref, wup_ref, wdn_ref, o_ref,
                        x1_scr, h2_scr, y_scr):
    c = pl.program_id(0)
    d = x_ref.shape[1]
    mod = _mod_rows(mod_ref)

    @pl.when(c == 0)
    def _():
        g1, sh2, sc2 = (mod[:, j * d:(j + 1) * d] for j in range(2, 5))
        x1 = x_ref[...] + g1 * _dot(a_ref[...].astype(BF16), wo_ref[0])
        x1_scr[...] = x1
        h2_scr[...] = _modulated_norm(x1, n2g_ref[layer:layer + 1, :], sh2, sc2).astype(BF16)
        y_scr[...] = jnp.zeros_like(y_scr)

    u = jnp.square(jnp.maximum(_dot(h2_scr[...], wup_ref[0]), 0.0)).astype(BF16)
    y_scr[...] += _dot(u, wdn_ref[0])

    @pl.when(c == pl.num_programs(0) - 1)
    def _():
        o_ref[...] = x1_scr[...] + mod[:, 5 * d:6 * d] * y_scr[...]


def _post_sample(layer, x, a, mods, n2g, w_o, o_layer, w_up, w_dn, slab):
    r, d = x.shape
    ff = w_up.shape[2]
    return pl.pallas_call(
        functools.partial(_post_sample_kernel, layer),
        grid=(ff // slab,),
        in_specs=[_full((r, d)), _full((r, d)), _mod_spec(mods, layer), _full(n2g.shape),
                  _layer_spec(w_o, o_layer),
                  pl.BlockSpec((1, d, slab), lambda c: (layer, 0, c)),
                  pl.BlockSpec((1, slab, d), lambda c: (layer, c, 0))],
        out_specs=_full((r, d)),
        out_shape=jax.ShapeDtypeStruct((r, d), F32),
        scratch_shapes=[pltpu.VMEM((r, d), F32), pltpu.VMEM((r, d), BF16), pltpu.VMEM((r, d), F32)],
        compiler_params=_params("arbitrary"),
        name="post_mlp_sample",
    )(x, a, mods, n2g, w_o, w_up, w_dn)


def _post(layer, x, a, mods, n2g, w_o, o_layer, w_up, w_dn, tm):
    nb, L, d = x.shape
    return pl.pallas_call(
        functools.partial(_post_kernel, layer),
        grid=(nb, L // tm),
        in_specs=[pl.BlockSpec((1, tm, d), lambda i, j: (i, j, 0)),
                  pl.BlockSpec((1, tm, d), lambda i, j: (i, j, 0)),
                  _mod_spec(mods, layer), _full(n2g.shape),
                  _layer_spec(w_o, o_layer), _layer_spec(w_up, layer), _layer_spec(w_dn, layer)],
        out_specs=pl.BlockSpec((1, tm, d), lambda i, j: (i, j, 0)),
        out_shape=jax.ShapeDtypeStruct((nb, L, d), F32),
        compiler_params=_params("arbitrary", "arbitrary"),
        name="post_mlp",
    )(x, a, mods, n2g, w_o, w_up, w_dn)


def _kv_kernel(head_dim, window, x_ref, mod_ref, g_ref, w_ref, kg_ref, cos_ref, sin_ref,
               k_ref, v_ref, kx_ref, vx_ref):
    tm, d = x_ref.shape[1], x_ref.shape[2]
    mod = _mod_rows(mod_ref)
    hk = _modulated_norm(x_ref[0], g_ref[...], mod[:, 0:d], mod[:, d:2 * d])
    y = _dot(hk.astype(BF16), w_ref[...])
    n = y.shape[1] // 2
    cos, sin, kg = cos_ref[...], sin_ref[...], kg_ref[...]
    low = lax.broadcasted_iota(jnp.int32, (tm, LANES), 1) < head_dim
    k_blocks, v_blocks = _heads_norm_rope(y[:, 0:n], kg, cos, sin, head_dim, 1.0), []
    for j in range(n // LANES):
        kb = k_blocks[j]
        vb = y[:, n + j * LANES:n + (j + 1) * LANES]
        v_blocks.append(vb)
        kx_ref[0, :, j * LANES:(j + 1) * LANES] = kb.astype(BF16)
        kx_ref[0, :, n + j * LANES:n + (j + 1) * LANES] = pltpu.roll(kb, head_dim, 1).astype(BF16)
        vs = pltpu.roll(vb, head_dim, 1)
        variants = (jnp.where(low, vb, 0.0), jnp.where(low, 0.0, vs),
                    jnp.where(low, vs, 0.0), jnp.where(low, 0.0, vb))
        for i, var in enumerate(variants):
            off = (4 * j + i) * LANES
            vx_ref[0, off:off + LANES, :] = var.astype(BF16).T

    @pl.when(pl.program_id(1) == pl.num_programs(1) - 1)
    def _():
        for j in range(n // LANES):
            k_ref[0, :, j * LANES:(j + 1) * LANES] = k_blocks[j][tm - window:, :]
            v_ref[0, :, j * LANES:(j + 1) * LANES] = v_blocks[j][tm - window:, :]


def _kv(x, mods, g, w_kv, kg, cos, sin, head_dim, window, tm):
    nb, L, d = x.shape
    n = w_kv.shape[1] // 2
    rope_rows = tm if cos.shape[0] > 1 else 1
    rope_map = (lambda i, j: (j, 0)) if cos.shape[0] > 1 else (lambda i, j: (0, 0))
    last = lambda i, j: (i, 0, 0)
    tile = lambda i, j: (i, j, 0)
    return pl.pallas_call(
        functools.partial(_kv_kernel, head_dim, window),
        grid=(nb, L // tm),
        in_specs=[pl.BlockSpec((1, tm, d), tile), _mod_spec(mods, 0),
                  _full((1, d)), _full(w_kv.shape), _full((1, LANES)),
                  pl.BlockSpec((rope_rows, LANES), rope_map),
                  pl.BlockSpec((rope_rows, LANES), rope_map)],
        out_specs=[pl.BlockSpec((1, window, n), last), pl.BlockSpec((1, window, n), last),
                   pl.BlockSpec((1, tm, 2 * n), tile), pl.BlockSpec((1, 4 * n, tm), lambda i, j: (i, 0, j))],
        out_shape=[jax.ShapeDtypeStruct((nb, window, n), F32), jax.ShapeDtypeStruct((nb, window, n), F32),
                   jax.ShapeDtypeStruct((nb, L, 2 * n), BF16), jax.ShapeDtypeStruct((nb, 4 * n, L), BF16)],
        compiler_params=_params("arbitrary", "arbitrary"),
        name="kv_proj",
    )(x, mods, g, w_kv, kg, cos, sin)


def _q_proj(x, mod, n1g, wq, qg, cos, sin, head_dim, extra_scale=1.0):
    d = x.shape[1]
    h = _modulated_norm(x, n1g, mod[:, 0:d], mod[:, d:2 * d])
    y = _dot(h.astype(BF16), wq)
    return _heads_norm_rope(y, qg, cos, sin, head_dim, extra_scale / math.sqrt(head_dim))


def _attn_prompt_kernel(layer, att_layer, head_dim, kv_heads, nq, x_ref, mod_ref, n1g_ref, wq_ref, qg_ref,
                        cos_ref, sin_ref, *rest):
    k_refs, v_refs = rest[0:nq + 1], rest[nq + 1:2 * nq + 2]
    sink_ref, o_ref, s_scr, p_scr = rest[2 * nq + 2:]
    w = x_ref.shape[1] // nq
    nkv = kv_heads * head_dim
    q_blocks = _q_proj(x_ref[0], _mod_rows(mod_ref), n1g_ref[layer:layer + 1, :], wq_ref[0],
                       qg_ref[att_layer:att_layer + 1, :], cos_ref[...], sin_ref[...], head_dim, LOG2E)
    group = (2 * len(q_blocks)) // kv_heads
    key = lax.broadcasted_iota(jnp.int32, (w, 2 * w), 0)
    qry = lax.broadcasted_iota(jnp.int32, (w, 2 * w), 1)
    own = key <= (qry & (w - 1))
    first = lax.broadcasted_iota(jnp.int32, (1, 2 * w), 1) < w
    low = lax.broadcasted_iota(jnp.int32, (w, LANES), 1) < head_dim
    no_prev = jnp.where(pl.program_id(1) == 0, -jnp.inf, 0.0)
    tiles = [(u, kh, half) for u in range(nq) for kh in range(kv_heads) for half in range(2)]
    for n, (u, kh, half) in enumerate(tiles):
        ha, hb = kh * group + half, kh * group + 2 + half
        mask = (lambda t: jnp.where(low, t, 0.0)) if half == 0 else (lambda t: jnp.where(low, 0.0, t))
        rows = slice(u * w, (u + 1) * w)
        qcat = jnp.concatenate([mask(q_blocks[ha // 2][rows]), mask(q_blocks[hb // 2][rows])],
                               axis=0).astype(BF16)
        koff = (kh // 2) * LANES + (0 if kh % 2 == half else nkv)
        kcat = jnp.concatenate([k_refs[u][0, :, koff:koff + LANES], k_refs[u + 1][0, :, koff:koff + LANES]],
                               axis=0)
        s_t = _dot_nt(kcat, qcat)
        s_prev = s_t[:w] + no_prev if u == 0 else s_t[:w]
        s_scr[n] = jnp.where(own, s_t[w:], s_prev)
    for n, (u, kh, half) in enumerate(tiles):
        sc = s_scr[n]
        sink = jnp.where(first, sink_ref[att_layer, kh * group + half],
                         sink_ref[att_layer, kh * group + 2 + half]) * LOG2E
        m = jnp.maximum(jnp.max(sc, axis=0, keepdims=True), sink)
        p = jnp.exp2(sc - m)
        pn = p * (1.0 / (jnp.sum(p, axis=0, keepdims=True) + jnp.exp2(sink - m)))
        p_scr[n, 0:w] = jnp.where(own, pn, 0.0).astype(BF16)
        p_scr[n, w:2 * w] = jnp.where(own, 0.0, pn).astype(BF16)
    for u in range(nq):
        for kh in range(kv_heads):
            out_t = None
            for half in range(2):
                voff = (2 * kh + half) * LANES
                v_t = jnp.concatenate([v_refs[u + 1][0, voff:voff + LANES, :],
                                       v_refs[u][0, voff:voff + LANES, :]], axis=1)
                part = _dot(v_t, p_scr[(u * kv_heads + kh) * 2 + half])
                out_t = part if out_t is None else out_t + part
            for i in range(2):
                o_ref[0, u * w:(u + 1) * w, (2 * kh + i) * LANES:(2 * kh + i + 1) * LANES] = (
                    out_t[:, i * w:(i + 1) * w].T)


def _attn_prompt(layer, att_layer, x, mods, n1g, wq, qg, cos, sin, kx, vx, sinks, head_dim, kv_heads, w, nq):
    b, L, d = x.shape
    rows = lambda i, j: (i, j, 0)
    k_specs = [pl.BlockSpec((1, w, kx.shape[2]), lambda i, j, o=o: (i, jnp.maximum(nq * j + o, 0), 0))
               for o in range(-1, nq)]
    v_specs = [pl.BlockSpec((1, vx.shape[1], w), lambda i, j, o=o: (i, 0, jnp.maximum(nq * j + o, 0)))
               for o in range(-1, nq)]
    n_tiles = nq * 2 * kv_heads
    return pl.pallas_call(
        functools.partial(_attn_prompt_kernel, layer, att_layer, head_dim, kv_heads, nq),
        grid=(b, L // (nq * w)),
        in_specs=[pl.BlockSpec((1, nq * w, d), rows), _mod_spec(mods, layer),
                  _full(n1g.shape), _layer_spec(wq, att_layer), _full(qg.shape),
                  pl.BlockSpec((nq * w, LANES), lambda i, j: (j, 0)),
                  pl.BlockSpec((nq * w, LANES), lambda i, j: (j, 0))]
                 + k_specs + v_specs + [pl.BlockSpec(memory_space=pltpu.SMEM)],
        out_specs=pl.BlockSpec((1, nq * w, d), rows),
        out_shape=jax.ShapeDtypeStruct((b, L, d), F32),
        scratch_shapes=[pltpu.VMEM((n_tiles, w, 2 * w), F32), pltpu.VMEM((n_tiles, 2 * w, 2 * w), BF16)],
        compiler_params=_params("arbitrary", "arbitrary"),
        name="attn_prompt",
    )(x, mods, n1g, wq, qg, cos, sin, *([kx] * (nq + 1)), *([vx] * (nq + 1)), sinks)


def _proj_kernel(layer, x_ref, mod_ref, g_ref, w_ref, o_ref):
    d = x_ref.shape[1]
    mod = _mod_rows(mod_ref)
    h = _modulated_norm(x_ref[...], g_ref[layer:layer + 1, :], mod[:, 0:d], mod[:, d:2 * d])
    o_ref[...] = _dot(h.astype(BF16), w_ref[0])


def _proj(layer, x, mods, g, w, tn):
    r, d = x.shape
    n = w.shape[2]
    return pl.pallas_call(
        functools.partial(_proj_kernel, layer),
        grid=(n // tn,),
        in_specs=[_full((r, d)), _mod_spec(mods, layer), _full(g.shape),
                  pl.BlockSpec((1, d, tn), lambda j: (layer, 0, j))],
        out_specs=pl.BlockSpec((r, tn), lambda j: (0, j)),
        out_shape=jax.ShapeDtypeStruct((r, n), F32),
        compiler_params=_params("arbitrary"),
        name="proj_sample",
    )(x, mods, g, w)


def _q_sample_kernel(layer, att_layer, head_dim, x_ref, mod_ref, n1g_ref, wq_ref, qg_ref, cos_ref, sin_ref,
                     o_ref):
    blocks = _q_proj(x_ref[...], _mod_rows(mod_ref), n1g_ref[layer:layer + 1, :], wq_ref[0],
                     qg_ref[att_layer:att_layer + 1, :], cos_ref[...], sin_ref[...], head_dim)
    for j, blk in enumerate(blocks):
        o_ref[:, j * LANES:(j + 1) * LANES] = blk


def _q_sample(layer, att_layer, x, mods, n1g, wq, qg, cos, sin, head_dim):
    r, d = x.shape
    return pl.pallas_call(
        functools.partial(_q_sample_kernel, layer, att_layer, head_dim),
        grid=(1,),
        in_specs=[_full((r, d)), _mod_spec(mods, layer), _full(n1g.shape), _layer_spec(wq, att_layer),
                  _full(qg.shape), _full((1, LANES)), _full((1, LANES))],
        out_specs=_full((r, wq.shape[2])),
        out_shape=jax.ShapeDtypeStruct((r, wq.shape[2]), F32),
        compiler_params=_params("arbitrary"),
        name="q_sample",
    )(x, mods, n1g, wq, qg, cos, sin)


def _hgrn_sample_kernel(layer, heads, tb, steps, aliased, proj_ref, lbraw_ref, gng_ref, s_ref, *rest):
    o_ref, so_ref, stack_scr, acc_scr = rest[1:] if aliased else rest
    i = pl.program_id(0)
    rows = proj_ref.shape[0]
    d = proj_ref.shape[1] // 4
    dk = d // heads

    def update():
        @pl.when(i == 0)
        def _():
            lb = _lower_bound(lbraw_ref[...], layer)
            q, fg = _hgrn_gates(proj_ref[:, 0:2 * d], lb, d)
            for hh in range(heads):
                sl = slice(hh * dk, (hh + 1) * dk)
                ft = fg[:, sl].T
                hi = ft.astype(BF16)
                stack_scr[hh, 0:dk] = hi
                stack_scr[hh, dk:2 * dk] = (ft - hi.astype(F32)).astype(BF16)
                stack_scr[hh, 2 * dk:3 * dk] = (1.0 - ft).astype(BF16)
                stack_scr[hh, 3 * dk:4 * dk] = q[:, sl].T.astype(BF16)

        token_row = lax.broadcasted_iota(jnp.int32, (rows, dk), 0)
        sub = lax.broadcasted_iota(jnp.int32, (tb, dk), 0)
        base = pl.multiple_of(i * tb, tb)
        v_rows = proj_ref[pl.ds(base, tb), 2 * d:3 * d]
        o_rows = [jnp.zeros((tb, dk), F32) for _ in range(heads)]
        for t0 in range(0, tb, 2):
            onehot = jnp.concatenate([jnp.where(token_row == base + t, 1.0, 0.0) for t in (t0, t0 + 1)],
                                     axis=1).astype(BF16)
            for hh in range(heads):
                bc2 = _dot(stack_scr[hh], onehot)
                for t in (t0, t0 + 1):
                    bc = bc2[:, (t - t0) * dk:(t - t0 + 1) * dk]
                    f_b = bc[0:dk] + bc[dk:2 * dk]
                    k_b = bc[2 * dk:3 * dk]
                    q_b = bc[3 * dk:4 * dk]
                    s_new = f_b * s_ref[0, t, hh] + k_b * v_rows[t:t + 1, hh * dk:(hh + 1) * dk]
                    so_ref[0, t, hh] = s_new
                    o_rows[hh] = jnp.where(sub == t, jnp.sum(q_b * s_new, axis=0, keepdims=True), o_rows[hh])
        for hh in range(heads):
            acc_scr[pl.ds(base, tb), hh * dk:(hh + 1) * dk] = o_rows[hh]

        @pl.when(i == steps - 1)
        def _():
            gng = gng_ref[layer:layer + 1, :]
            for hh in range(heads):
                sl = slice(hh * dk, (hh + 1) * dk)
                o_ref[:, sl] = _gated_group_norm(acc_scr[:, sl], gng,
                                                 proj_ref[:, 3 * d + hh * dk:3 * d + (hh + 1) * dk])

    if aliased:
        update()
    else:
        pl.when(i < steps)(update)

        @pl.when(i >= steps)
        def _():
            so_ref[...] = jnp.zeros_like(so_ref)


def _hgrn_sample(layer, proj, lb_raw, gn_g, state, new_state, tb):
    n_layers, nb, heads, dk, dv = state.shape
    d = proj.shape[1] // 4
    steps = nb // tb
    aliased = new_state is not None
    assert aliased == (layer > 0) and tb == SUBLANES
    if aliased:
        grid = (steps,)
        s_in = s_out = pl.BlockSpec((1, tb, heads, dk, dv), lambda i: (layer, i, 0, 0, 0))
    else:
        grid = (n_layers * steps,)
        s_in = pl.BlockSpec((1, tb, heads, dk, dv), lambda i: (0, jnp.minimum(i, steps - 1), 0, 0, 0))
        s_out = pl.BlockSpec((1, tb, heads, dk, dv), lambda i: (i // steps, i % steps, 0, 0, 0))
    in_specs = [_full(proj.shape), _full(lb_raw.shape), _full(gn_g.shape), s_in]
    args = [proj, lb_raw, gn_g, state]
    if aliased:
        in_specs.append(pl.BlockSpec(memory_space=pl.ANY))
        args.append(new_state)
    return pl.pallas_call(
        functools.partial(_hgrn_sample_kernel, layer, heads, tb, steps, aliased),
        grid=grid,
        in_specs=in_specs,
        out_specs=[_full((nb, d)), s_out],
        out_shape=[jax.ShapeDtypeStruct((nb, d), F32), jax.ShapeDtypeStruct(state.shape, F32)],
        scratch_shapes=[pltpu.VMEM((heads, 4 * dk, nb), BF16), pltpu.VMEM((nb, d), F32)],
        input_output_aliases={4: 1} if aliased else {},
        compiler_params=_params("arbitrary"),
        name=f"hgrn_sample_{layer}",
    )(*args)


def _cache_roll_kernel(ck_ref, cv_ref, kn_ref, vn_ref, ko_ref, vo_ref, kb_ref, vb_ref):
    w = ck_ref.shape[1]
    ko_ref[:, 0:w - 1, :] = ck_ref[:, 1:w, :]
    ko_ref[:, w - 1:w, :] = kn_ref[...]
    vo_ref[:, 0:w - 1, :] = cv_ref[:, 1:w, :]
    vo_ref[:, w - 1:w, :] = vn_ref[...]
    kb_ref[...] = ko_ref[...].astype(BF16)
    vb_ref[...] = vo_ref[...].astype(BF16)


def _cache_roll(ck, cv, kn, vn, tb):
    nb, w, n = ck.shape
    blk = pl.BlockSpec((tb, w, n), lambda i: (i, 0, 0))
    new = pl.BlockSpec((tb, 1, n), lambda i: (i, 0, 0))
    return pl.pallas_call(
        _cache_roll_kernel,
        grid=(nb // tb,),
        in_specs=[blk, blk, new, new],
        out_specs=[blk, blk, blk, blk],
        out_shape=[jax.ShapeDtypeStruct(ck.shape, F32), jax.ShapeDtypeStruct(cv.shape, F32),
                   jax.ShapeDtypeStruct(ck.shape, BF16), jax.ShapeDtypeStruct(cv.shape, BF16)],
        compiler_params=_params("arbitrary"),
        name="cache_roll",
    )(ck, cv, kn.reshape(nb, 1, n), vn.reshape(nb, 1, n))


def _attn_sample_kernel(att_layer, tb, q_ref, k_ref, v_ref, sink_ref, o_ref):
    heads, head_dim = q_ref.shape[1], q_ref.shape[2]
    nkv = k_ref.shape[2]
    group = heads // (nkv // head_dim)
    r = lax.broadcasted_iota(jnp.int32, (heads, nkv), 0) // group
    c = lax.broadcasted_iota(jnp.int32, (heads, nkv), 1) // head_dim
    own = r == c
    sink = sink_ref[:, att_layer:att_layer + 1][None]
    q = q_ref[...]
    qe = jnp.where(own[None], jnp.concatenate([q] * (nkv // head_dim), axis=-1), 0.0).astype(BF16)
    s = jnp.einsum("thc,tjc->thj", qe, k_ref[...], preferred_element_type=F32)
    m = jnp.maximum(jnp.max(s, axis=-1, keepdims=True), sink)
    p = jnp.exp(s - m)
    denom = jnp.sum(p, axis=-1, keepdims=True) + jnp.exp(sink - m)
    pv = jnp.where(own[None], jnp.einsum("thj,tjc->thc", p.astype(BF16), v_ref[...],
                                         preferred_element_type=F32), 0.0)
    o = pv[:, :, 0:head_dim]
    for j in range(1, nkv // head_dim):
        o = o + pv[:, :, j * head_dim:(j + 1) * head_dim]
    o_ref[...] = o / denom


def _attn_sample(att_layer, q3, k, v, sinks_t, tb):
    nb, heads, head_dim = q3.shape
    w, nkv = k.shape[1], k.shape[2]
    return pl.pallas_call(
        functools.partial(_attn_sample_kernel, att_layer, tb),
        grid=(nb // tb,),
        in_specs=[pl.BlockSpec((tb, heads, head_dim), lambda i: (i, 0, 0)),
                  pl.BlockSpec((tb, w, nkv), lambda i: (i, 0, 0)),
                  pl.BlockSpec((tb, w, nkv), lambda i: (i, 0, 0)),
                  _full(sinks_t.shape)],
        out_specs=pl.BlockSpec((tb, heads, head_dim), lambda i: (i, 0, 0)),
        out_shape=jax.ShapeDtypeStruct(q3.shape, F32),
        compiler_params=_params("arbitrary"),
        name="attn_sample",
    )(q3, k, v, sinks_t)


def _rope_tables(pos, head_dim):
    half = head_dim // 2
    inv = ROPE_THETA ** (-jnp.arange(half, dtype=F32) / half)
    ang = pos.astype(F32)[:, None] * inv[None, :]
    cos, sin = jnp.cos(ang), jnp.sin(ang)
    reps = LANES // head_dim
    return (jnp.tile(jnp.concatenate([cos, cos], axis=1), (1, reps)),
            jnp.tile(jnp.concatenate([-sin, sin], axis=1), (1, reps)))


def kernel(x_prompt, x_sample, c_prompt, c_sample, state_hgrn, cache_k, cache_v, w_ada, b_ada, norm1_g, norm2_g, hg_w_in, hg_w_out, hg_lower_bounds, hg_gn_g, kv_w_ada, kv_b_ada, kv_norm_g, w_kv, k_norm_g, w_q, q_norm_g, sinks, w_o, w_up, w_down):
    bp, Lp, d = x_prompt.shape
    bs = x_sample.shape[0]
    depth = w_ada.shape[0]
    n_a = hg_w_in.shape[0]
    heads = state_hgrn.shape[2]
    window, kv_heads, head_dim = cache_k.shape[1], cache_k.shape[2], cache_k.shape[3]
    assert LANES == 2 * head_dim and kv_heads % 2 == 0 and Lp % window == 0
    assert window & (window - 1) == 0 and (d // head_dim) // kv_heads == 4
    nkv = kv_heads * head_dim
    reps = LANES // head_dim
    bf = lambda t: t.astype(BF16)

    pad = (-(bs + bp)) % 16
    c_all = jnp.concatenate([c_sample, c_prompt, jnp.zeros((pad, d), F32)], axis=0)
    mods_s, mods_p = _ada(c_all, bs, bp, w_ada, b_ada, T.ada_cols)
    kv_mods_s, kv_mods_p = _ada(c_all, bs, bp, kv_w_ada[None], kv_b_ada[None], T.kv_ada_cols)

    cos_p, sin_p = _rope_tables(jnp.arange(Lp), head_dim)
    cos_s, sin_s = _rope_tables(jnp.full((1,), PAST_LEN), head_dim)
    kg = jnp.tile(k_norm_g, reps)[None]
    qg = jnp.tile(q_norm_g, (1, reps))
    sinks_t = sinks.T
    w_kv_b, w_in_b, w_out_b, w_q_b, w_o_b = bf(w_kv), bf(hg_w_in), bf(hg_w_out), bf(w_q), bf(w_o)
    w_up_b, w_dn_b = bf(w_up), bf(w_down)

    xp = x_prompt
    xs = x_sample.reshape(bs, d)
    hg_p, hg_s = [], None
    k_p = v_p = kx_p = vx_p = k_s = v_s = None
    for l in range(depth):
        if l == n_a:
            k_p, v_p, kx_p, vx_p = _kv(xp, kv_mods_p, kv_norm_g[None], w_kv_b, kg, cos_p, sin_p,
                                       head_dim, window, T.kv_rows)
            k_n, v_n, _, _ = _kv(xs[None], kv_mods_s, kv_norm_g[None], w_kv_b, kg, cos_s, sin_s,
                                 head_dim, bs, bs)
            k_s, v_s, kb_s, vb_s = _cache_roll(cache_k.reshape(bs, window, nkv),
                                               cache_v.reshape(bs, window, nkv), k_n[0], v_n[0], T.cache_tokens)
        if l < n_a:
            w_mix_b, mix_layer = w_out_b, l
            a_p, s_p = _hgrn_prompt(l, xp, mods_p, norm1_g, w_in_b, hg_lower_bounds, hg_gn_g, heads,
                                    T.hgrn_chunk, T.hgrn_rows)
            hg_p.append(s_p)
            proj_s = _proj(l, xs, mods_s, norm1_g, w_in_b, T.proj_cols)
            a_s, hg_s = _hgrn_sample(l, proj_s, hg_lower_bounds, hg_gn_g, state_hgrn, hg_s, SUBLANES)
        else:
            j = l - n_a
            w_mix_b, mix_layer = w_o_b, j
            a_p = _attn_prompt(l, j, xp, mods_p, norm1_g, w_q_b, qg, cos_p, sin_p, kx_p, vx_p, sinks,
                               head_dim, kv_heads, window, T.attn_blocks)
            q_s = _q_sample(l, j, xs, mods_s, norm1_g, w_q_b, qg, cos_s, sin_s, head_dim)
            a_s = _attn_sample(j, q_s.reshape(bs, d // head_dim, head_dim), kb_s, vb_s, sinks_t,
                               T.attn_sample_tokens)
            a_s = a_s.reshape(bs, d)
        xp = _post(l, xp, a_p, mods_p, norm2_g, w_mix_b, mix_layer, w_up_b, w_dn_b, T.mlp_rows)
        xs = _post_sample(l, xs, a_s, mods_s, norm2_g, w_mix_b, mix_layer, w_up_b, w_dn_b, T.mlp_sample_slab)

    shape4 = lambda t: t.reshape(t.shape[0], window, kv_heads, head_dim)
    return (xp, xs.reshape(bs, 1, d), jnp.stack(hg_p), shape4(k_p), shape4(v_p), hg_s,
            shape4(k_s), shape4(v_s))
```

```python
import functools
import math
from typing import NamedTuple

import numpy as np
import jax
import jax.numpy as jnp
from jax import lax
from jax.experimental import pallas as pl
from jax.experimental.pallas import tpu as pltpu

F32 = jnp.float32
BF16 = jnp.bfloat16

PAST_LEN = 8192
ROPE_THETA = 10000.0
EPS = 1e-6
LOG2E = 1.4426950408889634
LANES = 128
SUBLANES = 8
FIRST_TABLE_LEVEL = 2
VMEM_LIMIT = 56 * 1024 * 1024


class _Tiles(NamedTuple):
    ada_cols: int = 3072
    kv_ada_cols: int = 1024
    hgrn_chunk: int = 128
    hgrn_rows: int = 512
    mlp_rows: int = 512
    kv_rows: int = 1024
    attn_blocks: int = 8
    proj_cols: int = 2048
    cache_tokens: int = 16
    attn_sample_tokens: int = 16
    mlp_sample_slab: int = 1024


T = _Tiles()

NT_DIMS = (((1,), (1,)), ((), ()))
TN_DIMS = (((0,), (0,)), ((), ()))


def _dot(a, b):
    return jnp.dot(a, b, preferred_element_type=F32)


def _dot_nt(a, b):
    return lax.dot_general(a, b, NT_DIMS, preferred_element_type=F32)


def _sigmoid(x):
    return 1.0 / (1.0 + jnp.exp(-x))


def _silu(x):
    return x * (0.5 + 0.5 * jnp.tanh(0.5 * x))


def _rms(x, g):
    ms = jnp.mean(x * x, axis=-1, keepdims=True)
    return x * lax.rsqrt(ms + EPS) * g


def _modulated_norm(x, gain, shift, scale):
    return _rms(x, gain) * (1.0 + scale) + shift


def _params(*sem):
    return pltpu.CompilerParams(dimension_semantics=sem, vmem_limit_bytes=VMEM_LIMIT)


def _full(shape):
    n = len(shape)
    return pl.BlockSpec(shape, lambda *_: (0,) * n)


def _ada_kernel(c_ref, w_ref, b_ref, os_ref, op_ref):
    bs, bp = os_ref.shape[1], op_ref.shape[1]
    c = c_ref[...]
    a = _silu(c).astype(BF16)
    res = _dot(a, w_ref[0].astype(BF16)) + b_ref[0]
    os_ref[0] = res[0:bs]
    for r in range(bp):
        op_ref[0, r] = jnp.broadcast_to(res[bs + r:bs + r + 1], op_ref.shape[2:])


def _ada(c_all, bs, bp, w, b, tn):
    nl, d, n = w.shape
    r = c_all.shape[0]
    return pl.pallas_call(
        _ada_kernel,
        grid=(nl, n // tn),
        in_specs=[pl.BlockSpec((r, d), lambda l, j: (0, 0)),
                  pl.BlockSpec((1, d, tn), lambda l, j: (l, 0, j)),
                  pl.BlockSpec((1, 1, tn), lambda l, j: (l, 0, j))],
        out_specs=[pl.BlockSpec((1, bs, tn), lambda l, j: (l, 0, j)),
                   pl.BlockSpec((1, bp, SUBLANES, tn), lambda l, j: (l, 0, 0, j))],
        out_shape=[jax.ShapeDtypeStruct((nl, bs, n), F32),
                   jax.ShapeDtypeStruct((nl, bp, SUBLANES, n), F32)],
        compiler_params=_params("arbitrary", "arbitrary"),
        name="ada",
    )(c_all, w, b.reshape(nl, 1, n))


def _mod_rows(mod_ref):
    return mod_ref[0, 0, 0:1, :] if len(mod_ref.shape) == 4 else mod_ref[0]


def _mod_spec(mods, layer):
    if mods.ndim == 4:
        return pl.BlockSpec((1, 1) + mods.shape[2:], lambda i, *_: (layer, i, 0, 0))
    return pl.BlockSpec((1,) + mods.shape[1:], lambda *_: (layer, 0, 0))


def _layer_spec(stack, layer):
    zeros = (0,) * (stack.ndim - 1)
    return pl.BlockSpec((1,) + stack.shape[1:], lambda *_: (layer,) + zeros, pipeline_mode=pl.Buffered(1))


def _group_mean_matrix(group, width):
    r = lax.broadcasted_iota(jnp.int32, (width, width), 0) // group
    c = lax.broadcasted_iota(jnp.int32, (width, width), 1) // group
    return jnp.where(r == c, 1.0 / group, 0.0).astype(BF16)


def _heads_norm_rope(y, gain, cos, sin, head_dim, out_scale):
    rows, n = y.shape
    width = 2 * LANES if n % (2 * LANES) == 0 else LANES
    mean = _group_mean_matrix(head_dim, width)
    gain = gain * out_scale
    half = head_dim // 2
    first = (lax.broadcasted_iota(jnp.int32, (rows, LANES), 1) % head_dim) < half
    out = []
    for j in range(0, n, width):
        ms = _dot(jnp.square(y[:, j:j + width]).astype(BF16), mean)
        for i in range(0, width, LANES):
            yn = y[:, j + i:j + i + LANES] * lax.rsqrt(ms[:, i:i + LANES] + EPS) * gain
            rot = jnp.where(first, pltpu.roll(yn, LANES - half, 1), pltpu.roll(yn, half, 1))
            out.append(yn * cos + rot * sin)
    return out


def _lower_bound(raw, layer):
    m = jnp.max(raw, axis=0, keepdims=True)
    e = jnp.exp(raw - m)
    sm = e / jnp.sum(e, axis=0, keepdims=True)
    acc = sm[0:1]
    for j in range(1, layer + 1):
        acc = acc + sm[j:j + 1]
    return acc - sm[0:1]


def _hgrn_gates(proj, lb, d):
    qa = proj[:, 0:d]
    q = _silu(qa)
    fg = lb + (1.0 - lb) * _sigmoid(proj[:, d:2 * d])
    return q, fg


def _gated_group_norm(o, gain, gate_pre):
    ms = jnp.mean(o * o, axis=-1, keepdims=True)
    return o * lax.rsqrt(ms + EPS) * gain * _silu(gate_pre)


def _hgrn_prompt_kernel(layer, heads, x_ref, mod_ref, n1g_ref, win_ref, lbraw_ref, gng_ref,
                        lev_ref, sums_ref, o_ref, s_ref, st_scr, d_scr, xs_scr, att_scr):
    c = pl.program_id(1)
    total, d = x_ref.shape[1], x_ref.shape[2]
    rows = lev_ref.shape[0]
    dk = d // heads
    n_levels = int(math.log2(rows))

    @pl.when(c == 0)
    def _():
        st_scr[...] = jnp.zeros_like(st_scr)

    mod = _mod_rows(mod_ref)
    h = _modulated_norm(x_ref[0], n1g_ref[layer:layer + 1, :], mod[:, 0:d], mod[:, d:2 * d])
    proj_all = _dot(h.astype(BF16), win_ref[0])
    lb = _lower_bound(lbraw_ref[...], layer)
    q_all, fg_all = _hgrn_gates(proj_all, lb, d)
    lf2_all = jnp.log2(fg_all)
    n_table = d_scr.shape[0] // rows - 1 + FIRST_TABLE_LEVEL - 1
    cum_row = d_scr.shape[0] - rows
    lev = lev_ref[...]
    row_id = lax.broadcasted_iota(jnp.int32, (rows, dk), 0)
    upper = [((row_id >> p) & 1) == 1 for p in range(n_table + 1)]
    tile_row = lax.broadcasted_iota(jnp.int32, (rows // SUBLANES, SUBLANES, dk), 1) & 3
    col = lax.broadcasted_iota(jnp.int32, (1, rows), 1)
    gng = gng_ref[layer:layer + 1, :]

    for r0 in range(0, total, rows):
        proj, q, fg = (t[r0:r0 + rows] for t in (proj_all, q_all, fg_all))
        k = 1.0 - fg
        lf2 = lf2_all[r0:r0 + rows]
        hi = lf2.astype(BF16)
        lo = (lf2 - hi.astype(F32)).astype(BF16)
        d_scr[...] = _dot(sums_ref[...], jnp.concatenate([hi, lo], axis=0))

        for hh in range(heads):
            sl = slice(hh * dk, (hh + 1) * dk)
            q_h, k_h = q[:, sl], k[:, sl]
            cum = d_scr[cum_row:cum_row + rows, sl]
            f_h = fg[:, sl]
            xs_scr[hh, 0] = jnp.where(upper[0], q_h * f_h, k_h).astype(BF16)
            f_t = f_h.reshape(rows // SUBLANES, SUBLANES, dk)
            e1 = jnp.where(tile_row == 3, f_t * pltpu.roll(f_t, 1, 1),
                           jnp.where(tile_row == 2, f_t,
                                     jnp.where(tile_row == 1, 1.0, pltpu.roll(f_t, SUBLANES - 1, 1))))
            xs_scr[hh, 1] = (jnp.where(upper[1], q_h, k_h) * e1.reshape(rows, dk)).astype(BF16)
            cum_last = cum[rows - 1:rows, :]
            edge = 2 << n_table
            q_dec = [q_h[0:edge] * jnp.exp2(cum[0:edge])]
            k_dec = [k_h[rows - edge:rows] * jnp.exp2(cum_last - cum[rows - edge:rows])]
            for p in range(FIRST_TABLE_LEVEL, n_levels):
                m = 1 << p
                if p <= n_table:
                    e = jnp.exp2(d_scr[(p - FIRST_TABLE_LEVEL) * rows:(p - FIRST_TABLE_LEVEL + 1) * rows, sl])
                    xs_scr[hh, p] = (jnp.where(upper[p], q_h, k_h) * e).astype(BF16)
                    continue
                pieces = []
                for r in range(0, rows, m):
                    ref = (r // (2 * m)) * 2 * m + m - 1
                    if (r // m) % 2:
                        pieces.append(q_h[r:r + m] * jnp.exp2(cum[r:r + m] - cum[ref:ref + 1]))
                    else:
                        pieces.append(k_h[r:r + m] * jnp.exp2(cum[ref:ref + 1] - cum[r:r + m]))
                xs_scr[hh, p] = jnp.concatenate(pieces, axis=0).astype(BF16)
                q_dec.append(pieces[1] * jnp.exp2(cum[m - 1:m]))
                k_dec.insert(0, pieces[rows // m - 2] * jnp.exp2(cum_last - cum[rows - m - 1:rows - m]))
            xs_scr[hh, n_levels] = jnp.concatenate(q_dec, axis=0).astype(BF16)
            xs_scr[hh, n_levels + 1] = jnp.concatenate(k_dec, axis=0).astype(BF16)

        for hh in range(heads):
            xs = xs_scr[hh, 0]
            att = jnp.where(lev == 0, _dot_nt(xs, xs), 0.0)
            for p in range(1, n_levels):
                m = 1 << p
                xs = xs_scr[hh, p]
                pp = _dot_nt(xs, xs)
                if p <= n_table:
                    att = jnp.where(lev == p, pp, att)
                else:
                    att = jnp.concatenate(
                        [jnp.where((col >= r - m) & (col < r), pp[r:r + m], att[r:r + m]) if (r // m) % 2
                         else att[r:r + m] for r in range(0, rows, m)], axis=0)
            att_scr[hh] = att.astype(BF16)

        for hh in range(heads):
            sl = slice(hh * dk, (hh + 1) * dk)
            v_f = proj[:, 2 * d + hh * dk:2 * d + (hh + 1) * dk]
            v_h = v_f.astype(BF16)
            cum_last = d_scr[cum_row + rows - 1:cum_row + rows, sl]
            st = st_scr[hh]
            o_h = (_dot(jnp.concatenate([att_scr[hh], xs_scr[hh, n_levels]], axis=1),
                        jnp.concatenate([v_h, st.T.astype(BF16)], axis=0))
                   + jnp.sum(q[:, sl] * k[:, sl], axis=-1, keepdims=True) * v_f)
            st_scr[hh] = st * jnp.exp2(cum_last) + lax.dot_general(
                v_h, xs_scr[hh, n_levels + 1], TN_DIMS, preferred_element_type=F32)
            g_pre = proj[:, 3 * d + hh * dk:3 * d + (hh + 1) * dk]
            o_ref[0, r0:r0 + rows, sl] = _gated_group_norm(o_h, gng, g_pre)

    @pl.when(c == pl.num_programs(1) - 1)
    def _():
        for hh in range(heads):
            s_ref[0, hh] = st_scr[hh].T


def _level_table(rows):
    t = np.arange(rows)[:, None]
    s = np.arange(rows)[None, :]
    x = t ^ s
    lev = np.where(t > s, np.floor(np.log2(np.maximum(x, 1))).astype(np.int32), np.where(t == s, -1, -2))
    return jnp.asarray(lev, dtype=jnp.int32)


def _sum_table(rows):
    t = np.arange(rows)[:, None]
    j = np.arange(rows)[None, :]
    blocks = []
    for p in range(FIRST_TABLE_LEVEL, int(math.log2(SUBLANES))):
        m = 1 << p
        ref = (t // (2 * m)) * (2 * m) + m - 1
        up = ((t >> p) & 1) == 1
        blocks.append(np.where(up, (j > ref) & (j <= t), (j > t) & (j <= ref)))
    blocks.append(j <= t)
    table = np.concatenate(blocks, axis=0).astype(np.float32)
    return jnp.asarray(np.concatenate([table, table], axis=1), dtype=BF16)


def _hgrn_prompt(layer, x, mods, n1g, w_in, lb_raw, gn_g, heads, chunk, step_rows):
    b, L, d = x.shape
    dk = d // heads
    sums = _sum_table(chunk)
    return pl.pallas_call(
        functools.partial(_hgrn_prompt_kernel, layer, heads),
        grid=(b, L // step_rows),
        in_specs=[pl.BlockSpec((1, step_rows, d), lambda i, c: (i, c, 0)),
                  _mod_spec(mods, layer),
                  _full(n1g.shape), _layer_spec(w_in, layer), _full(lb_raw.shape), _full(gn_g.shape),
                  _full((chunk, chunk)), _full(sums.shape)],
        out_specs=[pl.BlockSpec((1, step_rows, d), lambda i, c: (i, c, 0)),
                   pl.BlockSpec((1, heads, dk, dk), lambda i, c: (i, 0, 0, 0))],
        out_shape=[jax.ShapeDtypeStruct((b, L, d), F32),
                   jax.ShapeDtypeStruct((b, heads, dk, dk), F32)],
        scratch_shapes=[pltpu.VMEM((heads, dk, dk), F32), pltpu.VMEM((sums.shape[0], d), F32),
                        pltpu.VMEM((heads, int(math.log2(chunk)) + 2, chunk, dk), BF16),
                        pltpu.VMEM((heads, chunk, chunk), BF16)],
        compiler_params=_params("arbitrary", "arbitrary"),
        name=f"hgrn_prompt_{layer}",
    )(x, mods, n1g, w_in, lb_raw, gn_g, _level_table(chunk), sums)


def _post_kernel(layer, x_ref, a_ref, mod_ref, n2g_ref, wo_ref, wup_ref, wdn_ref, o_ref):
    d = x_ref.shape[2]
    mod = _mod_rows(mod_ref)
    g1, sh2, sc2, g2 = (mod[:, j * d:(j + 1) * d] for j in range(2, 6))
    x1 = x_ref[0] + g1 * _dot(a_ref[0].astype(BF16), wo_ref[0])
    h2 = _modulated_norm(x1, n2g_ref[layer:layer + 1, :], sh2, sc2).astype(BF16)
    y = None
    for c in range(0, wup_ref.shape[2], d):
        u = jnp.square(jnp.maximum(_dot(h2, wup_ref[0, :, c:c + d]), 0.0)).astype(BF16)
        part = _dot(u, wdn_ref[0, c:c + d, :])
        y = part if y is None else y + part
    o_ref[0] = x1 + g2 * y


def _post_sample_kernel(layer, x_ref, a_ref, mod_ref, n2g_ref, wo_ref, wup_ref, wdn_ref, o_ref,
                        x1_scr, h2_scr, y_scr):
    c = pl.program_id(0)
    d = x_ref.shape[1]
    mod = _mod_rows(mod_ref)

    @pl.when(c == 0)
    def _():
        g1, sh2, sc2 = (mod[:, j * d:(j + 1) * d] for j in range(2, 5))
        x1 = x_ref[...] + g1 * _dot(a_ref[...].astype(BF16), wo_ref[0])
        x1_scr[...] = x1
        h2_scr[...] = _modulated_norm(x1, n2g_ref[layer:layer + 1, :], sh2, sc2).astype(BF16)
        y_scr[...] = jnp.zeros_like(y_scr)

    u = jnp.square(jnp.maximum(_dot(h2_scr[...], wup_ref[0]), 0.0)).astype(BF16)
    y_scr[...] += _dot(u, wdn_ref[0])

    @pl.when(c == pl.num_programs(0) - 1)
    def _():
        o_ref[...] = x1_scr[...] + mod[:, 5 * d:6 * d] * y_scr[...]


def _post_sample(layer, x, a, mods, n2g, w_o, o_layer, w_up, w_dn, slab):
    r, d = x.shape
    ff = w_up.shape[2]
    return pl.pallas_call(
        functools.partial(_post_sample_kernel, layer),
        grid=(ff // slab,),
        in_specs=[_full((r, d)), _full((r, d)), _mod_spec(mods, layer), _full(n2g.shape),
                  _layer_spec(w_o, o_layer),
                  pl.BlockSpec((1, d, slab), lambda c: (layer, 0, c)),
                  pl.BlockSpec((1, slab, d), lambda c: (layer, c, 0))],
        out_specs=_full((r, d)),
        out_shape=jax.ShapeDtypeStruct((r, d), F32),
        scratch_shapes=[pltpu.VMEM((r, d), F32), pltpu.VMEM((r, d), BF16), pltpu.VMEM((r, d), F32)],
        compiler_params=_params("arbitrary"),
        name="post_mlp_sample",
    )(x, a, mods, n2g, w_o, w_up, w_dn)


def _post(layer, x, a, mods, n2g, w_o, o_layer, w_up, w_dn, tm):
    nb, L, d = x.shape
    return pl.pallas_call(
        functools.partial(_post_kernel, layer),
        grid=(nb, L // tm),
        in_specs=[pl.BlockSpec((1, tm, d), lambda i, j: (i, j, 0)),
                  pl.BlockSpec((1, tm, d), lambda i, j: (i, j, 0)),
                  _mod_spec(mods, layer), _full(n2g.shape),
                  _layer_spec(w_o, o_layer), _layer_spec(w_up, layer), _layer_spec(w_dn, layer)],
        out_specs=pl.BlockSpec((1, tm, d), lambda i, j: (i, j, 0)),
        out_shape=jax.ShapeDtypeStruct((nb, L, d), F32),
        compiler_params=_params("arbitrary", "arbitrary"),
        name="post_mlp",
    )(x, a, mods, n2g, w_o, w_up, w_dn)


def _kv_kernel(head_dim, window, x_ref, mod_ref, g_ref, w_ref, kg_ref, cos_ref, sin_ref,
               k_ref, v_ref, kx_ref, vx_ref):
    tm, d = x_ref.shape[1], x_ref.shape[2]
    mod = _mod_rows(mod_ref)
    hk = _modulated_norm(x_ref[0], g_ref[...], mod[:, 0:d], mod[:, d:2 * d])
    y = _dot(hk.astype(BF16), w_ref[...])
    n = y.shape[1] // 2
    cos, sin, kg = cos_ref[...], sin_ref[...], kg_ref[...]
    low = lax.broadcasted_iota(jnp.int32, (tm, LANES), 1) < head_dim
    k_blocks, v_blocks = _heads_norm_rope(y[:, 0:n], kg, cos, sin, head_dim, 1.0), []
    for j in range(n // LANES):
        kb = k_blocks[j]
        vb = y[:, n + j * LANES:n + (j + 1) * LANES]
        v_blocks.append(vb)
        kx_ref[0, :, j * LANES:(j + 1) * LANES] = kb.astype(BF16)
        kx_ref[0, :, n + j * LANES:n + (j + 1) * LANES] = pltpu.roll(kb, head_dim, 1).astype(BF16)
        vs = pltpu.roll(vb, head_dim, 1)
        variants = (jnp.where(low, vb, 0.0), jnp.where(low, 0.0, vs),
                    jnp.where(low, vs, 0.0), jnp.where(low, 0.0, vb))
        for i, var in enumerate(variants):
            off = (4 * j + i) * LANES
            vx_ref[0, off:off + LANES, :] = var.astype(BF16).T

    @pl.when(pl.program_id(1) == pl.num_programs(1) - 1)
    def _():
        for j in range(n // LANES):
            k_ref[0, :, j * LANES:(j + 1) * LANES] = k_blocks[j][tm - window:, :]
            v_ref[0, :, j * LANES:(j + 1) * LANES] = v_blocks[j][tm - window:, :]


def _kv(x, mods, g, w_kv, kg, cos, sin, head_dim, window, tm):
    nb, L, d = x.shape
    n = w_kv.shape[1] // 2
    rope_rows = tm if cos.shape[0] > 1 else 1
    rope_map = (lambda i, j: (j, 0)) if cos.shape[0] > 1 else (lambda i, j: (0, 0))
    last = lambda i, j: (i, 0, 0)
    tile = lambda i, j: (i, j, 0)
    return pl.pallas_call(
        functools.partial(_kv_kernel, head_dim, window),
        grid=(nb, L // tm),
        in_specs=[pl.BlockSpec((1, tm, d), tile), _mod_spec(mods, 0),
                  _full((1, d)), _full(w_kv.shape), _full((1, LANES)),
                  pl.BlockSpec((rope_rows, LANES), rope_map),
                  pl.BlockSpec((rope_rows, LANES), rope_map)],
        out_specs=[pl.BlockSpec((1, window, n), last), pl.BlockSpec((1, window, n), last),
                   pl.BlockSpec((1, tm, 2 * n), tile), pl.BlockSpec((1, 4 * n, tm), lambda i, j: (i, 0, j))],
        out_shape=[jax.ShapeDtypeStruct((nb, window, n), F32), jax.ShapeDtypeStruct((nb, window, n), F32),
                   jax.ShapeDtypeStruct((nb, L, 2 * n), BF16), jax.ShapeDtypeStruct((nb, 4 * n, L), BF16)],
        compiler_params=_params("arbitrary", "arbitrary"),
        name="kv_proj",
    )(x, mods, g, w_kv, kg, cos, sin)


def _q_proj(x, mod, n1g, wq, qg, cos, sin, head_dim, extra_scale=1.0):
    d = x.shape[1]
    h = _modulated_norm(x, n1g, mod[:, 0:d], mod[:, d:2 * d])
    y = _dot(h.astype(BF16), wq)
    return _heads_norm_rope(y, qg, cos, sin, head_dim, extra_scale / math.sqrt(head_dim))


def _attn_prompt_kernel(layer, att_layer, head_dim, kv_heads, nq, x_ref, mod_ref, n1g_ref, wq_ref, qg_ref,
                        cos_ref, sin_ref, *rest):
    k_refs, v_refs = rest[0:nq + 1], rest[nq + 1:2 * nq + 2]
    sink_ref, o_ref, s_scr, p_scr = rest[2 * nq + 2:]
    w = x_ref.shape[1] // nq
    nkv = kv_heads * head_dim
    q_blocks = _q_proj(x_ref[0], _mod_rows(mod_ref), n1g_ref[layer:layer + 1, :], wq_ref[0],
                       qg_ref[att_layer:att_layer + 1, :], cos_ref[...], sin_ref[...], head_dim, LOG2E)
    group = (2 * len(q_blocks)) // kv_heads
    key = lax.broadcasted_iota(jnp.int32, (w, 2 * w), 0)
    qry = lax.broadcasted_iota(jnp.int32, (w, 2 * w), 1)
    own = key <= (qry & (w - 1))
    first = lax.broadcasted_iota(jnp.int32, (1, 2 * w), 1) < w
    low = lax.broadcasted_iota(jnp.int32, (w, LANES), 1) < head_dim
    no_prev = jnp.where(pl.program_id(1) == 0, -jnp.inf, 0.0)
    tiles = [(u, kh, half) for u in range(nq) for kh in range(kv_heads) for half in range(2)]
    for n, (u, kh, half) in enumerate(tiles):
        ha, hb = kh * group + half, kh * group + 2 + half
        mask = (lambda t: jnp.where(low, t, 0.0)) if half == 0 else (lambda t: jnp.where(low, 0.0, t))
        rows = slice(u * w, (u + 1) * w)
        qcat = jnp.concatenate([mask(q_blocks[ha // 2][rows]), mask(q_blocks[hb // 2][rows])],
                               axis=0).astype(BF16)
        koff = (kh // 2) * LANES + (0 if kh % 2 == half else nkv)
        kcat = jnp.concatenate([k_refs[u][0, :, koff:koff + LANES], k_refs[u + 1][0, :, koff:koff + LANES]],
                               axis=0)
        s_t = _dot_nt(kcat, qcat)
        s_prev = s_t[:w] + no_prev if u == 0 else s_t[:w]
        s_scr[n] = jnp.where(own, s_t[w:], s_prev)
    for n, (u, kh, half) in enumerate(tiles):
        sc = s_scr[n]
        sink = jnp.where(first, sink_ref[att_layer, kh * group + half],
                         sink_ref[att_layer, kh * group + 2 + half]) * LOG2E
        m = jnp.maximum(jnp.max(sc, axis=0, keepdims=True), sink)
        p = jnp.exp2(sc - m)
        pn = p * (1.0 / (jnp.sum(p, axis=0, keepdims=True) + jnp.exp2(sink - m)))
        p_scr[n, 0:w] = jnp.where(own, pn, 0.0).astype(BF16)
        p_scr[n, w:2 * w] = jnp.where(own, 0.0, pn).astype(BF16)
    for u in range(nq):
        for kh in range(kv_heads):
            out_t = None
            for half in range(2):
                voff = (2 * kh + half) * LANES
                v_t = jnp.concatenate([v_refs[u + 1][0, voff:voff + LANES, :],
                                       v_refs[u][0, voff:voff + LANES, :]], axis=1)
                part = _dot(v_t, p_scr[(u * kv_heads + kh) * 2 + half])
                out_t = part if out_t is None else out_t + part
            for i in range(2):
                o_ref[0, u * w:(u + 1) * w, (2 * kh + i) * LANES:(2 * kh + i + 1) * LANES] = (
                    out_t[:, i * w:(i + 1) * w].T)


def _attn_prompt(layer, att_layer, x, mods, n1g, wq, qg, cos, sin, kx, vx, sinks, head_dim, kv_heads, w, nq):
    b, L, d = x.shape
    rows = lambda i, j: (i, j, 0)
    k_specs = [pl.BlockSpec((1, w, kx.shape[2]), lambda i, j, o=o: (i, jnp.maximum(nq * j + o, 0), 0))
               for o in range(-1, nq)]
    v_specs = [pl.BlockSpec((1, vx.shape[1], w), lambda i, j, o=o: (i, 0, jnp.maximum(nq * j + o, 0)))
               for o in range(-1, nq)]
    n_tiles = nq * 2 * kv_heads
    return pl.pallas_call(
        functools.partial(_attn_prompt_kernel, layer, att_layer, head_dim, kv_heads, nq),
        grid=(b, L // (nq * w)),
        in_specs=[pl.BlockSpec((1, nq * w, d), rows), _mod_spec(mods, layer),
                  _full(n1g.shape), _layer_spec(wq, att_layer), _full(qg.shape),
                  pl.BlockSpec((nq * w, LANES), lambda i, j: (j, 0)),
                  pl.BlockSpec((nq * w, LANES), lambda i, j: (j, 0))]
                 + k_specs + v_specs + [pl.BlockSpec(memory_space=pltpu.SMEM)],
        out_specs=pl.BlockSpec((1, nq * w, d), rows),
        out_shape=jax.ShapeDtypeStruct((b, L, d), F32),
        scratch_shapes=[pltpu.VMEM((n_tiles, w, 2 * w), F32), pltpu.VMEM((n_tiles, 2 * w, 2 * w), BF16)],
        compiler_params=_params("arbitrary", "arbitrary"),
        name="attn_prompt",
    )(x, mods, n1g, wq, qg, cos, sin, *([kx] * (nq + 1)), *([vx] * (nq + 1)), sinks)


def _proj_kernel(layer, x_ref, mod_ref, g_ref, w_ref, o_ref):
    d = x_ref.shape[1]
    mod = _mod_rows(mod_ref)
    h = _modulated_norm(x_ref[...], g_ref[layer:layer + 1, :], mod[:, 0:d], mod[:, d:2 * d])
    o_ref[...] = _dot(h.astype(BF16), w_ref[0])


def _proj(layer, x, mods, g, w, tn):
    r, d = x.shape
    n = w.shape[2]
    return pl.pallas_call(
        functools.partial(_proj_kernel, layer),
        grid=(n // tn,),
        in_specs=[_full((r, d)), _mod_spec(mods, layer), _full(g.shape),
                  pl.BlockSpec((1, d, tn), lambda j: (layer, 0, j))],
        out_specs=pl.BlockSpec((r, tn), lambda j: (0, j)),
        out_shape=jax.ShapeDtypeStruct((r, n), F32),
        compiler_params=_params("arbitrary"),
        name="proj_sample",
    )(x, mods, g, w)


def _q_sample_kernel(layer, att_layer, head_dim, x_ref, mod_ref, n1g_ref, wq_ref, qg_ref, cos_ref, sin_ref,
                     o_ref):
    blocks = _q_proj(x_ref[...], _mod_rows(mod_ref), n1g_ref[layer:layer + 1, :], wq_ref[0],
                     qg_ref[att_layer:att_layer + 1, :], cos_ref[...], sin_ref[...], head_dim)
    for j, blk in enumerate(blocks):
        o_ref[:, j * LANES:(j + 1) * LANES] = blk


def _q_sample(layer, att_layer, x, mods, n1g, wq, qg, cos, sin, head_dim):
    r, d = x.shape
    return pl.pallas_call(
        functools.partial(_q_sample_kernel, layer, att_layer, head_dim),
        grid=(1,),
        in_specs=[_full((r, d)), _mod_spec(mods, layer), _full(n1g.shape), _layer_spec(wq, att_layer),
                  _full(qg.shape), _full((1, LANES)), _full((1, LANES))],
        out_specs=_full((r, wq.shape[2])),
        out_shape=jax.ShapeDtypeStruct((r, wq.shape[2]), F32),
        compiler_params=_params("arbitrary"),
        name="q_sample",
    )(x, mods, n1g, wq, qg, cos, sin)


def _hgrn_sample_kernel(layer, heads, tb, steps, aliased, proj_ref, lbraw_ref, gng_ref, s_ref, *rest):
    o_ref, so_ref, stack_scr, acc_scr = rest[1:] if aliased else rest
    i = pl.program_id(0)
    rows = proj_ref.shape[0]
    d = proj_ref.shape[1] // 4
    dk = d // heads

    def update():
        @pl.when(i == 0)
        def _():
            lb = _lower_bound(lbraw_ref[...], layer)
            q, fg = _hgrn_gates(proj_ref[:, 0:2 * d], lb, d)
            for hh in range(heads):
                sl = slice(hh * dk, (hh + 1) * dk)
                ft = fg[:, sl].T
                hi = ft.astype(BF16)
                stack_scr[hh, 0:dk] = hi
                stack_scr[hh, dk:2 * dk] = (ft - hi.astype(F32)).astype(BF16)
                stack_scr[hh, 2 * dk:3 * dk] = (1.0 - ft).astype(BF16)
                stack_scr[hh, 3 * dk:4 * dk] = q[:, sl].T.astype(BF16)

        token_row = lax.broadcasted_iota(jnp.int32, (rows, dk), 0)
        sub = lax.broadcasted_iota(jnp.int32, (tb, dk), 0)
        base = pl.multiple_of(i * tb, tb)
        v_rows = proj_ref[pl.ds(base, tb), 2 * d:3 * d]
        o_rows = [jnp.zeros((tb, dk), F32) for _ in range(heads)]
        for t0 in range(0, tb, 2):
            onehot = jnp.concatenate([jnp.where(token_row == base + t, 1.0, 0.0) for t in (t0, t0 + 1)],
                                     axis=1).astype(BF16)
            for hh in range(heads):
                bc2 = _dot(stack_scr[hh], onehot)
                for t in (t0, t0 + 1):
                    bc = bc2[:, (t - t0) * dk:(t - t0 + 1) * dk]
                    f_b = bc[0:dk] + bc[dk:2 * dk]
                    k_b = bc[2 * dk:3 * dk]
                    q_b = bc[3 * dk:4 * dk]
                    s_new = f_b * s_ref[0, t, hh] + k_b * v_rows[t:t + 1, hh * dk:(hh + 1) * dk]
                    so_ref[0, t, hh] = s_new
                    o_rows[hh] = jnp.where(sub == t, jnp.sum(q_b * s_new, axis=0, keepdims=True), o_rows[hh])
        for hh in range(heads):
            acc_scr[pl.ds(base, tb), hh * dk:(hh + 1) * dk] = o_rows[hh]

        @pl.when(i == steps - 1)
        def _():
            gng = gng_ref[layer:layer + 1, :]
            for hh in range(heads):
                sl = slice(hh * dk, (hh + 1) * dk)
                o_ref[:, sl] = _gated_group_norm(acc_scr[:, sl], gng,
                                                 proj_ref[:, 3 * d + hh * dk:3 * d + (hh + 1) * dk])

    if aliased:
        update()
    else:
        pl.when(i < steps)(update)

        @pl.when(i >= steps)
        def _():
            so_ref[...] = jnp.zeros_like(so_ref)


def _hgrn_sample(layer, proj, lb_raw, gn_g, state, new_state, tb):
    n_layers, nb, heads, dk, dv = state.shape
    d = proj.shape[1] // 4
    steps = nb // tb
    aliased = new_state is not None
    assert aliased == (layer > 0) and tb == SUBLANES
    if aliased:
        grid = (steps,)
        s_in = s_out = pl.BlockSpec((1, tb, heads, dk, dv), lambda i: (layer, i, 0, 0, 0))
    else:
        grid = (n_layers * steps,)
        s_in = pl.BlockSpec((1, tb, heads, dk, dv), lambda i: (0, jnp.minimum(i, steps - 1), 0, 0, 0))
        s_out = pl.BlockSpec((1, tb, heads, dk, dv), lambda i: (i // steps, i % steps, 0, 0, 0))
    in_specs = [_full(proj.shape), _full(lb_raw.shape), _full(gn_g.shape), s_in]
    args = [proj, lb_raw, gn_g, state]
    if aliased:
        in_specs.append(pl.BlockSpec(memory_space=pl.ANY))
        args.append(new_state)
    return pl.pallas_call(
        functools.partial(_hgrn_sample_kernel, layer, heads, tb, steps, aliased),
        grid=grid,
        in_specs=in_specs,
        out_specs=[_full((nb, d)), s_out],
        out_shape=[jax.ShapeDtypeStruct((nb, d), F32), jax.ShapeDtypeStruct(state.shape, F32)],
        scratch_shapes=[pltpu.VMEM((heads, 4 * dk, nb), BF16), pltpu.VMEM((nb, d), F32)],
        input_output_aliases={4: 1} if aliased else {},
        compiler_params=_params("arbitrary"),
        name=f"hgrn_sample_{layer}",
    )(*args)


def _cache_roll_kernel(ck_ref, cv_ref, kn_ref, vn_ref, ko_ref, vo_ref, kb_ref, vb_ref):
    w = ck_ref.shape[1]
    ko_ref[:, 0:w - 1, :] = ck_ref[:, 1:w, :]
    ko_ref[:, w - 1:w, :] = kn_ref[...]
    vo_ref[:, 0:w - 1, :] = cv_ref[:, 1:w, :]
    vo_ref[:, w - 1:w, :] = vn_ref[...]
    kb_ref[...] = ko_ref[...].astype(BF16)
    vb_ref[...] = vo_ref[...].astype(BF16)


def _cache_roll(ck, cv, kn, vn, tb):
    nb, w, n = ck.shape
    blk = pl.BlockSpec((tb, w, n), lambda i: (i, 0, 0))
    new = pl.BlockSpec((tb, 1, n), lambda i: (i, 0, 0))
    return pl.pallas_call(
        _cache_roll_kernel,
        grid=(nb // tb,),
        in_specs=[blk, blk, new, new],
        out_specs=[blk, blk, blk, blk],
        out_shape=[jax.ShapeDtypeStruct(ck.shape, F32), jax.ShapeDtypeStruct(cv.shape, F32),
                   jax.ShapeDtypeStruct(ck.shape, BF16), jax.ShapeDtypeStruct(cv.shape, BF16)],
        compiler_params=_params("arbitrary"),
        name="cache_roll",
    )(ck, cv, kn.reshape(nb, 1, n), vn.reshape(nb, 1, n))


def _attn_sample_kernel(att_layer, tb, q_ref, k_ref, v_ref, sink_ref, o_ref):
    heads, head_dim = q_ref.shape[1], q_ref.shape[2]
    nkv = k_ref.shape[2]
    group = heads // (nkv // head_dim)
    r = lax.broadcasted_iota(jnp.int32, (heads, nkv), 0) // group
    c = lax.broadcasted_iota(jnp.int32, (heads, nkv), 1) // head_dim
    own = r == c
    sink = sink_ref[:, att_layer:att_layer + 1][None]
    q = q_ref[...]
    qe = jnp.where(own[None], jnp.concatenate([q] * (nkv // head_dim), axis=-1), 0.0).astype(BF16)
    s = jnp.einsum("thc,tjc->thj", qe, k_ref[...], preferred_element_type=F32)
    m = jnp.maximum(jnp.max(s, axis=-1, keepdims=True), sink)
    p = jnp.exp(s - m)
    denom = jnp.sum(p, axis=-1, keepdims=True) + jnp.exp(sink - m)
    pv = jnp.where(own[None], jnp.einsum("thj,tjc->thc", p.astype(BF16), v_ref[...],
                                         preferred_element_type=F32), 0.0)
    o = pv[:, :, 0:head_dim]
    for j in range(1, nkv // head_dim):
        o = o + pv[:, :, j * head_dim:(j + 1) * head_dim]
    o_ref[...] = o / denom


def _attn_sample(att_layer, q3, k, v, sinks_t, tb):
    nb, heads, head_dim = q3.shape
    w, nkv = k.shape[1], k.shape[2]
    return pl.pallas_call(
        functools.partial(_attn_sample_kernel, att_layer, tb),
        grid=(nb // tb,),
        in_specs=[pl.BlockSpec((tb, heads, head_dim), lambda i: (i, 0, 0)),
                  pl.BlockSpec((tb, w, nkv), lambda i: (i, 0, 0)),
                  pl.BlockSpec((tb, w, nkv), lambda i: (i, 0, 0)),
                  _full(sinks_t.shape)],
        out_specs=pl.BlockSpec((tb, heads, head_dim), lambda i: (i, 0, 0)),
        out_shape=jax.ShapeDtypeStruct(q3.shape, F32),
        compiler_params=_params("arbitrary"),
        name="attn_sample",
    )(q3, k, v, sinks_t)


def _rope_tables(pos, head_dim):
    half = head_dim // 2
    inv = ROPE_THETA ** (-jnp.arange(half, dtype=F32) / half)
    ang = pos.astype(F32)[:, None] * inv[None, :]
    cos, sin = jnp.cos(ang), jnp.sin(ang)
    reps = LANES // head_dim
    return (jnp.tile(jnp.concatenate([cos, cos], axis=1), (1, reps)),
            jnp.tile(jnp.concatenate([-sin, sin], axis=1), (1, reps)))


def kernel(x_prompt, x_sample, c_prompt, c_sample, state_hgrn, cache_k, cache_v, w_ada, b_ada, norm1_g, norm2_g, hg_w_in, hg_w_out, hg_lower_bounds, hg_gn_g, kv_w_ada, kv_b_ada, kv_norm_g, w_kv, k_norm_g, w_q, q_norm_g, sinks, w_o, w_up, w_down):
    bp, Lp, d = x_prompt.shape
    bs = x_sample.shape[0]
    depth = w_ada.shape[0]
    n_a = hg_w_in.shape[0]
    heads = state_hgrn.shape[2]
    window, kv_heads, head_dim = cache_k.shape[1], cache_k.shape[2], cache_k.shape[3]
    assert LANES == 2 * head_dim and kv_heads % 2 == 0 and Lp % window == 0
    assert window & (window - 1) == 0 and (d // head_dim) // kv_heads == 4
    nkv = kv_heads * head_dim
    reps = LANES // head_dim
    bf = lambda t: t.astype(BF16)

    pad = (-(bs + bp)) % 16
    c_all = jnp.concatenate([c_sample, c_prompt, jnp.zeros((pad, d), F32)], axis=0)
    mods_s, mods_p = _ada(c_all, bs, bp, w_ada, b_ada, T.ada_cols)
    kv_mods_s, kv_mods_p = _ada(c_all, bs, bp, kv_w_ada[None], kv_b_ada[None], T.kv_ada_cols)

    cos_p, sin_p = _rope_tables(jnp.arange(Lp), head_dim)
    cos_s, sin_s = _rope_tables(jnp.full((1,), PAST_LEN), head_dim)
    kg = jnp.tile(k_norm_g, reps)[None]
    qg = jnp.tile(q_norm_g, (1, reps))
    sinks_t = sinks.T
    w_kv_b, w_in_b, w_out_b, w_q_b, w_o_b = bf(w_kv), bf(hg_w_in), bf(hg_w_out), bf(w_q), bf(w_o)
    w_up_b, w_dn_b = bf(w_up), bf(w_down)

    xp = x_prompt
    xs = x_sample.reshape(bs, d)
    hg_p, hg_s = [], None
    k_p = v_p = kx_p = vx_p = k_s = v_s = None
    for l in range(depth):
        if l == n_a:
            k_p, v_p, kx_p, vx_p = _kv(xp, kv_mods_p, kv_norm_g[None], w_kv_b, kg, cos_p, sin_p,
                                       head_dim, window, T.kv_rows)
            k_n, v_n, _, _ = _kv(xs[None], kv_mods_s, kv_norm_g[None], w_kv_b, kg, cos_s, sin_s,
                                 head_dim, bs, bs)
            k_s, v_s, kb_s, vb_s = _cache_roll(cache_k.reshape(bs, window, nkv),
                                               cache_v.reshape(bs, window, nkv), k_n[0], v_n[0], T.cache_tokens)
        if l < n_a:
            w_mix_b, mix_layer = w_out_b, l
            a_p, s_p = _hgrn_prompt(l, xp, mods_p, norm1_g, w_in_b, hg_lower_bounds, hg_gn_g, heads,
                                    T.hgrn_chunk, T.hgrn_rows)
            hg_p.append(s_p)
            proj_s = _proj(l, xs, mods_s, norm1_g, w_in_b, T.proj_cols)
            a_s, hg_s = _hgrn_sample(l, proj_s, hg_lower_bounds, hg_gn_g, state_hgrn, hg_s, SUBLANES)
        else:
            j = l - n_a
            w_mix_b, mix_layer = w_o_b, j
            a_p = _attn_prompt(l, j, xp, mods_p, norm1_g, w_q_b, qg, cos_p, sin_p, kx_p, vx_p, sinks,
                               head_dim, kv_heads, window, T.attn_blocks)
            q_s = _q_sample(l, j, xs, mods_s, norm1_g, w_q_b, qg, cos_s, sin_s, head_dim)
            a_s = _attn_sample(j, q_s.reshape(bs, d // head_dim, head_dim), kb_s, vb_s, sinks_t,
                               T.attn_sample_tokens)
            a_s = a_s.reshape(bs, d)
        xp = _post(l, xp, a_p, mods_p, norm2_g, w_mix_b, mix_layer, w_up_b, w_dn_b, T.mlp_rows)
        xs = _post_sample(l, xs, a_s, mods_s, norm2_g, w_mix_b, mix_layer, w_up_b, w_dn_b, T.mlp_sample_slab)

    shape4 = lambda t: t.reshape(t.shape[0], window, kv_heads, head_dim)
    return (xp, xs.reshape(bs, 1, d), jnp.stack(hg_p), shape4(k_p), shape4(v_p), hg_s,
            shape4(k_s), shape4(v_s))
```

```python
import functools
import math
from typing import NamedTuple

import numpy as np
import jax
import jax.numpy as jnp
from jax import lax
from jax.experimental import pallas as pl
from jax.experimental.pallas import tpu as pltpu

F32 = jnp.float32
BF16 = jnp.bfloat16

PAST_LEN = 8192
ROPE_THETA = 10000.0
EPS = 1e-6
LOG2E = 1.4426950408889634
LANES = 128
SUBLANES = 8
FIRST_TABLE_LEVEL = 2
VMEM_LIMIT = 56 * 1024 * 1024


class _Tiles(NamedTuple):
    ada_cols: int = 3072
    kv_ada_cols: int = 2048
    hgrn_chunk: int = 128
    hgrn_rows: int = 512
    mlp_rows: int = 512
    kv_rows: int = 1024
    attn_blocks: int = 8
    proj_cols: int = 2048
    cache_tokens: int = 16
    attn_sample_tokens: int = 32
    mlp_sample_slab: int = 2048


T = _Tiles()

NT_DIMS = (((1,), (1,)), ((), ()))
TN_DIMS = (((0,), (0,)), ((), ()))


def _dot(a, b):
    return jnp.dot(a, b, preferred_element_type=F32)


def _dot_nt(a, b):
    return lax.dot_general(a, b, NT_DIMS, preferred_element_type=F32)


def _sigmoid(x):
    return 1.0 / (1.0 + jnp.exp(-x))


def _silu(x):
    return x * (0.5 + 0.5 * jnp.tanh(0.5 * x))


def _rms(x, g):
    ms = jnp.mean(x * x, axis=-1, keepdims=True)
    return x * lax.rsqrt(ms + EPS) * g


def _modulated_norm(x, gain, shift, scale):
    return _rms(x, gain) * (1.0 + scale) + shift


def _params(*sem):
    return pltpu.CompilerParams(dimension_semantics=sem, vmem_limit_bytes=VMEM_LIMIT)


def _full(shape):
    n = len(shape)
    return pl.BlockSpec(shape, lambda *_: (0,) * n)


def _ada_kernel(c_ref, w_ref, b_ref, os_ref, op_ref):
    bs, bp = os_ref.shape[1], op_ref.shape[1]
    c = c_ref[...]
    a = _silu(c).astype(BF16)
    res = _dot(a, w_ref[0].astype(BF16)) + b_ref[0]
    os_ref[0] = res[0:bs]
    for r in range(bp):
        op_ref[0, r] = jnp.broadcast_to(res[bs + r:bs + r + 1], op_ref.shape[2:])


def _ada(c_all, bs, bp, w, b, tn):
    nl, d, n = w.shape
    r = c_all.shape[0]
    return pl.pallas_call(
        _ada_kernel,
        grid=(nl, n // tn),
        in_specs=[pl.BlockSpec((r, d), lambda l, j: (0, 0)),
                  pl.BlockSpec((1, d, tn), lambda l, j: (l, 0, j)),
                  pl.BlockSpec((1, 1, tn), lambda l, j: (l, 0, j))],
        out_specs=[pl.BlockSpec((1, bs, tn), lambda l, j: (l, 0, j)),
                   pl.BlockSpec((1, bp, SUBLANES, tn), lambda l, j: (l, 0, 0, j))],
        out_shape=[jax.ShapeDtypeStruct((nl, bs, n), F32),
                   jax.ShapeDtypeStruct((nl, bp, SUBLANES, n), F32)],
        compiler_params=_params("arbitrary", "arbitrary"),
        name="ada",
    )(c_all, w, b.reshape(nl, 1, n))


def _mod_rows(mod_ref):
    return mod_ref[0, 0, 0:1, :] if len(mod_ref.shape) == 4 else mod_ref[0]


def _mod_spec(mods, layer):
    if mods.ndim == 4:
        return pl.BlockSpec((1, 1) + mods.shape[2:], lambda i, *_: (layer, i, 0, 0))
    return pl.BlockSpec((1,) + mods.shape[1:], lambda *_: (layer, 0, 0))


def _layer_spec(stack, layer):
    zeros = (0,) * (stack.ndim - 1)
    return pl.BlockSpec((1,) + stack.shape[1:], lambda *_: (layer,) + zeros, pipeline_mode=pl.Buffered(1))


def _group_mean_matrix(group, width):
    r = lax.broadcasted_iota(jnp.int32, (width, width), 0) // group
    c = lax.broadcasted_iota(jnp.int32, (width, width), 1) // group
    return jnp.where(r == c, 1.0 / group, 0.0).astype(BF16)


def _heads_norm_rope(y, gain, cos, sin, head_dim, out_scale):
    rows, n = y.shape
    width = 2 * LANES if n % (2 * LANES) == 0 else LANES
    mean = _group_mean_matrix(head_dim, width)
    gain = gain * out_scale
    half = head_dim // 2
    first = (lax.broadcasted_iota(jnp.int32, (rows, LANES), 1) % head_dim) < half
    out = []
    for j in range(0, n, width):
        ms = _dot(jnp.square(y[:, j:j + width]).astype(BF16), mean)
        for i in range(0, width, LANES):
            yn = y[:, j + i:j + i + LANES] * lax.rsqrt(ms[:, i:i + LANES] + EPS) * gain
            rot = jnp.where(first, pltpu.roll(yn, LANES - half, 1), pltpu.roll(yn, half, 1))
            out.append(yn * cos + rot * sin)
    return out


def _lower_bound(raw, layer):
    m = jnp.max(raw, axis=0, keepdims=True)
    e = jnp.exp(raw - m)
    sm = e / jnp.sum(e, axis=0, keepdims=True)
    acc = sm[0:1]
    for j in range(1, layer + 1):
        acc = acc + sm[j:j + 1]
    return acc - sm[0:1]


def _hgrn_gates(proj, lb, d):
    qa = proj[:, 0:d]
    q = _silu(qa)
    fg = lb + (1.0 - lb) * _sigmoid(proj[:, d:2 * d])
    return q, fg


def _gated_group_norm(o, gain, gate_pre):
    ms = jnp.mean(o * o, axis=-1, keepdims=True)
    return o * lax.rsqrt(ms + EPS) * gain * _silu(gate_pre)


def _hgrn_prompt_kernel(layer, heads, x_ref, mod_ref, n1g_ref, win_ref, lbraw_ref, gng_ref,
                        lev_ref, sums_ref, o_ref, s_ref, st_scr, d_scr, xs_scr, att_scr):
    c = pl.program_id(1)
    total, d = x_ref.shape[1], x_ref.shape[2]
    rows = lev_ref.shape[0]
    dk = d // heads
    n_levels = int(math.log2(rows))

    @pl.when(c == 0)
    def _():
        st_scr[...] = jnp.zeros_like(st_scr)

    mod = _mod_rows(mod_ref)
    h = _modulated_norm(x_ref[0], n1g_ref[layer:layer + 1, :], mod[:, 0:d], mod[:, d:2 * d])
    proj_all = _dot(h.astype(BF16), win_ref[0])
    lb = _lower_bound(lbraw_ref[...], layer)
    q_all, fg_all = _hgrn_gates(proj_all, lb, d)
    lf2_all = jnp.log2(fg_all)
    n_table = d_scr.shape[0] // rows - 1 + FIRST_TABLE_LEVEL - 1
    cum_row = d_scr.shape[0] - rows
    lev = lev_ref[...]
    row_id = lax.broadcasted_iota(jnp.int32, (rows, dk), 0)
    upper = [((row_id >> p) & 1) == 1 for p in range(n_table + 1)]
    tile_row = lax.broadcasted_iota(jnp.int32, (rows // SUBLANES, SUBLANES, dk), 1) & 3
    col = lax.broadcasted_iota(jnp.int32, (1, rows), 1)
    gng = gng_ref[layer:layer + 1, :]

    for r0 in range(0, total, rows):
        proj, q, fg = (t[r0:r0 + rows] for t in (proj_all, q_all, fg_all))
        k = 1.0 - fg
        lf2 = lf2_all[r0:r0 + rows]
        hi = lf2.astype(BF16)
        lo = (lf2 - hi.astype(F32)).astype(BF16)
        d_scr[...] = _dot(sums_ref[...], jnp.concatenate([hi, lo], axis=0))

        for hh in range(heads):
            sl = slice(hh * dk, (hh + 1) * dk)
            q_h, k_h = q[:, sl], k[:, sl]
            cum = d_scr[cum_row:cum_row + rows, sl]
            f_h = fg[:, sl]
            xs_scr[hh, 0] = jnp.where(upper[0], q_h * f_h, k_h).astype(BF16)
            f_t = f_h.reshape(rows // SUBLANES, SUBLANES, dk)
            e1 = jnp.where(tile_row == 3, f_t * pltpu.roll(f_t, 1, 1),
                           jnp.where(tile_row == 2, f_t,
                                     jnp.where(tile_row == 1, 1.0, pltpu.roll(f_t, SUBLANES - 1, 1))))
            xs_scr[hh, 1] = (jnp.where(upper[1], q_h, k_h) * e1.reshape(rows, dk)).astype(BF16)
            cum_last = cum[rows - 1:rows, :]
            edge = 2 << n_table
            q_dec = [q_h[0:edge] * jnp.exp2(cum[0:edge])]
            k_dec = [k_h[rows - edge:rows] * jnp.exp2(cum_last - cum[rows - edge:rows])]
            for p in range(FIRST_TABLE_LEVEL, n_levels):
                m = 1 << p
                if p <= n_table:
                    e = jnp.exp2(d_scr[(p - FIRST_TABLE_LEVEL) * rows:(p - FIRST_TABLE_LEVEL + 1) * rows, sl])
                    xs_scr[hh, p] = (jnp.where(upper[p], q_h, k_h) * e).astype(BF16)
                    continue
                pieces = []
                for r in range(0, rows, m):
                    ref = (r // (2 * m)) * 2 * m + m - 1
                    if (r // m) % 2:
                        pieces.append(q_h[r:r + m] * jnp.exp2(cum[r:r + m] - cum[ref:ref + 1]))
                    else:
                        pieces.append(k_h[r:r + m] * jnp.exp2(cum[ref:ref + 1] - cum[r:r + m]))
                xs_scr[hh, p] = jnp.concatenate(pieces, axis=0).astype(BF16)
                q_dec.append(pieces[1] * jnp.exp2(cum[m - 1:m]))
                k_dec.insert(0, pieces[rows // m - 2] * jnp.exp2(cum_last - cum[rows - m - 1:rows - m]))
            xs_scr[hh, n_levels] = jnp.concatenate(q_dec, axis=0).astype(BF16)
            xs_scr[hh, n_levels + 1] = jnp.concatenate(k_dec, axis=0).astype(BF16)

        for hh in range(heads):
            xs = xs_scr[hh, 0]
            att = jnp.where(lev == 0, _dot_nt(xs, xs), 0.0)
            for p in range(1, n_levels):
                m = 1 << p
                xs = xs_scr[hh, p]
                pp = _dot_nt(xs, xs)
                if p <= n_table:
                    att = jnp.where(lev == p, pp, att)
                else:
                    att = jnp.concatenate(
                        [jnp.where((col >= r - m) & (col < r), pp[r:r + m], att[r:r + m]) if (r // m) % 2
                         else att[r:r + m] for r in range(0, rows, m)], axis=0)
            att_scr[hh] = att.astype(BF16)

        for hh in range(heads):
            sl = slice(hh * dk, (hh + 1) * dk)
            v_f = proj[:, 2 * d + hh * dk:2 * d + (hh + 1) * dk]
            v_h = v_f.astype(BF16)
            cum_last = d_scr[cum_row + rows - 1:cum_row + rows, sl]
            st = st_scr[hh]
            o_h = (_dot(jnp.concatenate([att_scr[hh], xs_scr[hh, n_levels]], axis=1),
                        jnp.concatenate([v_h, st.T.astype(BF16)], axis=0))
                   + jnp.sum(q[:, sl] * k[:, sl], axis=-1, keepdims=True) * v_f)
            st_scr[hh] = st * jnp.exp2(cum_last) + lax.dot_general(
                v_h, xs_scr[hh, n_levels + 1], TN_DIMS, preferred_element_type=F32)
            g_pre = proj[:, 3 * d + hh * dk:3 * d + (hh + 1) * dk]
            o_ref[0, r0:r0 + rows, sl] = _gated_group_norm(o_h, gng, g_pre)

    @pl.when(c == pl.num_programs(1) - 1)
    def _():
        for hh in range(heads):
            s_ref[0, hh] = st_scr[hh].T


def _level_table(rows):
    t = np.arange(rows)[:, None]
    s = np.arange(rows)[None, :]
    x = t ^ s
    lev = np.where(t > s, np.floor(np.log2(np.maximum(x, 1))).astype(np.int32), np.where(t == s, -1, -2))
    return jnp.asarray(lev, dtype=jnp.int32)


def _sum_table(rows):
    t = np.arange(rows)[:, None]
    j = np.arange(rows)[None, :]
    blocks = []
    for p in range(FIRST_TABLE_LEVEL, int(math.log2(SUBLANES))):
        m = 1 << p
        ref = (t // (2 * m)) * (2 * m) + m - 1
        up = ((t >> p) & 1) == 1
        blocks.append(np.where(up, (j > ref) & (j <= t), (j > t) & (j <= ref)))
    blocks.append(j <= t)
    table = np.concatenate(blocks, axis=0).astype(np.float32)
    return jnp.asarray(np.concatenate([table, table], axis=1), dtype=BF16)


def _hgrn_prompt(layer, x, mods, n1g, w_in, lb_raw, gn_g, heads, chunk, step_rows):
    b, L, d = x.shape
    dk = d // heads
    sums = _sum_table(chunk)
    return pl.pallas_call(
        functools.partial(_hgrn_prompt_kernel, layer, heads),
        grid=(b, L // step_rows),
        in_specs=[pl.BlockSpec((1, step_rows, d), lambda i, c: (i, c, 0)),
                  _mod_spec(mods, layer),
                  _full(n1g.shape), _layer_spec(w_in, layer), _full(lb_raw.shape), _full(gn_g.shape),
                  _full((chunk, chunk)), _full(sums.shape)],
        out_specs=[pl.BlockSpec((1, step_rows, d), lambda i, c: (i, c, 0)),
                   pl.BlockSpec((1, heads, dk, dk), lambda i, c: (i, 0, 0, 0))],
        out_shape=[jax.ShapeDtypeStruct((b, L, d), F32),
                   jax.ShapeDtypeStruct((b, heads, dk, dk), F32)],
        scratch_shapes=[pltpu.VMEM((heads, dk, dk), F32), pltpu.VMEM((sums.shape[0], d), F32),
                        pltpu.VMEM((heads, int(math.log2(chunk)) + 2, chunk, dk), BF16),
                        pltpu.VMEM((heads, chunk, chunk), BF16)],
        compiler_params=_params("arbitrary", "arbitrary"),
        name=f"hgrn_prompt_{layer}",
    )(x, mods, n1g, w_in, lb_raw, gn_g, _level_table(chunk), sums)


def _post_kernel(layer, x_ref, a_ref, mod_ref, n2g_ref, wo_ref, wup_ref, wdn_ref, o_ref):
    d = x_ref.shape[2]
    mod = _mod_rows(mod_ref)
    g1, sh2, sc2, g2 = (mod[:, j * d:(j + 1) * d] for j in range(2, 6))
    x1 = x_ref[0] + g1 * _dot(a_ref[0].astype(BF16), wo_ref[0])
    h2 = _modulated_norm(x1, n2g_ref[layer:layer + 1, :], sh2, sc2).astype(BF16)
    y = None
    for c in range(0, wup_ref.shape[2], d):
        u = jnp.square(jnp.maximum(_dot(h2, wup_ref[0, :, c:c + d]), 0.0)).astype(BF16)
        part = _dot(u, wdn_ref[0, c:c + d, :])
        y = part if y is None else y + part
    o_ref[0] = x1 + g2 * y


def _post_sample_kernel(layer, x_ref, a_ref, mod_ref, n2g_ref, wo_ref, wup_ref, wdn_ref, o_ref,
                        x1_scr, h2_scr, y_scr):
    c = pl.program_id(0)
    d = x_ref.shape[1]
    mod = _mod_rows(mod_ref)

    @pl.when(c == 0)
    def _():
        g1, sh2, sc2 = (mod[:, j * d:(j + 1) * d] for j in range(2, 5))
        x1 = x_ref[...] + g1 * _dot(a_ref[...].astype(BF16), wo_ref[0])
        x1_scr[...] = x1
        h2_scr[...] = _modulated_norm(x1, n2g_ref[layer:layer + 1, :], sh2, sc2).astype(BF16)
        y_scr[...] = jnp.zeros_like(y_scr)

    u = jnp.square(jnp.maximum(_dot(h2_scr[...], wup_ref[0]), 0.0)).astype(BF16)
    y_scr[...] += _dot(u, wdn_ref[0])

    @pl.when(c == pl.num_programs(0) - 1)
    def _():
        o_ref[...] = x1_scr[...] + mod[:, 5 * d:6 * d] * y_scr[...]


def _post_sample(layer, x, a, mods, n2g, w_o, o_layer, w_up, w_dn, slab):
    r, d = x.shape
    ff = w_up.shape[2]
    return pl.pallas_call(
        functools.partial(_post_sample_kernel, layer),
        grid=(ff // slab,),
        in_specs=[_full((r, d)), _full((r, d)), _mod_spec(mods, layer), _full(n2g.shape),
                  _layer_spec(w_o, o_layer),
                  pl.BlockSpec((1, d, slab), lambda c: (layer, 0, c)),
                  pl.BlockSpec((1, slab, d), lambda c: (layer, c, 0))],
        out_specs=_full((r, d)),
        out_shape=jax.ShapeDtypeStruct((r, d), F32),
        scratch_shapes=[pltpu.VMEM((r, d), F32), pltpu.VMEM((r, d), BF16), pltpu.VMEM((r, d), F32)],
        compiler_params=_params("arbitrary"),
        name="post_mlp_sample",
    )(x, a, mods, n2g, w_o, w_up, w_dn)


def _post(layer, x, a, mods, n2g, w_o, o_layer, w_up, w_dn, tm):
    nb, L, d = x.shape
    return pl.pallas_call(
        functools.partial(_post_kernel, layer),
        grid=(nb, L // tm),
        in_specs=[pl.BlockSpec((1, tm, d), lambda i, j: (i, j, 0)),
                  pl.BlockSpec((1, tm, d), lambda i, j: (i, j, 0)),
                  _mod_spec(mods, layer), _full(n2g.shape),
                  _layer_spec(w_o, o_layer), _layer_spec(w_up, layer), _layer_spec(w_dn, layer)],
        out_specs=pl.BlockSpec((1, tm, d), lambda i, j: (i, j, 0)),
        out_shape=jax.ShapeDtypeStruct((nb, L, d), F32),
        compiler_params=_params("arbitrary", "arbitrary"),
        name="post_mlp",
    )(x, a, mods, n2g, w_o, w_up, w_dn)


def _kv_kernel(head_dim, window, x_ref, mod_ref, g_ref, w_ref, kg_ref, cos_ref, sin_ref,
               k_ref, v_ref, kx_ref, vx_ref):
    tm, d = x_ref.shape[1], x_ref.shape[2]
    mod = _mod_rows(mod_ref)
    hk = _modulated_norm(x_ref[0], g_ref[...], mod[:, 0:d], mod[:, d:2 * d])
    y = _dot(hk.astype(BF16), w_ref[...])
    n = y.shape[1] // 2
    cos, sin, kg = cos_ref[...], sin_ref[...], kg_ref[...]
    low = lax.broadcasted_iota(jnp.int32, (tm, LANES), 1) < head_dim
    k_blocks, v_blocks = _heads_norm_rope(y[:, 0:n], kg, cos, sin, head_dim, 1.0), []
    for j in range(n // LANES):
        kb = k_blocks[j]
        vb = y[:, n + j * LANES:n + (j + 1) * LANES]
        v_blocks.append(vb)
        kx_ref[0, :, j * LANES:(j + 1) * LANES] = kb.astype(BF16)
        kx_ref[0, :, n + j * LANES:n + (j + 1) * LANES] = pltpu.roll(kb, head_dim, 1).astype(BF16)
        vs = pltpu.roll(vb, head_dim, 1)
        variants = (jnp.where(low, vb, 0.0), jnp.where(low, 0.0, vs),
                    jnp.where(low, vs, 0.0), jnp.where(low, 0.0, vb))
        for i, var in enumerate(variants):
            off = (4 * j + i) * LANES
            vx_ref[0, off:off + LANES, :] = var.astype(BF16).T

    @pl.when(pl.program_id(1) == pl.num_programs(1) - 1)
    def _():
        for j in range(n // LANES):
            k_ref[0, :, j * LANES:(j + 1) * LANES] = k_blocks[j][tm - window:, :]
            v_ref[0, :, j * LANES:(j + 1) * LANES] = v_blocks[j][tm - window:, :]


def _kv(x, mods, g, w_kv, kg, cos, sin, head_dim, window, tm):
    nb, L, d = x.shape
    n = w_kv.shape[1] // 2
    rope_rows = tm if cos.shape[0] > 1 else 1
    rope_map = (lambda i, j: (j, 0)) if cos.shape[0] > 1 else (lambda i, j: (0, 0))
    last = lambda i, j: (i, 0, 0)
    tile = lambda i, j: (i, j, 0)
    return pl.pallas_call(
        functools.partial(_kv_kernel, head_dim, window),
        grid=(nb, L // tm),
        in_specs=[pl.BlockSpec((1, tm, d), tile), _mod_spec(mods, 0),
                  _full((1, d)), _full(w_kv.shape), _full((1, LANES)),
                  pl.BlockSpec((rope_rows, LANES), rope_map),
                  pl.BlockSpec((rope_rows, LANES), rope_map)],
        out_specs=[pl.BlockSpec((1, window, n), last), pl.BlockSpec((1, window, n), last),
                   pl.BlockSpec((1, tm, 2 * n), tile), pl.BlockSpec((1, 4 * n, tm), lambda i, j: (i, 0, j))],
        out_shape=[jax.ShapeDtypeStruct((nb, window, n), F32), jax.ShapeDtypeStruct((nb, window, n), F32),
                   jax.ShapeDtypeStruct((nb, L, 2 * n), BF16), jax.ShapeDtypeStruct((nb, 4 * n, L), BF16)],
        compiler_params=_params("arbitrary", "arbitrary"),
        name="kv_proj",
    )(x, mods, g, w_kv, kg, cos, sin)


def _q_proj(x, mod, n1g, wq, qg, cos, sin, head_dim, extra_scale=1.0):
    d = x.shape[1]
    h = _modulated_norm(x, n1g, mod[:, 0:d], mod[:, d:2 * d])
    y = _dot(h.astype(BF16), wq)
    return _heads_norm_rope(y, qg, cos, sin, head_dim, extra_scale / math.sqrt(head_dim))


def _attn_prompt_kernel(layer, att_layer, head_dim, kv_heads, nq, x_ref, mod_ref, n1g_ref, wq_ref, qg_ref,
                        cos_ref, sin_ref, *rest):
    k_refs, v_refs = rest[0:nq + 1], rest[nq + 1:2 * nq + 2]
    sink_ref, o_ref, s_scr, p_scr = rest[2 * nq + 2:]
    w = x_ref.shape[1] // nq
    nkv = kv_heads * head_dim
    q_blocks = _q_proj(x_ref[0], _mod_rows(mod_ref), n1g_ref[layer:layer + 1, :], wq_ref[0],
                       qg_ref[att_layer:att_layer + 1, :], cos_ref[...], sin_ref[...], head_dim, LOG2E)
    group = (2 * len(q_blocks)) // kv_heads
    key = lax.broadcasted_iota(jnp.int32, (w, 2 * w), 0)
    qry = lax.broadcasted_iota(jnp.int32, (w, 2 * w), 1)
    own = key <= (qry & (w - 1))
    first = lax.broadcasted_iota(jnp.int32, (1, 2 * w), 1) < w
    low = lax.broadcasted_iota(jnp.int32, (w, LANES), 1) < head_dim
    no_prev = jnp.where(pl.program_id(1) == 0, -jnp.inf, 0.0)
    tiles = [(u, kh, half) for u in range(nq) for kh in range(kv_heads) for half in range(2)]
    for n, (u, kh, half) in enumerate(tiles):
        ha, hb = kh * group + half, kh * group + 2 + half
        mask = (lambda t: jnp.where(low, t, 0.0)) if half == 0 else (lambda t: jnp.where(low, 0.0, t))
        rows = slice(u * w, (u + 1) * w)
        qcat = jnp.concatenate([mask(q_blocks[ha // 2][rows]), mask(q_blocks[hb // 2][rows])],
                               axis=0).astype(BF16)
        koff = (kh // 2) * LANES + (0 if kh % 2 == half else nkv)
        kcat = jnp.concatenate([k_refs[u][0, :, koff:koff + LANES], k_refs[u + 1][0, :, koff:koff + LANES]],
                               axis=0)
        s_t = _dot_nt(kcat, qcat)
        s_prev = s_t[:w] + no_prev if u == 0 else s_t[:w]
        s_scr[n] = jnp.where(own, s_t[w:], s_prev)
    for n, (u, kh, half) in enumerate(tiles):
        sc = s_scr[n]
        sink = jnp.where(first, sink_ref[att_layer, kh * group + half],
                         sink_ref[att_layer, kh * group + 2 + half]) * LOG2E
        m = jnp.maximum(jnp.max(sc, axis=0, keepdims=True), sink)
        p = jnp.exp2(sc - m)
        pn = p * (1.0 / (jnp.sum(p, axis=0, keepdims=True) + jnp.exp2(sink - m)))
        p_scr[n, 0:w] = jnp.where(own, pn, 0.0).astype(BF16)
        p_scr[n, w:2 * w] = jnp.where(own, 0.0, pn).astype(BF16)
    for u in range(nq):
        for kh in range(kv_heads):
            out_t = None
            for half in range(2):
                voff = (2 * kh + half) * LANES
                v_t = jnp.concatenate([v_refs[u + 1][0, voff:voff + LANES, :],
                                       v_refs[u][0, voff:voff + LANES, :]], axis=1)
                part = _dot(v_t, p_scr[(u * kv_heads + kh) * 2 + half])
                out_t = part if out_t is None else out_t + part
            for i in range(2):
                o_ref[0, u * w:(u + 1) * w, (2 * kh + i) * LANES:(2 * kh + i + 1) * LANES] = (
                    out_t[:, i * w:(i + 1) * w].T)


def _attn_prompt(layer, att_layer, x, mods, n1g, wq, qg, cos, sin, kx, vx, sinks, head_dim, kv_heads, w, nq):
    b, L, d = x.shape
    rows = lambda i, j: (i, j, 0)
    k_specs = [pl.BlockSpec((1, w, kx.shape[2]), lambda i, j, o=o: (i, jnp.maximum(nq * j + o, 0), 0))
               for o in range(-1, nq)]
    v_specs = [pl.BlockSpec((1, vx.shape[1], w), lambda i, j, o=o: (i, 0, jnp.maximum(nq * j + o, 0)))
               for o in range(-1, nq)]
    n_tiles = nq * 2 * kv_heads
    return pl.pallas_call(
        functools.partial(_attn_prompt_kernel, layer, att_layer, head_dim, kv_heads, nq),
        grid=(b, L // (nq * w)),
        in_specs=[pl.BlockSpec((1, nq * w, d), rows), _mod_spec(mods, layer),
                  _full(n1g.shape), _layer_spec(wq, att_layer), _full(qg.shape),
                  pl.BlockSpec((nq * w, LANES), lambda i, j: (j, 0)),
                  pl.BlockSpec((nq * w, LANES), lambda i, j: (j, 0))]
                 + k_specs + v_specs + [pl.BlockSpec(memory_space=pltpu.SMEM)],
        out_specs=pl.BlockSpec((1, nq * w, d), rows),
        out_shape=jax.ShapeDtypeStruct((b, L, d), F32),
        scratch_shapes=[pltpu.VMEM((n_tiles, w, 2 * w), F32), pltpu.VMEM((n_tiles, 2 * w, 2 * w), BF16)],
        compiler_params=_params("arbitrary", "arbitrary"),
        name="attn_prompt",
    )(x, mods, n1g, wq, qg, cos, sin, *([kx] * (nq + 1)), *([vx] * (nq + 1)), sinks)


def _proj_kernel(layer, x_ref, mod_ref, g_ref, w_ref, o_ref):
    d = x_ref.shape[1]
    mod = _mod_rows(mod_ref)
    h = _modulated_norm(x_ref[...], g_ref[layer:layer + 1, :], mod[:, 0:d], mod[:, d:2 * d])
    o_ref[...] = _dot(h.astype(BF16), w_ref[0])


def _proj(layer, x, mods, g, w, tn):
    r, d = x.shape
    n = w.shape[2]
    return pl.pallas_call(
        functools.partial(_proj_kernel, layer),
        grid=(n // tn,),
        in_specs=[_full((r, d)), _mod_spec(mods, layer), _full(g.shape),
                  pl.BlockSpec((1, d, tn), lambda j: (layer, 0, j))],
        out_specs=pl.BlockSpec((r, tn), lambda j: (0, j)),
        out_shape=jax.ShapeDtypeStruct((r, n), F32),
        compiler_params=_params("arbitrary"),
        name="proj_sample",
    )(x, mods, g, w)


def _q_sample_kernel(layer, att_layer, head_dim, x_ref, mod_ref, n1g_ref, wq_ref, qg_ref, cos_ref, sin_ref,
                     o_ref):
    blocks = _q_proj(x_ref[...], _mod_rows(mod_ref), n1g_ref[layer:layer + 1, :], wq_ref[0],
                     qg_ref[att_layer:att_layer + 1, :], cos_ref[...], sin_ref[...], head_dim)
    for j, blk in enumerate(blocks):
        o_ref[:, j * LANES:(j + 1) * LANES] = blk


def _q_sample(layer, att_layer, x, mods, n1g, wq, qg, cos, sin, head_dim):
    r, d = x.shape
    return pl.pallas_call(
        functools.partial(_q_sample_kernel, layer, att_layer, head_dim),
        grid=(1,),
        in_specs=[_full((r, d)), _mod_spec(mods, layer), _full(n1g.shape), _layer_spec(wq, att_layer),
                  _full(qg.shape), _full((1, LANES)), _full((1, LANES))],
        out_specs=_full((r, wq.shape[2])),
        out_shape=jax.ShapeDtypeStruct((r, wq.shape[2]), F32),
        compiler_params=_params("arbitrary"),
        name="q_sample",
    )(x, mods, n1g, wq, qg, cos, sin)


def _hgrn_sample_kernel(layer, heads, tb, steps, aliased, proj_ref, lbraw_ref, gng_ref, s_ref, *rest):
    o_ref, so_ref, stack_scr, acc_scr = rest[1:] if aliased else rest
    i = pl.program_id(0)
    rows = proj_ref.shape[0]
    d = proj_ref.shape[1] // 4
    dk = d // heads

    def update():
        @pl.when(i == 0)
        def _():
            lb = _lower_bound(lbraw_ref[...], layer)
            q, fg = _hgrn_gates(proj_ref[:, 0:2 * d], lb, d)
            for hh in range(heads):
                sl = slice(hh * dk, (hh + 1) * dk)
                ft = fg[:, sl].T
                hi = ft.astype(BF16)
                stack_scr[hh, 0:dk] = hi
                stack_scr[hh, dk:2 * dk] = (ft - hi.astype(F32)).astype(BF16)
                stack_scr[hh, 2 * dk:3 * dk] = (1.0 - ft).astype(BF16)
                stack_scr[hh, 3 * dk:4 * dk] = q[:, sl].T.astype(BF16)

        token_row = lax.broadcasted_iota(jnp.int32, (rows, dk), 0)
        sub = lax.broadcasted_iota(jnp.int32, (tb, dk), 0)
        base = pl.multiple_of(i * tb, tb)
        v_rows = proj_ref[pl.ds(base, tb), 2 * d:3 * d]
        o_rows = [jnp.zeros((tb, dk), F32) for _ in range(heads)]
        for t0 in range(0, tb, 2):
            onehot = jnp.concatenate([jnp.where(token_row == base + t, 1.0, 0.0) for t in (t0, t0 + 1)],
                                     axis=1).astype(BF16)
            for hh in range(heads):
                bc2 = _dot(stack_scr[hh], onehot)
                for t in (t0, t0 + 1):
                    bc = bc2[:, (t - t0) * dk:(t - t0 + 1) * dk]
                    f_b = bc[0:dk] + bc[dk:2 * dk]
                    k_b = bc[2 * dk:3 * dk]
                    q_b = bc[3 * dk:4 * dk]
                    s_new = f_b * s_ref[0, t, hh] + k_b * v_rows[t:t + 1, hh * dk:(hh + 1) * dk]
                    so_ref[0, t, hh] = s_new
                    o_rows[hh] = jnp.where(sub == t, jnp.sum(q_b * s_new, axis=0, keepdims=True), o_rows[hh])
        for hh in range(heads):
            acc_scr[pl.ds(base, tb), hh * dk:(hh + 1) * dk] = o_rows[hh]

        @pl.when(i == steps - 1)
        def _():
            gng = gng_ref[layer:layer + 1, :]
            for hh in range(heads):
                sl = slice(hh * dk, (hh + 1) * dk)
                o_ref[:, sl] = _gated_group_norm(acc_scr[:, sl], gng,
                                                 proj_ref[:, 3 * d + hh * dk:3 * d + (hh + 1) * dk])

    if aliased:
        update()
    else:
        pl.when(i < steps)(update)

        @pl.when(i >= steps)
        def _():
            so_ref[...] = jnp.zeros_like(so_ref)


def _hgrn_sample(layer, proj, lb_raw, gn_g, state, new_state, tb):
    n_layers, nb, heads, dk, dv = state.shape
    d = proj.shape[1] // 4
    steps = nb // tb
    aliased = new_state is not None
    assert aliased == (layer > 0) and tb == SUBLANES
    if aliased:
        grid = (steps,)
        s_in = s_out = pl.BlockSpec((1, tb, heads, dk, dv), lambda i: (layer, i, 0, 0, 0))
    else:
        grid = (n_layers * steps,)
        s_in = pl.BlockSpec((1, tb, heads, dk, dv), lambda i: (0, jnp.minimum(i, steps - 1), 0, 0, 0))
        s_out = pl.BlockSpec((1, tb, heads, dk, dv), lambda i: (i // steps, i % steps, 0, 0, 0))
    in_specs = [_full(proj.shape), _full(lb_raw.shape), _full(gn_g.shape), s_in]
    args = [proj, lb_raw, gn_g, state]
    if aliased:
        in_specs.append(pl.BlockSpec(memory_space=pl.ANY))
        args.append(new_state)
    return pl.pallas_call(
        functools.partial(_hgrn_sample_kernel, layer, heads, tb, steps, aliased),
        grid=grid,
        in_specs=in_specs,
        out_specs=[_full((nb, d)), s_out],
        out_shape=[jax.ShapeDtypeStruct((nb, d), F32), jax.ShapeDtypeStruct(state.shape, F32)],
        scratch_shapes=[pltpu.VMEM((heads, 4 * dk, nb), BF16), pltpu.VMEM((nb, d), F32)],
        input_output_aliases={4: 1} if aliased else {},
        compiler_params=_params("arbitrary"),
        name=f"hgrn_sample_{layer}",
    )(*args)


def _cache_roll_kernel(ck_ref, cv_ref, kn_ref, vn_ref, ko_ref, vo_ref, kb_ref, vb_ref):
    w = ck_ref.shape[1]
    ko_ref[:, 0:w - 1, :] = ck_ref[:, 1:w, :]
    ko_ref[:, w - 1:w, :] = kn_ref[...]
    vo_ref[:, 0:w - 1, :] = cv_ref[:, 1:w, :]
    vo_ref[:, w - 1:w, :] = vn_ref[...]
    kb_ref[...] = ko_ref[...].astype(BF16)
    vb_ref[...] = vo_ref[...].astype(BF16)


def _cache_roll(ck, cv, kn, vn, tb):
    nb, w, n = ck.shape
    blk = pl.BlockSpec((tb, w, n), lambda i: (i, 0, 0))
    new = pl.BlockSpec((tb, 1, n), lambda i: (i, 0, 0))
    return pl.pallas_call(
        _cache_roll_kernel,
        grid=(nb // tb,),
        in_specs=[blk, blk, new, new],
        out_specs=[blk, blk, blk, blk],
        out_shape=[jax.ShapeDtypeStruct(ck.shape, F32), jax.ShapeDtypeStruct(cv.shape, F32),
                   jax.ShapeDtypeStruct(ck.shape, BF16), jax.ShapeDtypeStruct(cv.shape, BF16)],
        compiler_params=_params("arbitrary"),
        name="cache_roll",
    )(ck, cv, kn.reshape(nb, 1, n), vn.reshape(nb, 1, n))


def _attn_sample_kernel(att_layer, tb, q_ref, k_ref, v_ref, sink_ref, o_ref):
    heads, head_dim = q_ref.shape[1], q_ref.shape[2]
    nkv = k_ref.shape[2]
    group = heads // (nkv // head_dim)
    r = lax.broadcasted_iota(jnp.int32, (heads, nkv), 0) // group
    c = lax.broadcasted_iota(jnp.int32, (heads, nkv), 1) // head_dim
    own = r == c
    sink = sink_ref[:, att_layer:att_layer + 1][None]
    q = q_ref[...]
    qe = jnp.where(own[None], jnp.concatenate([q] * (nkv // head_dim), axis=-1), 0.0).astype(BF16)
    s = jnp.einsum("thc,tjc->thj", qe, k_ref[...], preferred_element_type=F32)
    m = jnp.maximum(jnp.max(s, axis=-1, keepdims=True), sink)
    p = jnp.exp(s - m)
    denom = jnp.sum(p, axis=-1, keepdims=True) + jnp.exp(sink - m)
    pv = jnp.where(own[None], jnp.einsum("thj,tjc->thc", p.astype(BF16), v_ref[...],
                                         preferred_element_type=F32), 0.0)
    o = pv[:, :, 0:head_dim]
    for j in range(1, nkv // head_dim):
        o = o + pv[:, :, j * head_dim:(j + 1) * head_dim]
    o_ref[...] = o / denom


def _attn_sample(att_layer, q3, k, v, sinks_t, tb):
    nb, heads, head_dim = q3.shape
    w, nkv = k.shape[1], k.shape[2]
    return pl.pallas_call(
        functools.partial(_attn_sample_kernel, att_layer, tb),
        grid=(nb // tb,),
        in_specs=[pl.BlockSpec((tb, heads, head_dim), lambda i: (i, 0, 0)),
                  pl.BlockSpec((tb, w, nkv), lambda i: (i, 0, 0)),
                  pl.BlockSpec((tb, w, nkv), lambda i: (i, 0, 0)),
                  _full(sinks_t.shape)],
        out_specs=pl.BlockSpec((tb, heads, head_dim), lambda i: (i, 0, 0)),
        out_shape=jax.ShapeDtypeStruct(q3.shape, F32),
        compiler_params=_params("arbitrary"),
        name="attn_sample",
    )(q3, k, v, sinks_t)


def _rope_tables(pos, head_dim):
    half = head_dim // 2
    inv = ROPE_THETA ** (-jnp.arange(half, dtype=F32) / half)
    ang = pos.astype(F32)[:, None] * inv[None, :]
    cos, sin = jnp.cos(ang), jnp.sin(ang)
    reps = LANES // head_dim
    return (jnp.tile(jnp.concatenate([cos, cos], axis=1), (1, reps)),
            jnp.tile(jnp.concatenate([-sin, sin], axis=1), (1, reps)))


def kernel(x_prompt, x_sample, c_prompt, c_sample, state_hgrn, cache_k, cache_v, w_ada, b_ada, norm1_g, norm2_g, hg_w_in, hg_w_out, hg_lower_bounds, hg_gn_g, kv_w_ada, kv_b_ada, kv_norm_g, w_kv, k_norm_g, w_q, q_norm_g, sinks, w_o, w_up, w_down):
    bp, Lp, d = x_prompt.shape
    bs = x_sample.shape[0]
    depth = w_ada.shape[0]
    n_a = hg_w_in.shape[0]
    heads = state_hgrn.shape[2]
    window, kv_heads, head_dim = cache_k.shape[1], cache_k.shape[2], cache_k.shape[3]
    assert LANES == 2 * head_dim and kv_heads % 2 == 0 and Lp % window == 0
    assert window & (window - 1) == 0 and (d // head_dim) // kv_heads == 4
    nkv = kv_heads * head_dim
    reps = LANES // head_dim
    bf = lambda t: t.astype(BF16)

    pad = (-(bs + bp)) % 16
    c_all = jnp.concatenate([c_sample, c_prompt, jnp.zeros((pad, d), F32)], axis=0)
    mods_s, mods_p = _ada(c_all, bs, bp, w_ada, b_ada, T.ada_cols)
    kv_mods_s, kv_mods_p = _ada(c_all, bs, bp, kv_w_ada[None], kv_b_ada[None], T.kv_ada_cols)

    cos_p, sin_p = _rope_tables(jnp.arange(Lp), head_dim)
    cos_s, sin_s = _rope_tables(jnp.full((1,), PAST_LEN), head_dim)
    kg = jnp.tile(k_norm_g, reps)[None]
    qg = jnp.tile(q_norm_g, (1, reps))
    sinks_t = sinks.T
    w_kv_b, w_in_b, w_out_b, w_q_b, w_o_b = bf(w_kv), bf(hg_w_in), bf(hg_w_out), bf(w_q), bf(w_o)
    w_up_b, w_dn_b = bf(w_up), bf(w_down)

    xp = x_prompt
    xs = x_sample.reshape(bs, d)
    hg_p, hg_s = [], None
    k_p = v_p = kx_p = vx_p = k_s = v_s = None
    for l in range(depth):
        if l == n_a:
            k_p, v_p, kx_p, vx_p = _kv(xp, kv_mods_p, kv_norm_g[None], w_kv_b, kg, cos_p, sin_p,
                                       head_dim, window, T.kv_rows)
            k_n, v_n, _, _ = _kv(xs[None], kv_mods_s, kv_norm_g[None], w_kv_b, kg, cos_s, sin_s,
                                 head_dim, bs, bs)
            k_s, v_s, kb_s, vb_s = _cache_roll(cache_k.reshape(bs, window, nkv),
                                               cache_v.reshape(bs, window, nkv), k_n[0], v_n[0], T.cache_tokens)
        if l < n_a:
            w_mix_b, mix_layer = w_out_b, l
            a_p, s_p = _hgrn_prompt(l, xp, mods_p, norm1_g, w_in_b, hg_lower_bounds, hg_gn_g, heads,
                                    T.hgrn_chunk, T.hgrn_rows)
            hg_p.append(s_p)
            proj_s = _proj(l, xs, mods_s, norm1_g, w_in_b, T.proj_cols)
            a_s, hg_s = _hgrn_sample(l, proj_s, hg_lower_bounds, hg_gn_g, state_hgrn, hg_s, SUBLANES)
        else:
            j = l - n_a
            w_mix_b, mix_layer = w_o_b, j
            a_p = _attn_prompt(l, j, xp, mods_p, norm1_g, w_q_b, qg, cos_p, sin_p, kx_p, vx_p, sinks,
                               head_dim, kv_heads, window, T.attn_blocks)
            q_s = _q_sample(l, j, xs, mods_s, norm1_g, w_q_b, qg, cos_s, sin_s, head_dim)
            a_s = _attn_sample(j, q_s.reshape(bs, d // head_dim, head_dim), kb_s, vb_s, sinks_t,
                               T.attn_sample_tokens)
            a_s = a_s.reshape(bs, d)
        xp = _post(l, xp, a_p, mods_p, norm2_g, w_mix_b, mix_layer, w_up_b, w_dn_b, T.mlp_rows)
        xs = _post_sample(l, xs, a_s, mods_s, norm2_g, w_mix_b, mix_layer, w_up_b, w_dn_b, T.mlp_sample_slab)

    shape4 = lambda t: t.reshape(t.shape[0], window, kv_heads, head_dim)
    return (xp, xs.reshape(bs, 1, d), jnp.stack(hg_p), shape4(k_p), shape4(v_p), hg_s,
            shape4(k_s), shape4(v_s))
```
